```python
import jax, jax.numpy as jnp
from jax import lax
import numpy as np

D_MODEL = 4096
BATCH = 2
SEQ = 4096
DEPTH = 2

ROPE_THETA = 10000.0
NORM_EPS = 1e-6
NEG_INF = -1e30

A_HEADS = 16
A_HEAD_DIM = 128
MOBA_BLOCK = 256
MOBA_TOPK = 3
MOBA_Q_CHUNK = 16
B_HEADS = 32
B_KV_HEADS = 4
B_HEAD_DIM = 64
SWA_WINDOW = 128
C_HEADS = 16
C_Q_RANK = 1024
C_KV_RANK = 512
C_NOPE_DIM = 128
C_ROPE_DIM = 64
C_V_DIM = 128
C_Q_BLOCK = 128
N_GROUPS = 8
EXPERTS_PER_GROUP = 4
N_EXPERTS = N_GROUPS * EXPERTS_PER_GROUP
EXPERT_TOPK = 2
D_FF_EXPERT = 768
MOE_BLOCK = 128

A_WIDTH = A_HEADS * A_HEAD_DIM
B_WIDTH = B_HEADS * B_HEAD_DIM
B_KV_WIDTH = B_KV_HEADS * B_HEAD_DIM
C_WIDTH = C_HEADS * C_V_DIM
IN_SIZES = (A_WIDTH, A_WIDTH, A_WIDTH, B_WIDTH, B_KV_WIDTH, B_KV_WIDTH,
            C_Q_RANK, C_KV_RANK, C_ROPE_DIM, D_MODEL, D_MODEL, D_MODEL)
N_IN = sum(IN_SIZES)

kernel_name = 'hybrid_moba_swa_mla_hier_moe'


def rmsnorm(x, g):
    xf = x.astype(jnp.float32)
    y = xf * lax.rsqrt(jnp.mean(xf * xf, axis=-1, keepdims=True) + NORM_EPS)
    return (y * g.astype(jnp.float32)).astype(x.dtype)


def rope_tables(positions, dim):
    inv_freq = ROPE_THETA ** (-jnp.arange(0, dim, 2, dtype=jnp.float32) / dim)
    ang = positions.astype(jnp.float32)[:, None, :, None] * inv_freq
    return jnp.cos(ang), jnp.sin(ang)


def apply_rope(x, cos, sin):
    x1, x2 = jnp.split(x, 2, axis=-1)
    c = cos.astype(x.dtype)
    s = sin.astype(x.dtype)
    return jnp.concatenate([x1 * c - x2 * s, x1 * s + x2 * c], axis=-1)


def split_heads(t, n_heads):
    b, s, _ = t.shape
    return t.reshape(b, s, n_heads, -1).transpose(0, 2, 1, 3)


def merge_heads(t):
    b, h, s, d = t.shape
    return t.transpose(0, 2, 1, 3).reshape(b, s, h * d)


def moba_attention(q, k, v):
    b, h, s, d = q.shape
    n_blk = -(-s // MOBA_BLOCK)
    s_pad = n_blk * MOBA_BLOCK
    pad = ((0, 0), (0, 0), (0, s_pad - s), (0, 0))
    kb = jnp.pad(k, pad).reshape(b, h, n_blk, MOBA_BLOCK, d)
    vb = jnp.pad(v, pad).reshape(b, h, n_blk, MOBA_BLOCK, d)
    k_mean = jnp.mean(kb.astype(jnp.float32), axis=3)
    topk = min(MOBA_TOPK, n_blk)
    scale = d ** -0.5
    n_chunks = s // MOBA_Q_CHUNK
    q_chunks = q.reshape(b, h, n_chunks, MOBA_Q_CHUNK, d).transpose(2, 0, 1, 3, 4)
    bi = jnp.arange(b)[:, None, None, None]
    hi = jnp.arange(h)[None, :, None, None]
    blk_ids = jnp.arange(n_blk)
    key_off = jnp.arange(MOBA_BLOCK)

    def chunk(args):
        qc, c = args
        q_pos = c * MOBA_Q_CHUNK + jnp.arange(MOBA_Q_CHUNK)
        own = (c * MOBA_Q_CHUNK) // MOBA_BLOCK
        gate = jnp.einsum('bhqd,bhnd->bhqn', qc.astype(jnp.float32), k_mean)
        gate = jnp.where(blk_ids < own, gate, NEG_INF)
        _, idx = lax.top_k(gate, topk)
        valid = idx < own
        k_sel = kb[bi, hi, idx]
        v_sel = vb[bi, hi, idx]
        s_sel = jnp.einsum('bhqd,bhqnkd->bhqnk', qc, k_sel,
                           preferred_element_type=jnp.float32) * scale
        s_sel = jnp.where(valid[..., None], s_sel, NEG_INF)
        k_own = lax.dynamic_index_in_dim(kb, own, axis=2, keepdims=False)
        v_own = lax.dynamic_index_in_dim(vb, own, axis=2, keepdims=False)
        s_own = jnp.einsum('bhqd,bhkd->bhqk', qc, k_own,
                           preferred_element_type=jnp.float32) * scale
        s_own = jnp.where((own * MOBA_BLOCK + key_off)[None, :] <= q_pos[:, None], s_own, NEG_INF)
        logits = jnp.concatenate(
            [s_sel.reshape(b, h, MOBA_Q_CHUNK, topk * MOBA_BLOCK), s_own], axis=-1)
        p = jax.nn.softmax(logits, axis=-1).astype(v.dtype)
        p_sel = p[..., :topk * MOBA_BLOCK].reshape(b, h, MOBA_Q_CHUNK, topk, MOBA_BLOCK)
        p_own = p[..., topk * MOBA_BLOCK:]
        return (jnp.einsum('bhqnk,bhqnkd->bhqd', p_sel, v_sel)
                + jnp.einsum('bhqk,bhkd->bhqd', p_own, v_own))

    out = lax.map(chunk, (q_chunks, jnp.arange(n_chunks)))
    return out.transpose(1, 2, 0, 3, 4).reshape(b, h, s, d)


def swa_sink_attention(q, k, v, sinks):
    b, hq, s, d = q.shape
    g = k.shape[1]
    r = hq // g
    w = SWA_WINDOW
    nb = s // w
    qb = q.reshape(b, g, r, nb, w, d)
    kb = k.reshape(b, g, nb, w, d)
    vb = v.reshape(b, g, nb, w, d)
    blk_pad = ((0, 0), (0, 0), (1, 0), (0, 0), (0, 0))
    kk = jnp.concatenate([jnp.pad(kb, blk_pad)[:, :, :-1], kb], axis=3)
    vv = jnp.concatenate([jnp.pad(vb, blk_pad)[:, :, :-1], vb], axis=3)
    sc = jnp.einsum('bgrnqd,bgnkd->bgrnqk', qb, kk,
                    preferred_element_type=jnp.float32) * (d ** -0.5)
    rel = (w + jnp.arange(w))[:, None] - jnp.arange(2 * w)[None, :]
    band = (rel >= 0) & (rel < w)
    key_exists = (jnp.arange(nb)[:, None] * w - w + jnp.arange(2 * w)[None, :]) >= 0
    mask = band[None, :, :] & key_exists[:, None, :]
    sc = jnp.where(mask, sc, NEG_INF)
    sink = jnp.broadcast_to(sinks.astype(jnp.float32).reshape(1, g, r, 1, 1, 1),
                            sc.shape[:-1] + (1,))
    p = jax.nn.softmax(jnp.concatenate([sc, sink], axis=-1), axis=-1)[..., :-1]
    out = jnp.einsum('bgrnqk,bgnkd->bgrnqd', p.astype(v.dtype), vv)
    return out.reshape(b, hq, s, d)


def mla_attention(c_q, c_kv, k_pe_in, q_norm_g, kv_norm_g, wq_b, wkv_b, cos, sin):
    b, s, _ = c_q.shape
    q = split_heads(rmsnorm(c_q, q_norm_g) @ wq_b, C_HEADS)
    q_nope = q[..., :C_NOPE_DIM]
    q_pe = apply_rope(q[..., C_NOPE_DIM:], cos, sin)
    kv = split_heads(rmsnorm(c_kv, kv_norm_g) @ wkv_b, C_HEADS)
    k_nope = kv[..., :C_NOPE_DIM]
    v = kv[..., C_NOPE_DIM:]
    k_pe = apply_rope(k_pe_in[:, None], cos, sin)[:, 0]
    scale = (C_NOPE_DIM + C_ROPE_DIM) ** -0.5
    nq = s // C_Q_BLOCK
    qn_b = q_nope.reshape(b, C_HEADS, nq, C_Q_BLOCK, C_NOPE_DIM).transpose(2, 0, 1, 3, 4)
    qp_b = q_pe.reshape(b, C_HEADS, nq, C_Q_BLOCK, C_ROPE_DIM).transpose(2, 0, 1, 3, 4)
    key_pos = jnp.arange(s)

    def block(args):
        qn, qp, i = args
        sc = (jnp.einsum('bhqd,bhkd->bhqk', qn, k_nope, preferred_element_type=jnp.float32)
              + jnp.einsum('bhqd,bkd->bhqk', qp, k_pe, preferred_element_type=jnp.float32)) * scale
        q_pos = i * C_Q_BLOCK + jnp.arange(C_Q_BLOCK)
        sc = jnp.where(key_pos[None, :] <= q_pos[:, None], sc, NEG_INF)
        p = jax.nn.softmax(sc, axis=-1).astype(v.dtype)
        return jnp.einsum('bhqk,bhkd->bhqd', p, v)

    out = lax.map(block, (qn_b, qp_b, jnp.arange(nq)))
    return out.transpose(1, 2, 0, 3, 4).reshape(b, C_HEADS, s, C_V_DIM)


def mixer_layer(hn, cos_a, sin_a, cos_b, sin_b, cos_c, sin_c, w_in, q_norm_g, kv_norm_g,
                wq_b, wkv_b, sinks, w_out_a, w_out_b, w_out_c, w_o):
    z = hn @ w_in
    offsets = [int(o) for o in np.cumsum(IN_SIZES)[:-1]]
    qa, ka, va, qb, kb, vb, cq, ckv, kpe, ga, gb, gc = jnp.split(z, offsets, axis=-1)
    oa = moba_attention(apply_rope(split_heads(qa, A_HEADS), cos_a, sin_a),
                        apply_rope(split_heads(ka, A_HEADS), cos_a, sin_a),
                        split_heads(va, A_HEADS))
    ob = swa_sink_attention(apply_rope(split_heads(qb, B_HEADS), cos_b, sin_b),
                            apply_rope(split_heads(kb, B_KV_HEADS), cos_b, sin_b),
                            split_heads(vb, B_KV_HEADS), sinks)
    oc = mla_attention(cq, ckv, kpe, q_norm_g, kv_norm_g, wq_b, wkv_b, cos_c, sin_c)
    y = (jax.nn.sigmoid(ga) * (merge_heads(oa) @ w_out_a)
         + jax.nn.sigmoid(gb) * (merge_heads(ob) @ w_out_b)
         + jax.nn.sigmoid(gc) * (merge_heads(oc) @ w_out_c))
    return y @ w_o


def hier_moe(hn, w_group, b_group, w_expert, b_expert, w_gate, w_up, w_down):
    b, s, d = hn.shape
    t = b * s
    xt = hn.reshape(t, d)
    tok = jnp.arange(t)
    g_logits = (xt @ w_group).astype(jnp.float32) + b_group.astype(jnp.float32)
    g_prob = jax.nn.softmax(g_logits, axis=-1)
    g_sel = jnp.argmax(g_logits, axis=-1)
    p_g = g_prob[tok, g_sel][:, None]
    e_logits = ((xt @ w_expert).astype(jnp.float32)
                + b_expert.astype(jnp.float32)).reshape(t, N_GROUPS, EXPERTS_PER_GROUP)
    e_prob = jax.nn.softmax(e_logits[tok, g_sel], axis=-1)
    top_p, top_local = lax.top_k(e_prob, EXPERT_TOPK)
    weights = p_g * top_p / jnp.sum(top_p, axis=-1, keepdims=True)
    expert_id = g_sel[:, None] * EXPERTS_PER_GROUP + top_local
    tk = t * EXPERT_TOPK
    flat_e = expert_id.reshape(tk).astype(jnp.int32)
    flat_tok = jnp.repeat(jnp.arange(t, dtype=jnp.int32), EXPERT_TOPK)
    flat_w = weights.reshape(tk)
    order = jnp.argsort(flat_e)
    e_sorted = flat_e[order]
    counts = jnp.bincount(flat_e, length=N_EXPERTS)
    padded = (counts + MOE_BLOCK - 1) // MOE_BLOCK * MOE_BLOCK
    pad_end = jnp.cumsum(padded)
    pad_start = pad_end - padded
    start = jnp.cumsum(counts) - counts
    dest = pad_start[e_sorted] + jnp.arange(tk) - start[e_sorted]
    n_blocks = -(-tk // MOE_BLOCK) + N_EXPERTS
    n_buf = n_blocks * MOE_BLOCK
    buf_tok = jnp.zeros((n_buf,), jnp.int32).at[dest].set(flat_tok[order])
    buf_w = jnp.zeros((n_buf,), jnp.float32).at[dest].set(flat_w[order])
    block_expert = jnp.minimum(
        jnp.searchsorted(pad_end, jnp.arange(n_blocks) * MOE_BLOCK, side='right'), N_EXPERTS - 1)
    xb = xt[buf_tok].reshape(n_blocks, MOE_BLOCK, d)

    def expert_block(args):
        xblk, e = args
        return (jax.nn.silu(xblk @ w_gate[e]) * (xblk @ w_up[e])) @ w_down[e]

    yb = lax.map(expert_block, (xb, block_expert)).reshape(n_buf, d)
    out = jnp.zeros((t, d), hn.dtype).at[buf_tok].add(yb * buf_w[:, None].astype(hn.dtype))
    return out.reshape(b, s, d)


def setup_inputs(seed: int = 0) -> dict:
    key = jax.random.key(seed)
    ks = jax.random.split(key, 24)
    f32 = jnp.float32
    L, D = DEPTH, D_MODEL

    def normal(k, shape, scale):
        return jax.random.normal(k, shape, f32) * scale

    def gain(k, shape):
        return 1.0 + 0.02 * jax.random.normal(k, shape, f32)

    offset = jax.random.randint(ks[1], (BATCH, 1), 0, 1024, dtype=jnp.int32)
    positions = offset + jnp.arange(SEQ, dtype=jnp.int32)[None, :]
    return {
        'x': normal(ks[0], (BATCH, SEQ, D), 1.0),
        'positions': positions,
        'attn_norm_g': gain(ks[2], (L, D)),
        'w_in': normal(ks[3], (L, D, N_IN), D ** -0.5),
        'q_norm_g': gain(ks[4], (L, C_Q_RANK)),
        'kv_norm_g': gain(ks[5], (L, C_KV_RANK)),
        'wq_b': normal(ks[6], (L, C_Q_RANK, C_HEADS * (C_NOPE_DIM + C_ROPE_DIM)), C_Q_RANK ** -0.5),
        'wkv_b': normal(ks[7], (L, C_KV_RANK, C_HEADS * (C_NOPE_DIM + C_V_DIM)), C_KV_RANK ** -0.5),
        'sinks': normal(ks[8], (L, B_HEADS), 0.5),
        'w_out_a': normal(ks[9], (L, A_WIDTH, D), A_WIDTH ** -0.5),
        'w_out_b': normal(ks[10], (L, B_WIDTH, D), B_WIDTH ** -0.5),
        'w_out_c': normal(ks[11], (L, C_WIDTH, D), C_WIDTH ** -0.5),
        'w_o': normal(ks[12], (L, D, D), D ** -0.5),
        'ffn_norm_g': gain(ks[13], (L, D)),
        'w_group': normal(ks[14], (L, D, N_GROUPS), D ** -0.5),
        'b_group': normal(ks[15], (L, N_GROUPS), 0.01),
        'w_expert': normal(ks[16], (L, D, N_EXPERTS), D ** -0.5),
        'b_expert': normal(ks[17], (L, N_EXPERTS), 0.01),
        'w_gate': normal(ks[18], (L, N_EXPERTS, D, D_FF_EXPERT), D ** -0.5),
        'w_up': normal(ks[19], (L, N_EXPERTS, D, D_FF_EXPERT), D ** -0.5),
        'w_down': normal(ks[20], (L, N_EXPERTS, D_FF_EXPERT, D), D_FF_EXPERT ** -0.5),
        'final_norm_g': gain(ks[21], (D,)),
    }


def reference(x, positions, attn_norm_g, w_in, q_norm_g, kv_norm_g, wq_b, wkv_b, sinks,
              w_out_a, w_out_b, w_out_c, w_o, ffn_norm_g, w_group, b_group, w_expert,
              b_expert, w_gate, w_up, w_down, final_norm_g):
    cos_a, sin_a = rope_tables(positions, A_HEAD_DIM)
    cos_b, sin_b = rope_tables(positions, B_HEAD_DIM)
    cos_c, sin_c = rope_tables(positions, C_ROPE_DIM)
    h = x
    for l in range(DEPTH):
        h = h + mixer_layer(rmsnorm(h, attn_norm_g[l]), cos_a, sin_a, cos_b, sin_b, cos_c, sin_c,
                            w_in[l], q_norm_g[l], kv_norm_g[l], wq_b[l], wkv_b[l], sinks[l],
                            w_out_a[l], w_out_b[l], w_out_c[l], w_o[l])
        h = h + hier_moe(rmsnorm(h, ffn_norm_g[l]), w_group[l], b_group[l], w_expert[l],
                         b_expert[l], w_gate[l], w_up[l], w_down[l])
    return rmsnorm(h, final_norm_g)
```

```python
import functools

import jax
import jax.numpy as jnp
from jax import lax
from jax.experimental import pallas as pl
from jax.experimental.pallas import tpu as pltpu

F32 = jnp.float32
BF16 = jnp.bfloat16

ROPE_THETA = 10000.0
NORM_EPS = 1e-6
NEG_INF = -1e30
PICKED = -3e38

A_HEADS, A_HEAD_DIM = 16, 128
MOBA_BLOCK, MOBA_TOPK = 256, 3
B_HEADS, B_KV_HEADS, B_HEAD_DIM, SWA_WINDOW = 32, 4, 64, 128
C_HEADS, C_Q_RANK, C_KV_RANK, C_NOPE_DIM, C_ROPE_DIM, C_V_DIM = 16, 1024, 512, 128, 64, 128
N_GROUPS, EXPERTS_PER_GROUP, EXPERT_TOPK, D_FF_EXPERT = 8, 4, 2, 768
N_EXPERTS = N_GROUPS * EXPERTS_PER_GROUP
assert EXPERTS_PER_GROUP == 4

LANES = 128
VMEM_LIMIT_BYTES = 58 * 1024 * 1024

A_WIDTH = A_HEADS * A_HEAD_DIM
B_WIDTH = B_HEADS * B_HEAD_DIM
B_KV_WIDTH = B_KV_HEADS * B_HEAD_DIM
C_WIDTH = C_HEADS * C_V_DIM
OFF_QA = 0
OFF_KA = OFF_QA + A_WIDTH
OFF_VA = OFF_KA + A_WIDTH
OFF_QB = OFF_VA + A_WIDTH
OFF_KB = OFF_QB + B_WIDTH
OFF_VB = OFF_KB + B_KV_WIDTH
OFF_CQ = OFF_VB + B_KV_WIDTH
OFF_CKV = OFF_CQ + C_Q_RANK
OFF_KPE = OFF_CKV + C_KV_RANK
OFF_GATES = OFF_KPE + C_ROPE_DIM
Z1_WIDTH = 10752

MOE_ROWS = 256
MOE_FF_TILE = 256


def _params(*sem):
    return pltpu.CompilerParams(dimension_semantics=sem, vmem_limit_bytes=VMEM_LIMIT_BYTES)


def _rmsnorm_kernel(*refs, n_pieces, width):
    x_refs, g_ref, o_ref = refs[:n_pieces], refs[n_pieces], refs[n_pieces + 1]
    xs = [r[...].astype(F32) for r in x_refs]
    ss = sum(jnp.sum(x * x, axis=-1, keepdims=True) for x in xs)
    inv = lax.rsqrt(ss * (1.0 / width) + NORM_EPS)
    pw = xs[0].shape[1]
    for p, x in enumerate(xs):
        o_ref[:, p * pw:(p + 1) * pw] = (x * inv * g_ref[:, p * pw:(p + 1) * pw]).astype(o_ref.dtype)


def _rmsnorm(x, g, *, col_off=0, width=None, piece=None, tm=256, out_dtype=BF16):
    t = x.shape[0]
    width = width or x.shape[1]
    piece = piece or width
    n_pieces = width // piece
    off = col_off // piece
    assert col_off % piece == 0 and width % piece == 0
    in_specs = [pl.BlockSpec((tm, piece), functools.partial(lambda i, p: (i, off + p), p=p))
                for p in range(n_pieces)]
    in_specs.append(pl.BlockSpec((1, width), lambda i: (0, 0)))
    return pl.pallas_call(
        functools.partial(_rmsnorm_kernel, n_pieces=n_pieces, width=width),
        grid=(t // tm,),
        in_specs=in_specs,
        out_specs=pl.BlockSpec((tm, width), lambda i: (i, 0)),
        out_shape=jax.ShapeDtypeStruct((t, width), out_dtype),
        compiler_params=_params("parallel"),
        name="rmsnorm",
    )(*([x] * n_pieces), g.reshape(1, width))


def _matmul_kernel(*refs, cast_w, has_res):
    x_ref, w_ref = refs[0], refs[1]
    r_ref = refs[2] if has_res else None
    o_ref = refs[2 + has_res]
    if cast_w:
        wbf_ref = refs[3 + has_res]

        @pl.when(pl.program_id(1) == 0)
        def _():
            wbf_ref[...] = w_ref[...].astype(BF16)

        w = wbf_ref[...]
    else:
        w = w_ref[...]
    acc = jnp.dot(x_ref[...], w, preferred_element_type=F32)
    if has_res:
        acc = acc + r_ref[...]
    o_ref[...] = acc.astype(o_ref.dtype)


def _matmul(x, w, layer, *, n_out, col_off=0, tm=1024, tn=512, out_dtype=F32, residual=None, name="matmul"):
    m, k = x.shape
    assert w.shape[1] == k and m % tm == 0 and n_out % tn == 0 and col_off % tn == 0
    off = col_off // tn
    cast_w = w.dtype != BF16
    in_specs = [pl.BlockSpec((tm, k), lambda j, i: (i, 0)),
                pl.BlockSpec((None, k, tn), lambda j, i: (layer, 0, j + off))]
    args = [x, w]
    if residual is not None:
        in_specs.append(pl.BlockSpec((tm, tn), lambda j, i: (i, j)))
        args.append(residual)
    return pl.pallas_call(
        functools.partial(_matmul_kernel, cast_w=cast_w, has_res=residual is not None),
        grid=(n_out // tn, m // tm),
        in_specs=in_specs,
        out_specs=pl.BlockSpec((tm, tn), lambda j, i: (i, j)),
        out_shape=jax.ShapeDtypeStruct((m, n_out), out_dtype),
        scratch_shapes=[pltpu.VMEM((k, tn), BF16)] if cast_w else [],
        compiler_params=_params("arbitrary", "arbitrary"),
        name=name,
    )(*args)


def _rope_tables_kernel(pos_ref, invf_ref, sign_ref, keep_ref, c_ref, s_ref):
    ang = pos_ref[...] * invf_ref[...]
    c_ref[...] = jnp.cos(ang) * keep_ref[...]
    s_ref[...] = jnp.sin(ang) * sign_ref[...]


def _rope_tables(pos, dim, *, keep_lanes=LANES):
    t = pos.shape[0]
    half = dim // 2
    lane = jnp.arange(LANES)
    invf = (ROPE_THETA ** (-(2.0 * (lane % half)).astype(F32) / dim)).reshape(1, LANES)
    keep = (lane < keep_lanes).astype(F32).reshape(1, LANES)
    sign = jnp.where((lane % dim) < half, -1.0, 1.0).astype(F32).reshape(1, LANES) * keep
    tm = 1024
    row = pl.BlockSpec((1, LANES), lambda i: (0, 0))
    out = pl.BlockSpec((tm, LANES), lambda i: (i, 0))
    return pl.pallas_call(
        _rope_tables_kernel,
        grid=(t // tm,),
        in_specs=[pl.BlockSpec((tm, 1), lambda i: (i, 0)), row, row, row],
        out_specs=[out, out],
        out_shape=[jax.ShapeDtypeStruct((t, LANES), F32)] * 2,
        compiler_params=_params("parallel"),
        name="rope_tables",
    )(pos.reshape(t, 1), invf, sign, keep)


def _swap_halves(x, half):
    if half == 64:
        return pltpu.roll(x, 64, axis=1)
    lane = lax.broadcasted_iota(jnp.int32, x.shape, 1)
    return jnp.where((lane & half) == 0, pltpu.roll(x, LANES - half, axis=1), pltpu.roll(x, half, axis=1))


def _rope_kernel(x_ref, c_ref, s_ref, o_ref, *, half, rope_groups):
    n_groups = x_ref.shape[1] // LANES
    for k in range(n_groups):
        x = x_ref[:, k * LANES:(k + 1) * LANES].astype(F32)
        if half is not None and rope_groups[k % len(rope_groups)]:
            x = x * c_ref[...] + _swap_halves(x, half) * s_ref[...]
        o_ref[:, k * LANES:(k + 1) * LANES] = x.astype(o_ref.dtype)


def _rope(z, c, s, *, col_off, width, half, out_dtype, rope_groups=(True,), tm=512, bw=512):
    t = z.shape[0]
    bw = min(bw, width)
    assert width % bw == 0 and col_off % bw == 0 and bw % (LANES * len(rope_groups)) == 0
    off = col_off // bw
    tab = pl.BlockSpec((tm, LANES), lambda i, j: (i, 0))
    return pl.pallas_call(
        functools.partial(_rope_kernel, half=half, rope_groups=rope_groups),
        grid=(t // tm, width // bw),
        in_specs=[pl.BlockSpec((tm, bw), lambda i, j: (i, j + off)), tab, tab],
        out_specs=pl.BlockSpec((tm, bw), lambda i, j: (i, j)),
        out_shape=jax.ShapeDtypeStruct((t, width), out_dtype),
        compiler_params=_params("parallel", "parallel"),
        name="rope",
    )(z, c, s)


def _online_block(q, k, v, s_mask, carry, scale):
    m, l, acc = carry
    s = lax.dot_general(q, k, (((1,), (1,)), ((), ())), preferred_element_type=F32) * scale
    s = s_mask(s)
    m_new = jnp.maximum(m, jnp.max(s, axis=1, keepdims=True))
    alpha = jnp.exp(m - m_new)
    p = jnp.exp(s - m_new)
    l = alpha * l + jnp.sum(p, axis=1, keepdims=True)
    acc = alpha * acc + jnp.dot(p.astype(BF16), v, preferred_element_type=F32)
    return m_new, l, acc


def _causal_attend(qi, q, load_kv, tq, dv, scale, row_keep=None):
    row = lax.broadcasted_iota(jnp.int32, (tq, tq), 0)
    col = lax.broadcasted_iota(jnp.int32, (tq, tq), 1)
    init = (jnp.full((tq, 1), NEG_INF, F32), jnp.zeros((tq, 1), F32), jnp.zeros((tq, dv), F32))
    k, v = load_kv(qi)
    carry = _online_block(q, k, v, lambda s: jnp.where(col <= row, s, NEG_INF), init, scale)

    def past(j, carry):
        j = jnp.asarray(j, jnp.int32)
        k, v = load_kv(j)
        if row_keep is None:
            mask = lambda s: s
        else:
            keep = row_keep(j)
            mask = lambda s: jnp.where(keep, s, NEG_INF)
        return _online_block(q, k, v, mask, carry, scale)

    _, l, acc = lax.fori_loop(0, qi, past, carry)
    return acc / l


def _mla_kernel(q_ref, kn_ref, kpe_ref, v_ref, o_ref, *, tq, scale):
    n_q = q_ref.shape[0] // tq

    def load_kv(j):
        rows = pl.ds(pl.multiple_of(j * tq, tq), tq)
        return jnp.concatenate([kn_ref[rows, :], kpe_ref[rows, :]], axis=1), v_ref[rows, :]

    def q_block(qi, _):
        qi = jnp.asarray(qi, jnp.int32)
        rows = pl.ds(pl.multiple_of(qi * tq, tq), tq)
        o_ref[rows, :] = _causal_attend(qi, q_ref[rows, :], load_kv, tq, v_ref.shape[1], scale).astype(o_ref.dtype)
        return 0

    lax.fori_loop(0, n_q, q_block, 0)


def _mla_attention(q, kv, kpe, *, batch, seq, tq=256):
    t = q.shape[0]
    scale = (C_NOPE_DIM + C_ROPE_DIM) ** -0.5
    return pl.pallas_call(
        functools.partial(_mla_kernel, tq=tq, scale=scale),
        grid=(batch, C_HEADS),
        in_specs=[pl.BlockSpec((seq, 2 * LANES), lambda b, h: (b, h)),
                  pl.BlockSpec((seq, LANES), lambda b, h: (b, 2 * h)),
                  pl.BlockSpec((seq, LANES), lambda b, h: (b, 0)),
                  pl.BlockSpec((seq, LANES), lambda b, h: (b, 2 * h + 1))],
        out_specs=pl.BlockSpec((seq, C_V_DIM), lambda b, h: (b, h)),
        out_shape=jax.ShapeDtypeStruct((t, C_WIDTH), BF16),
        compiler_params=_params("parallel", "parallel"),
        name="mla_attention",
    )(q, kv, kpe, kv)


def _moba_kernel(q_ref, k_ref, v_ref, o_ref, kmean_ref, *, scale):
    tq = MOBA_BLOCK
    n_blk = q_ref.shape[0] // tq
    for n in range(n_blk):
        kmean_ref[n:n + 1, :] = jnp.mean(k_ref[n * tq:(n + 1) * tq, :], axis=0, keepdims=True)
    blk = lax.broadcasted_iota(jnp.int32, (tq, n_blk), 1).astype(F32)

    def load_kv(j):
        rows = pl.ds(pl.multiple_of(j * tq, tq), tq)
        return k_ref[rows, :].astype(BF16), v_ref[rows, :]

    def q_block(qi, _):
        qi = jnp.asarray(qi, jnp.int32)
        rows = pl.ds(pl.multiple_of(qi * tq, tq), tq)
        qf = q_ref[rows, :]
        gate = lax.dot_general(qf, kmean_ref[...], (((1,), (1,)), ((), ())),
                               precision=lax.Precision.HIGHEST, preferred_element_type=F32)
        own = qi.astype(F32)
        gate = jnp.where(blk < own, gate, NEG_INF)
        sel = jnp.zeros((tq, n_blk), F32)
        for _ in range(MOBA_TOPK):
            best = jnp.max(gate, axis=1, keepdims=True)
            idx = jnp.min(jnp.where(gate == best, blk, float(n_blk)), axis=1, keepdims=True)
            pick = blk == idx
            sel = jnp.where(pick & (blk < own), 1.0, sel)
            gate = jnp.where(pick, PICKED, gate)

        def row_keep(j):
            return jnp.sum(jnp.where(blk == j.astype(F32), sel, 0.0), axis=1, keepdims=True) > 0.5

        out = _causal_attend(qi, qf.astype(BF16), load_kv, tq, v_ref.shape[1], scale, row_keep)
        o_ref[rows, :] = out.astype(o_ref.dtype)
        return 0

    lax.fori_loop(0, n_blk, q_block, 0)


def _moba_attention(qk, v, *, batch, seq):
    t = qk.shape[0]
    assert seq % MOBA_BLOCK == 0
    return pl.pallas_call(
        functools.partial(_moba_kernel, scale=A_HEAD_DIM ** -0.5),
        grid=(batch, A_HEADS),
        in_specs=[pl.BlockSpec((seq, A_HEAD_DIM), lambda b, h: (b, h)),
                  pl.BlockSpec((seq, A_HEAD_DIM), lambda b, h: (b, A_HEADS + h)),
                  pl.BlockSpec((seq, A_HEAD_DIM), lambda b, h: (b, h))],
        out_specs=pl.BlockSpec((seq, A_HEAD_DIM), lambda b, h: (b, h)),
        out_shape=jax.ShapeDtypeStruct((t, A_WIDTH), BF16),
        scratch_shapes=[pltpu.VMEM((seq // MOBA_BLOCK, A_HEAD_DIM), F32)],
        compiler_params=_params("parallel", "parallel"),
        name="moba_attention",
    )(qk, qk, v)


def _swa_kernel(sinks_ref, q_ref, k_ref, v_ref, o_ref, *, scale):
    w = SWA_WINDOW
    n_blk = q_ref.shape[0] // w
    pair = pl.program_id(1)
    kv_odd = ((pair * 2) // (B_HEADS // B_KV_HEADS)) % 2
    lane = lax.broadcasted_iota(jnp.int32, (w, LANES), 1)
    low = lane < B_HEAD_DIM
    keep_orig = jnp.where(low, 0, 1) == kv_odd
    row = lax.broadcasted_iota(jnp.int32, (2 * w, 2 * w), 0)
    col = lax.broadcasted_iota(jnp.int32, (2 * w, 2 * w), 1)
    rel = (row & (w - 1)) + w - col
    band = (rel >= 0) & (rel < w)
    sink = jnp.where(lax.broadcasted_iota(jnp.int32, (2 * w, 1), 0) < w,
                     sinks_ref[2 * pair], sinks_ref[2 * pair + 1])

    def dup(x):
        xf = x.astype(F32)
        return jnp.where(keep_orig, xf, pltpu.roll(xf, B_HEAD_DIM, axis=1)).astype(BF16)

    def q_block(n, _):
        n = jnp.asarray(n, jnp.int32)
        cur = pl.ds(pl.multiple_of(n * w, w), w)
        prev = pl.ds(pl.multiple_of(jnp.maximum(n - 1, 0) * w, w), w)
        q = q_ref[cur, :]
        zero = jnp.zeros_like(q)
        q2 = jnp.concatenate([jnp.where(low, q, zero), jnp.where(low, zero, q)], axis=0)
        k2 = jnp.concatenate([dup(k_ref[prev, :]), dup(k_ref[cur, :])], axis=0)
        v2 = jnp.concatenate([dup(v_ref[prev, :]), dup(v_ref[cur, :])], axis=0)
        s = lax.dot_general(q2, k2, (((1,), (1,)), ((), ())), preferred_element_type=F32) * scale
        first_key_col = jnp.where(n > 0, 0, w)
        s = jnp.where(band & (col >= first_key_col), s, NEG_INF)
        m = jnp.maximum(jnp.max(s, axis=1, keepdims=True), sink)
        p = jnp.exp(s - m)
        denom = jnp.sum(p, axis=1, keepdims=True) + jnp.exp(sink - m)
        o2 = jnp.dot((p / denom).astype(BF16), v2, preferred_element_type=F32)
        o_ref[cur, :] = jnp.where(low, o2[:w], o2[w:]).astype(o_ref.dtype)
        return 0

    lax.fori_loop(0, n_blk, q_block, 0)


def _swa_attention(q, k, v, sinks, *, batch, seq):
    t = q.shape[0]
    pairs = B_HEADS // 2
    rep = B_HEADS // B_KV_HEADS
    kv_block = lambda b, p, sinks: (b, (2 * p) // rep // 2)
    return pl.pallas_call(
        functools.partial(_swa_kernel, scale=B_HEAD_DIM ** -0.5),
        grid_spec=pltpu.PrefetchScalarGridSpec(
            num_scalar_prefetch=1,
            grid=(batch, pairs),
            in_specs=[pl.BlockSpec((seq, LANES), lambda b, p, sinks: (b, p)),
                      pl.BlockSpec((seq, LANES), kv_block),
                      pl.BlockSpec((seq, LANES), kv_block)],
            out_specs=pl.BlockSpec((seq, LANES), lambda b, p, sinks: (b, p)),
        ),
        out_shape=jax.ShapeDtypeStruct((t, B_WIDTH), BF16),
        compiler_params=_params("parallel", "parallel"),
        name="swa_attention",
    )(sinks, q, k, v)


def _gated_out_kernel(oa_ref, ob_ref, oc_ref, ga_ref, gb_ref, gc_ref, wa_ref, wb_ref, wc_ref, y_ref,
                      wa_bf, wb_bf, wc_bf):
    @pl.when(pl.program_id(1) == 0)
    def _():
        wa_bf[...] = wa_ref[...].astype(BF16)
        wb_bf[...] = wb_ref[...].astype(BF16)
        wc_bf[...] = wc_ref[...].astype(BF16)

    y = jax.nn.sigmoid(ga_ref[...]) * jnp.dot(oa_ref[...], wa_bf[...], preferred_element_type=F32)
    y += jax.nn.sigmoid(gb_ref[...]) * jnp.dot(ob_ref[...], wb_bf[...], preferred_element_type=F32)
    y += jax.nn.sigmoid(gc_ref[...]) * jnp.dot(oc_ref[...], wc_bf[...], preferred_element_type=F32)
    y_ref[...] = y.astype(y_ref.dtype)


def _gated_out(oa, ob, oc, gates, w_out_a, w_out_b, w_out_c, layer, *, tm=512, tn=512):
    t = oa.shape[0]
    d = w_out_a.shape[2]
    nb = d // tn
    o_spec = lambda width: pl.BlockSpec((tm, width), lambda j, i: (i, 0))
    g_spec = lambda which: pl.BlockSpec((tm, tn), lambda j, i: (i, which * nb + j))
    w_spec = lambda width: pl.BlockSpec((None, width, tn), lambda j, i: (layer, 0, j))
    return pl.pallas_call(
        _gated_out_kernel,
        grid=(nb, t // tm),
        in_specs=[o_spec(A_WIDTH), o_spec(B_WIDTH), o_spec(C_WIDTH), g_spec(0), g_spec(1), g_spec(2),
                  w_spec(A_WIDTH), w_spec(B_WIDTH), w_spec(C_WIDTH)],
        out_specs=pl.BlockSpec((tm, tn), lambda j, i: (i, j)),
        out_shape=jax.ShapeDtypeStruct((t, d), BF16),
        scratch_shapes=[pltpu.VMEM((A_WIDTH, tn), BF16), pltpu.VMEM((B_WIDTH, tn), BF16),
                        pltpu.VMEM((C_WIDTH, tn), BF16)],
        compiler_params=_params("arbitrary", "arbitrary"),
        name="gated_out",
    )(oa, ob, oc, gates, gates, gates, w_out_a, w_out_b, w_out_c)


def _router_kernel(h_ref, g_ref, w_ref, b_ref, ids_ref, wts_ref):
    x = h_ref[...]
    hn = x * lax.rsqrt(jnp.mean(x * x, axis=-1, keepdims=True) + NORM_EPS) * g_ref[...]
    logits = jnp.dot(hn, w_ref[...], precision=lax.Precision.HIGHEST, preferred_element_type=F32) + b_ref[...]
    lane = lax.broadcasted_iota(jnp.int32, logits.shape, 1)
    far = float(LANES)
    is_g = lane < N_GROUPS
    g_id = lane.astype(F32)
    gmax = jnp.max(jnp.where(is_g, logits, NEG_INF), axis=1, keepdims=True)
    gsel = jnp.min(jnp.where(is_g & (logits == gmax), g_id, far), axis=1, keepdims=True)
    p_g = 1.0 / jnp.sum(jnp.where(is_g, jnp.exp(logits - gmax), 0.0), axis=1, keepdims=True)
    e_lane = lane - N_GROUPS
    e_id = e_lane.astype(F32)
    e_group = jnp.right_shift(e_lane, 2).astype(F32)
    in_grp = (e_lane >= 0) & (e_lane < N_EXPERTS) & (e_group == gsel)
    emax = jnp.max(jnp.where(in_grp, logits, NEG_INF), axis=1, keepdims=True)
    ee = jnp.where(in_grp, jnp.exp(jnp.where(in_grp, logits, emax) - emax), 0.0)
    ep = ee / jnp.sum(ee, axis=1, keepdims=True)
    p1 = jnp.max(jnp.where(in_grp, ep, -1.0), axis=1, keepdims=True)
    i1 = jnp.min(jnp.where(in_grp & (ep == p1), e_id, far), axis=1, keepdims=True)
    rest = in_grp & (e_id != i1)
    p2 = jnp.max(jnp.where(rest, ep, -1.0), axis=1, keepdims=True)
    i2 = jnp.min(jnp.where(rest & (ep == p2), e_id, far), axis=1, keepdims=True)
    tot = p1 + p2
    ids_ref[...] = jnp.where(lane == 0, i1, jnp.where(lane == 1, i2, 0.0)).astype(jnp.int32)
    wts_ref[...] = jnp.where(lane == 0, p_g * p1 / tot, jnp.where(lane == 1, p_g * p2 / tot, 0.0))


def _router(h, g, w_r, b_r, layer, *, tm=256):
    t, d = h.shape
    out = pl.BlockSpec((tm, LANES), lambda i: (i, 0))
    return pl.pallas_call(
        _router_kernel,
        grid=(t // tm,),
        in_specs=[pl.BlockSpec((tm, d), lambda i: (i, 0)),
                  pl.BlockSpec((None, 1, d), lambda i: (layer, 0, 0)),
                  pl.BlockSpec((None, d, LANES), lambda i: (layer, 0, 0)),
                  pl.BlockSpec((None, 1, LANES), lambda i: (layer, 0, 0))],
        out_specs=[out, out],
        out_shape=[jax.ShapeDtypeStruct((t, LANES), jnp.int32), jax.ShapeDtypeStruct((t, LANES), F32)],
        compiler_params=_params("parallel"),
        name="moe_router",
    )(h, g, w_r, b_r)


def _row_copy(src_hbm, tok, dst_ref, r, sem):
    return pltpu.make_async_copy(src_hbm.at[pl.ds(tok, 1), :], dst_ref.at[pl.ds(r, 1), :], sem)


def _dispatch_kernel(tok_ref, nvalid_ref, h_hbm, g_ref, o_ref, rows_ref, sem):
    i = pl.program_id(0)
    n_rows = rows_ref.shape[0]

    @pl.when(i < nvalid_ref[0])
    def _():
        base = i * n_rows

        def issue(r, _):
            _row_copy(h_hbm, tok_ref[base + r], rows_ref, r, sem).start()
            return 0

        lax.fori_loop(0, n_rows, issue, 0)

        def drain(r, _):
            _row_copy(h_hbm, 0, rows_ref, r, sem).wait()
            return 0

        lax.fori_loop(0, n_rows, drain, 0)
        x = rows_ref[...]
        hn = x * lax.rsqrt(jnp.mean(x * x, axis=-1, keepdims=True) + NORM_EPS) * g_ref[...]
        o_ref[...] = hn.astype(o_ref.dtype)


def _dispatch(h, g, layer, buf_tok, nvalid, *, n_blocks):
    t, d = h.shape
    blk = lambda i, tok, nv: (jnp.minimum(i, nv[0] - 1), 0)
    return pl.pallas_call(
        _dispatch_kernel,
        grid_spec=pltpu.PrefetchScalarGridSpec(
            num_scalar_prefetch=2,
            grid=(n_blocks,),
            in_specs=[pl.BlockSpec(memory_space=pl.ANY),
                      pl.BlockSpec((None, 1, d), lambda i, tok, nv: (layer, 0, 0))],
            out_specs=pl.BlockSpec((MOE_ROWS, d), blk),
            scratch_shapes=[pltpu.VMEM((MOE_ROWS, d), F32), pltpu.SemaphoreType.DMA(())],
        ),
        out_shape=jax.ShapeDtypeStruct((n_blocks * MOE_ROWS, d), BF16),
        compiler_params=_params("arbitrary"),
        name="moe_dispatch",
    )(buf_tok, nvalid, h, g)


def _expert_up_kernel(be_ref, nvalid_ref, x_ref, wg_ref, wu_ref, o_ref, wg_bf, wu_bf):
    i = pl.program_id(1)

    @pl.when(i < nvalid_ref[0])
    def _():
        @pl.when((i == 0) | (be_ref[i] != be_ref[jnp.maximum(i - 1, 0)]))
        def _():
            wg_bf[...] = wg_ref[...].astype(BF16)
            wu_bf[...] = wu_ref[...].astype(BF16)

        x = x_ref[...]
        hg = jnp.dot(x, wg_bf[...], preferred_element_type=F32)
        hu = jnp.dot(x, wu_bf[...], preferred_element_type=F32)
        o_ref[...] = (jax.nn.silu(hg) * hu).astype(o_ref.dtype)


def _expert_up(xb, w_gate, w_up, layer, block_expert, nvalid, *, n_blocks):
    d = xb.shape[1]
    ff = w_gate.shape[3]
    tf = MOE_FF_TILE
    row_blk = lambda f, i, be, nv: (jnp.minimum(i, nv[0] - 1), 0)
    w_spec = pl.BlockSpec((None, None, d, tf), lambda f, i, be, nv: (layer, be[i], 0, f))
    return pl.pallas_call(
        _expert_up_kernel,
        grid_spec=pltpu.PrefetchScalarGridSpec(
            num_scalar_prefetch=2,
            grid=(ff // tf, n_blocks),
            in_specs=[pl.BlockSpec((MOE_ROWS, d), row_blk), w_spec, w_spec],
            out_specs=pl.BlockSpec((MOE_ROWS, tf), lambda f, i, be, nv: (jnp.minimum(i, nv[0] - 1), f)),
            scratch_shapes=[pltpu.VMEM((d, tf), BF16), pltpu.VMEM((d, tf), BF16)],
        ),
        out_shape=jax.ShapeDtypeStruct((n_blocks * MOE_ROWS, ff), BF16),
        compiler_params=_params("arbitrary", "arbitrary"),
        name="moe_expert_up",
    )(block_expert, nvalid, xb, w_gate, w_up)


def _expert_down_kernel(be_ref, nvalid_ref, a_ref, wd_ref, o_ref, wd_bf):
    i = pl.program_id(0)

    @pl.when(i < nvalid_ref[0])
    def _():
        @pl.when((i == 0) | (be_ref[i] != be_ref[jnp.maximum(i - 1, 0)]))
        def _():
            wd_bf[...] = wd_ref[...].astype(BF16)

        o_ref[...] = jnp.dot(a_ref[...], wd_bf[...], preferred_element_type=F32)


def _expert_down(act, w_down, layer, block_expert, nvalid, *, n_blocks):
    ff, d = w_down.shape[2], w_down.shape[3]
    row_blk = lambda i, be, nv: (jnp.minimum(i, nv[0] - 1), 0)
    return pl.pallas_call(
        _expert_down_kernel,
        grid_spec=pltpu.PrefetchScalarGridSpec(
            num_scalar_prefetch=2,
            grid=(n_blocks,),
            in_specs=[pl.BlockSpec((MOE_ROWS, ff), row_blk),
                      pl.BlockSpec((None, None, ff, d), lambda i, be, nv: (layer, be[i], 0, 0))],
            out_specs=pl.BlockSpec((MOE_ROWS, d), row_blk),
            scratch_shapes=[pltpu.VMEM((ff, d), BF16)],
        ),
        out_shape=jax.ShapeDtypeStruct((n_blocks * MOE_ROWS, d), F32),
        compiler_params=_params("arbitrary"),
        name="moe_expert_down",
    )(block_expert, nvalid, act, w_down)


def _combine_kernel(pos_ref, h_ref, wts_ref, y_hbm, o_ref, y0_ref, y1_ref, sem):
    i = pl.program_id(0)
    tm = h_ref.shape[0]
    base = i * tm * EXPERT_TOPK

    def issue(r, _):
        _row_copy(y_hbm, pos_ref[base + EXPERT_TOPK * r], y0_ref, r, sem).start()
        _row_copy(y_hbm, pos_ref[base + EXPERT_TOPK * r + 1], y1_ref, r, sem).start()
        return 0

    lax.fori_loop(0, tm, issue, 0)

    def drain(r, _):
        _row_copy(y_hbm, 0, y0_ref, r, sem).wait()
        _row_copy(y_hbm, 0, y1_ref, r, sem).wait()
        return 0

    lax.fori_loop(0, tm, drain, 0)
    w = wts_ref[...]
    o_ref[...] = h_ref[...] + w[:, 0:1] * y0_ref[...] + w[:, 1:2] * y1_ref[...]


def _combine(h, wts, y, pos, *, tm=128):
    t, d = h.shape
    return pl.pallas_call(
        _combine_kernel,
        grid_spec=pltpu.PrefetchScalarGridSpec(
            num_scalar_prefetch=1,
            grid=(t // tm,),
            in_specs=[pl.BlockSpec((tm, d), lambda i, pos: (i, 0)),
                      pl.BlockSpec((tm, LANES), lambda i, pos: (i, 0)),
                      pl.BlockSpec(memory_space=pl.ANY)],
            out_specs=pl.BlockSpec((tm, d), lambda i, pos: (i, 0)),
            scratch_shapes=[pltpu.VMEM((tm, d), F32), pltpu.VMEM((tm, d), F32), pltpu.SemaphoreType.DMA(())],
        ),
        out_shape=jax.ShapeDtypeStruct((t, d), F32),
        compiler_params=_params("arbitrary"),
        name="moe_combine",
    )(pos, h, wts, y)


def _hier_moe(h, ffn_norm_g, w_r, b_r, w_gate, w_up, w_down, layer):
    t, d = h.shape
    tk = t * EXPERT_TOPK
    n_blocks = tk // MOE_ROWS + N_EXPERTS
    ids, wts = _router(h, ffn_norm_g, w_r, b_r, layer)
    flat_e = ids[:, :EXPERT_TOPK].reshape(tk)
    onehot = (flat_e[:, None] == jnp.arange(N_EXPERTS, dtype=jnp.int32)[None, :]).astype(jnp.int32)
    csum = jnp.cumsum(onehot, axis=0)
    rank = jnp.sum(onehot * csum, axis=1) - 1
    counts = csum[-1]
    padded = (counts + MOE_ROWS - 1) // MOE_ROWS * MOE_ROWS
    pad_end = jnp.cumsum(padded)
    pad_start = pad_end - padded
    dest = (pad_start[flat_e] + rank).astype(jnp.int32)
    flat_tok = jnp.arange(tk, dtype=jnp.int32) // EXPERT_TOPK
    buf_tok = jnp.zeros((n_blocks * MOE_ROWS,), jnp.int32).at[dest].set(flat_tok)
    nvalid = (pad_end[-1] // MOE_ROWS).astype(jnp.int32).reshape(1)
    blk_start = jnp.minimum(jnp.arange(n_blocks, dtype=jnp.int32), nvalid[0] - 1) * MOE_ROWS
    block_expert = jnp.minimum(jnp.searchsorted(pad_end, blk_start, side='right'), N_EXPERTS - 1).astype(jnp.int32)

    xb = _dispatch(h, ffn_norm_g, layer, buf_tok, nvalid, n_blocks=n_blocks)
    act = _expert_up(xb, w_gate, w_up, layer, block_expert, nvalid, n_blocks=n_blocks)
    y = _expert_down(act, w_down, layer, block_expert, nvalid, n_blocks=n_blocks)
    return _combine(h, wts, y, dest)


def kernel(x, positions, attn_norm_g, w_in, q_norm_g, kv_norm_g, wq_b, wkv_b, sinks, w_out_a, w_out_b,
           w_out_c, w_o, ffn_norm_g, w_group, b_group, w_expert, b_expert, w_gate, w_up, w_down,
           final_norm_g):
    batch, seq, d = x.shape
    depth = w_in.shape[0]
    t = batch * seq
    h = x.reshape(t, d)
    pos = positions.reshape(t).astype(F32)

    w_gates = w_in[:, :, OFF_GATES:].astype(BF16)
    q_head = C_NOPE_DIM + C_ROPE_DIM
    wq_pad = jnp.pad(wq_b.reshape(depth, C_Q_RANK, C_HEADS, q_head),
                     ((0, 0), (0, 0), (0, 0), (0, 2 * LANES - q_head))
                     ).reshape(depth, C_Q_RANK, C_HEADS * 2 * LANES).astype(BF16)
    w_r = jnp.concatenate([w_group, w_expert,
                           jnp.zeros((depth, d, LANES - N_GROUPS - N_EXPERTS), F32)], axis=2)
    b_r = jnp.concatenate([b_group, b_expert,
                           jnp.zeros((depth, LANES - N_GROUPS - N_EXPERTS), F32)], axis=1).reshape(depth, 1, LANES)
    attn_g = attn_norm_g.reshape(depth, 1, d)
    ffn_g = ffn_norm_g.reshape(depth, 1, d)

    c_a, s_a = _rope_tables(pos, A_HEAD_DIM)
    c_b, s_b = _rope_tables(pos, B_HEAD_DIM)
    c_pe, s_pe = _rope_tables(pos, C_ROPE_DIM, keep_lanes=C_ROPE_DIM)

    for l in range(depth):
        hn = _rmsnorm(h, attn_norm_g[l])
        z1 = _matmul(hn, w_in, l, n_out=Z1_WIDTH, name="in_proj")
        gates = _matmul(hn, w_gates, l, n_out=3 * d, name="gate_proj")
        qk_a = _rope(z1, c_a, s_a, col_off=OFF_QA, width=2 * A_WIDTH, half=A_HEAD_DIM // 2, out_dtype=F32)
        v_a = _rope(z1, c_a, s_a, col_off=OFF_VA, width=A_WIDTH, half=None, out_dtype=BF16)
        o_a = _moba_attention(qk_a, v_a, batch=batch, seq=seq)
        q_b = _rope(z1, c_b, s_b, col_off=OFF_QB, width=B_WIDTH, half=B_HEAD_DIM // 2, out_dtype=BF16)
        k_b = _rope(z1, c_b, s_b, col_off=OFF_KB, width=B_KV_WIDTH, half=B_HEAD_DIM // 2, out_dtype=BF16)
        v_b = _rope(z1, c_b, s_b, col_off=OFF_VB, width=B_KV_WIDTH, half=None, out_dtype=BF16)
        o_b = _swa_attention(q_b, k_b, v_b, sinks[l], batch=batch, seq=seq)
        cq_n = _rmsnorm(z1, q_norm_g[l], col_off=OFF_CQ, width=C_Q_RANK, piece=512)
        ckv_n = _rmsnorm(z1, kv_norm_g[l], col_off=OFF_CKV, width=C_KV_RANK, piece=512)
        q_c = _matmul(cq_n, wq_pad, l, n_out=C_HEADS * 2 * LANES, tn=1024, name="mla_q_proj")
        q_c = _rope(q_c, c_pe, s_pe, col_off=0, width=C_HEADS * 2 * LANES, half=C_ROPE_DIM // 2,
                    out_dtype=BF16, rope_groups=(False, True))
        kv_c = _matmul(ckv_n, wkv_b, l, n_out=C_HEADS * (C_NOPE_DIM + C_V_DIM), tn=1024, out_dtype=BF16,
                       name="mla_kv_proj")
        kpe = _rope(z1, c_pe, s_pe, col_off=OFF_KPE, width=LANES, half=C_ROPE_DIM // 2, out_dtype=BF16)
        o_c = _mla_attention(q_c, kv_c, kpe, batch=batch, seq=seq)
        y = _gated_out(o_a, o_b, o_c, gates, w_out_a, w_out_b, w_out_c, l)
        h = _matmul(y, w_o, l, n_out=d, residual=h, name="out_proj")
        h = _hier_moe(h, ffn_g, w_r, b_r, w_gate, w_up, w_down, l)
    out = _rmsnorm(h, final_norm_g, out_dtype=F32)
    return out.reshape(batch, seq, d)
```

```python
import functools

import jax
import jax.numpy as jnp
from jax import lax
from jax.experimental import pallas as pl
from jax.experimental.pallas import tpu as pltpu

F32 = jnp.float32
BF16 = jnp.bfloat16

ROPE_THETA = 10000.0
NORM_EPS = 1e-6
NEG_INF = -1e30
PICKED = -3e38
LOG2E = 1.4426950408889634

A_HEADS, A_HEAD_DIM = 16, 128
MOBA_BLOCK, MOBA_TOPK = 256, 3
B_HEADS, B_KV_HEADS, B_HEAD_DIM, SWA_WINDOW = 32, 4, 64, 128
C_HEADS, C_Q_RANK, C_KV_RANK, C_NOPE_DIM, C_ROPE_DIM, C_V_DIM = 16, 1024, 512, 128, 64, 128
N_GROUPS, EXPERTS_PER_GROUP, EXPERT_TOPK, D_FF_EXPERT = 8, 4, 2, 768
N_EXPERTS = N_GROUPS * EXPERTS_PER_GROUP
assert EXPERTS_PER_GROUP == 4

LANES = 128
VMEM_LIMIT_BYTES = 58 * 1024 * 1024

A_WIDTH = A_HEADS * A_HEAD_DIM
B_WIDTH = B_HEADS * B_HEAD_DIM
B_KV_WIDTH = B_KV_HEADS * B_HEAD_DIM
C_WIDTH = C_HEADS * C_V_DIM
OFF_QA = 0
OFF_KA = OFF_QA + A_WIDTH
OFF_VA = OFF_KA + A_WIDTH
OFF_QB = OFF_VA + A_WIDTH
OFF_KB = OFF_QB + B_WIDTH
OFF_VB = OFF_KB + B_KV_WIDTH
OFF_CQ = OFF_VB + B_KV_WIDTH
OFF_CKV = OFF_CQ + C_Q_RANK
OFF_KPE = OFF_CKV + C_KV_RANK
OFF_GATES = OFF_KPE + C_ROPE_DIM
Z1_WIDTH = 10752

MOE_ROWS = 256
MOE_FF_TILE = 256


def _params(*sem):
    return pltpu.CompilerParams(dimension_semantics=sem, vmem_limit_bytes=VMEM_LIMIT_BYTES)


def _rmsnorm_kernel(*refs, n_pieces, width):
    x_refs, g_ref, o_ref = refs[:n_pieces], refs[n_pieces], refs[n_pieces + 1]
    xs = [r[...].astype(F32) for r in x_refs]
    ss = sum(jnp.sum(x * x, axis=-1, keepdims=True) for x in xs)
    inv = lax.rsqrt(ss * (1.0 / width) + NORM_EPS)
    pw = xs[0].shape[1]
    for p, x in enumerate(xs):
        o_ref[:, p * pw:(p + 1) * pw] = (x * inv * g_ref[:, p * pw:(p + 1) * pw]).astype(o_ref.dtype)


def _rmsnorm(x, g, *, col_off=0, width=None, piece=None, tm=256, out_dtype=BF16):
    t = x.shape[0]
    width = width or x.shape[1]
    piece = piece or width
    n_pieces = width // piece
    off = col_off // piece
    assert col_off % piece == 0 and width % piece == 0
    in_specs = [pl.BlockSpec((tm, piece), functools.partial(lambda i, p: (i, off + p), p=p))
                for p in range(n_pieces)]
    in_specs.append(pl.BlockSpec((1, width), lambda i: (0, 0)))
    return pl.pallas_call(
        functools.partial(_rmsnorm_kernel, n_pieces=n_pieces, width=width),
        grid=(t // tm,),
        in_specs=in_specs,
        out_specs=pl.BlockSpec((tm, width), lambda i: (i, 0)),
        out_shape=jax.ShapeDtypeStruct((t, width), out_dtype),
        compiler_params=_params("parallel"),
        name="rmsnorm",
    )(*([x] * n_pieces), g.reshape(1, width))


def _stage_weight(w_ref, wbf_ref, w_is_nk):
    if not w_is_nk:
        wbf_ref[...] = w_ref[...].astype(BF16)
        return
    _, tn, k = w_ref.shape
    for c in range(k // tn):
        wbf_ref[c * tn:(c + 1) * tn, :] = w_ref[0, :, c * tn:(c + 1) * tn].T.astype(BF16)


def _matmul_kernel(*refs, stage_w, w_is_nk, has_res):
    x_ref, w_ref = refs[0], refs[1]
    r_ref = refs[2] if has_res else None
    o_ref = refs[2 + has_res]
    if stage_w:
        wbf_ref = refs[3 + has_res]

        @pl.when(pl.program_id(1) == 0)
        def _():
            _stage_weight(w_ref, wbf_ref, w_is_nk)

        w = wbf_ref[...]
    else:
        w = w_ref[...]
    acc = jnp.dot(x_ref[...], w, preferred_element_type=F32)
    if has_res:
        acc = acc + r_ref[...]
    o_ref[...] = acc.astype(o_ref.dtype)


def _matmul(x, w, layer, *, n_out, col_off=0, w_is_nk=False, tm=1024, tn=512, out_dtype=F32, residual=None,
            name="matmul"):
    m, k = x.shape
    assert w.shape[2 if w_is_nk else 1] == k and m % tm == 0 and n_out % tn == 0
    stage_w = w_is_nk or w.dtype != BF16
    if w_is_nk:
        assert k % tn == 0 and col_off % 8 == 0
        w_spec = pl.BlockSpec((pl.Element(1), pl.Element(tn), pl.Element(k)),
                              lambda j, i: (layer, pl.multiple_of(col_off + j * tn, 8), 0))
    else:
        assert col_off % tn == 0
        w_spec = pl.BlockSpec((None, k, tn), lambda j, i: (layer, 0, j + col_off // tn))
    in_specs = [pl.BlockSpec((tm, k), lambda j, i: (i, 0)), w_spec]
    args = [x, w]
    if residual is not None:
        in_specs.append(pl.BlockSpec((tm, tn), lambda j, i: (i, j)))
        args.append(residual)
    return pl.pallas_call(
        functools.partial(_matmul_kernel, stage_w=stage_w, w_is_nk=w_is_nk, has_res=residual is not None),
        grid=(n_out // tn, m // tm),
        in_specs=in_specs,
        out_specs=pl.BlockSpec((tm, tn), lambda j, i: (i, j)),
        out_shape=jax.ShapeDtypeStruct((m, n_out), out_dtype),
        scratch_shapes=[pltpu.VMEM((k, tn), BF16)] if stage_w else [],
        compiler_params=_params("arbitrary", "arbitrary"),
        name=name,
    )(*args)


def _rope_tables_kernel(pos_ref, invf_ref, sign_ref, keep_ref, c_ref, s_ref):
    ang = pos_ref[...] * invf_ref[...]
    c_ref[...] = jnp.cos(ang) * keep_ref[...]
    s_ref[...] = jnp.sin(ang) * sign_ref[...]


def _rope_tables(pos, dim, *, keep_lanes=LANES):
    t = pos.shape[0]
    half = dim // 2
    lane = jnp.arange(LANES)
    invf = (ROPE_THETA ** (-(2.0 * (lane % half)).astype(F32) / dim)).reshape(1, LANES)
    keep = (lane < keep_lanes).astype(F32).reshape(1, LANES)
    sign = jnp.where((lane % dim) < half, -1.0, 1.0).astype(F32).reshape(1, LANES) * keep
    tm = 1024
    row = pl.BlockSpec((1, LANES), lambda i: (0, 0))
    out = pl.BlockSpec((tm, LANES), lambda i: (i, 0))
    return pl.pallas_call(
        _rope_tables_kernel,
        grid=(t // tm,),
        in_specs=[pl.BlockSpec((tm, 1), lambda i: (i, 0)), row, row, row],
        out_specs=[out, out],
        out_shape=[jax.ShapeDtypeStruct((t, LANES), F32)] * 2,
        compiler_params=_params("parallel"),
        name="rope_tables",
    )(pos.reshape(t, 1), invf, sign, keep)


def _swap_halves(x, half):
    if half == 64:
        return pltpu.roll(x, 64, axis=1)
    lane = lax.broadcasted_iota(jnp.int32, x.shape, 1)
    return jnp.where((lane & half) == 0, pltpu.roll(x, LANES - half, axis=1), pltpu.roll(x, half, axis=1))


def _rope_kernel(x_ref, c_ref, s_ref, o_ref, *, half, rope_groups):
    n_groups = x_ref.shape[1] // LANES
    for k in range(n_groups):
        x = x_ref[:, k * LANES:(k + 1) * LANES].astype(F32)
        if half is not None and rope_groups[k % len(rope_groups)]:
            x = x * c_ref[...] + _swap_halves(x, half) * s_ref[...]
        o_ref[:, k * LANES:(k + 1) * LANES] = x.astype(o_ref.dtype)


def _rope(z, c, s, *, col_off, width, half, out_dtype, rope_groups=(True,), tm=512, bw=512):
    t = z.shape[0]
    bw = min(bw, width)
    assert width % bw == 0 and col_off % bw == 0 and bw % (LANES * len(rope_groups)) == 0
    off = col_off // bw
    tab = pl.BlockSpec((tm, LANES), lambda i, j: (i, 0))
    return pl.pallas_call(
        functools.partial(_rope_kernel, half=half, rope_groups=rope_groups),
        grid=(t // tm, width // bw),
        in_specs=[pl.BlockSpec((tm, bw), lambda i, j: (i, j + off)), tab, tab],
        out_specs=pl.BlockSpec((tm, bw), lambda i, j: (i, j)),
        out_shape=jax.ShapeDtypeStruct((t, width), out_dtype),
        compiler_params=_params("parallel", "parallel"),
        name="rope",
    )(z, c, s)


def _online_block(q, k, v, s_mask, carry, c_exp):
    m, l, acc = carry
    s = s_mask(lax.dot_general(q, k, (((1,), (1,)), ((), ())), preferred_element_type=F32))
    m_new = jnp.maximum(m, jnp.max(s, axis=1, keepdims=True))
    alpha = jnp.exp2((m - m_new) * c_exp)
    p = jnp.exp2((s - m_new) * c_exp)
    l = alpha * l + jnp.sum(p, axis=1, keepdims=True)
    acc = alpha * acc + jnp.dot(p.astype(BF16), v, preferred_element_type=F32)
    return m_new, l, acc


def _causal_attend(qi, qs, load_kvs, tq, dv, scale, diag_mask, past_mask):
    heads = range(len(qs))
    c_exp = scale * LOG2E
    init = (jnp.full((tq, 1), NEG_INF, F32), jnp.zeros((tq, 1), F32), jnp.zeros((tq, dv), F32))
    kvs = load_kvs(qi)
    carries = tuple(_online_block(qs[h], *kvs[h], functools.partial(diag_mask, h), init, c_exp) for h in heads)

    def past(j, carries):
        j = jnp.asarray(j, jnp.int32)
        kvs = load_kvs(j)
        return tuple(_online_block(qs[h], *kvs[h], functools.partial(past_mask, h, j), carries[h], c_exp)
                     for h in heads)

    carries = lax.fori_loop(0, qi, past, carries)
    return [acc / l for _, l, acc in carries]


def _mla_kernel(q_ref, kv_ref, kpe_ref, o_ref, *, tq, heads, scale):
    n_q = q_ref.shape[0] // tq
    qw, dv = 2 * LANES, C_V_DIM
    row = lax.broadcasted_iota(jnp.int32, (tq, tq), 0)
    col = lax.broadcasted_iota(jnp.int32, (tq, tq), 1)
    causal = col <= row

    def load_kvs(j):
        rows = pl.ds(pl.multiple_of(j * tq, tq), tq)
        kpe = kpe_ref[rows, :]
        return [(jnp.concatenate([kv_ref[rows, h * qw:h * qw + C_NOPE_DIM], kpe], axis=1),
                 kv_ref[rows, h * qw + C_NOPE_DIM:(h + 1) * qw]) for h in range(heads)]

    def q_tile(qi, _):
        qi = jnp.asarray(qi, jnp.int32)
        rows = pl.ds(pl.multiple_of(qi * tq, tq), tq)
        qs = [q_ref[rows, h * qw:(h + 1) * qw] for h in range(heads)]
        outs = _causal_attend(qi, qs, load_kvs, tq, dv, scale,
                              lambda h, s: jnp.where(causal, s, NEG_INF), lambda h, j, s: s)
        for h in range(heads):
            o_ref[rows, h * dv:(h + 1) * dv] = outs[h].astype(o_ref.dtype)
        return 0

    lax.fori_loop(0, n_q, q_tile, 0)


def _mla_attention(q, kv, kpe, *, batch, seq, tq=512, heads=2):
    t = q.shape[0]
    scale = (C_NOPE_DIM + C_ROPE_DIM) ** -0.5
    wide = pl.BlockSpec((seq, heads * 2 * LANES), lambda b, g: (b, g))
    return pl.pallas_call(
        functools.partial(_mla_kernel, tq=tq, heads=heads, scale=scale),
        grid=(batch, C_HEADS // heads),
        in_specs=[wide, wide, pl.BlockSpec((seq, LANES), lambda b, g: (b, 0))],
        out_specs=pl.BlockSpec((seq, heads * C_V_DIM), lambda b, g: (b, g)),
        out_shape=jax.ShapeDtypeStruct((t, C_WIDTH), BF16),
        compiler_params=_params("parallel", "parallel"),
        name="mla_attention",
    )(q, kv, kpe)


def _moba_kernel(q_ref, k_ref, v_ref, o_ref, kmean_ref, *, tq, heads, scale):
    bs, d = MOBA_BLOCK, A_HEAD_DIM
    n_blk = q_ref.shape[0] // bs
    per_tile = tq // bs
    for h in range(heads):
        for n in range(n_blk):
            kmean_ref[h, n:n + 1, :] = jnp.mean(k_ref[n * bs:(n + 1) * bs, h * d:(h + 1) * d], axis=0, keepdims=True)
    blk = lax.broadcasted_iota(jnp.int32, (tq, n_blk), 1).astype(F32)
    row1 = lax.broadcasted_iota(jnp.int32, (tq, 1), 0)
    row_sub = sum(((row1 >= c * bs).astype(F32) for c in range(1, per_tile)), jnp.zeros((tq, 1), F32))
    col_local = lax.broadcasted_iota(jnp.int32, (tq, bs), 1)

    def load_kvs(j):
        rows = pl.ds(pl.multiple_of(j * tq, tq), tq)
        return [(k_ref[rows, h * d:(h + 1) * d].astype(BF16), v_ref[rows, h * d:(h + 1) * d]) for h in range(heads)]

    def q_tile(qi, _):
        qi = jnp.asarray(qi, jnp.int32)
        rows = pl.ds(pl.multiple_of(qi * tq, tq), tq)
        first = (qi * per_tile).astype(F32)
        own = first + row_sub
        qs, sels = [], []
        for h in range(heads):
            qf = q_ref[rows, h * d:(h + 1) * d]
            gate = lax.dot_general(qf, kmean_ref[h], (((1,), (1,)), ((), ())),
                                   precision=lax.Precision.HIGHEST, preferred_element_type=F32)
            gate = jnp.where(blk < own, gate, NEG_INF)
            sel = jnp.zeros((tq, n_blk), F32)
            for _ in range(MOBA_TOPK):
                best = jnp.max(gate, axis=1, keepdims=True)
                idx = jnp.min(jnp.where(gate == best, blk, float(n_blk)), axis=1, keepdims=True)
                pick = blk == idx
                sel = jnp.where(pick & (blk < own), 1.0, sel)
                gate = jnp.where(pick, PICKED, gate)
            qs.append(qf.astype(BF16))
            sels.append(sel)

        def chosen(h, b):
            return jnp.sum(jnp.where(blk == b, sels[h], 0.0), axis=1, keepdims=True) > 0.5

        def by_block(s, mask_block):
            parts = [mask_block(c, s[:, c * bs:(c + 1) * bs]) for c in range(per_tile)]
            return parts[0] if per_tile == 1 else jnp.concatenate(parts, axis=1)

        def diag_mask(h, s):
            def mask_block(c, sc):
                limit = jnp.where(row_sub == c, row1 - c * bs + 1,
                                  jnp.where((row_sub > c) & chosen(h, first + c), bs, 0))
                return jnp.where(col_local < limit, sc, NEG_INF)
            return by_block(s, mask_block)

        def past_mask(h, j, s):
            jb = (j * per_tile).astype(F32)
            return by_block(s, lambda c, sc: jnp.where(chosen(h, jb + c), sc, NEG_INF))

        outs = _causal_attend(qi, qs, load_kvs, tq, d, scale, diag_mask, past_mask)
        for h in range(heads):
            o_ref[rows, h * d:(h + 1) * d] = outs[h].astype(o_ref.dtype)
        return 0

    lax.fori_loop(0, q_ref.shape[0] // tq, q_tile, 0)


def _moba_attention(qk, v, *, batch, seq, tq=512, heads=2):
    t = qk.shape[0]
    assert seq % tq == 0 and tq % MOBA_BLOCK == 0
    n_groups = A_HEADS // heads
    wide = heads * A_HEAD_DIM
    return pl.pallas_call(
        functools.partial(_moba_kernel, tq=tq, heads=heads, scale=A_HEAD_DIM ** -0.5),
        grid=(batch, n_groups),
        in_specs=[pl.BlockSpec((seq, wide), lambda b, g: (b, g)),
                  pl.BlockSpec((seq, wide), lambda b, g: (b, n_groups + g)),
                  pl.BlockSpec((seq, wide), lambda b, g: (b, g))],
        out_specs=pl.BlockSpec((seq, wide), lambda b, g: (b, g)),
        out_shape=jax.ShapeDtypeStruct((t, A_WIDTH), BF16),
        scratch_shapes=[pltpu.VMEM((heads, seq // MOBA_BLOCK, A_HEAD_DIM), F32)],
        compiler_params=_params("parallel", "parallel"),
        name="moba_attention",
    )(qk, qk, v)


def _swa_kernel(sinks_ref, q_ref, k_ref, v_ref, o_ref, *, scale):
    w = SWA_WINDOW
    n_blk = q_ref.shape[0] // w
    pair = pl.program_id(1)
    kv_odd = ((pair * 2) // (B_HEADS // B_KV_HEADS)) % 2
    lane = lax.broadcasted_iota(jnp.int32, (w, LANES), 1)
    low = lane < B_HEAD_DIM
    keep_orig = jnp.where(low, 0, 1) == kv_odd
    row = lax.broadcasted_iota(jnp.int32, (2 * w, 2 * w), 0)
    col = lax.broadcasted_iota(jnp.int32, (2 * w, 2 * w), 1)
    rel = (row & (w - 1)) + w - col
    band = (rel >= 0) & (rel < w)
    sink = jnp.where(lax.broadcasted_iota(jnp.int32, (2 * w, 1), 0) < w,
                     sinks_ref[2 * pair], sinks_ref[2 * pair + 1])

    def dup(x):
        xf = x.astype(F32)
        return jnp.where(keep_orig, xf, pltpu.roll(xf, B_HEAD_DIM, axis=1)).astype(BF16)

    def q_block(n, _):
        n = jnp.asarray(n, jnp.int32)
        cur = pl.ds(pl.multiple_of(n * w, w), w)
        prev = pl.ds(pl.multiple_of(jnp.maximum(n - 1, 0) * w, w), w)
        q = q_ref[cur, :]
        zero = jnp.zeros_like(q)
        q2 = jnp.concatenate([jnp.where(low, q, zero), jnp.where(low, zero, q)], axis=0)
        k2 = jnp.concatenate([dup(k_ref[prev, :]), dup(k_ref[cur, :])], axis=0)
        v2 = jnp.concatenate([dup(v_ref[prev, :]), dup(v_ref[cur, :])], axis=0)
        s = lax.dot_general(q2, k2, (((1,), (1,)), ((), ())), preferred_element_type=F32) * scale
        first_key_col = jnp.where(n > 0, 0, w)
        s = jnp.where(band & (col >= first_key_col), s, NEG_INF)
        m = jnp.maximum(jnp.max(s, axis=1, keepdims=True), sink)
        p = jnp.exp(s - m)
        denom = jnp.sum(p, axis=1, keepdims=True) + jnp.exp(sink - m)
        o2 = jnp.dot((p / denom).astype(BF16), v2, preferred_element_type=F32)
        o_ref[cur, :] = jnp.where(low, o2[:w], o2[w:]).astype(o_ref.dtype)
        return 0

    lax.fori_loop(0, n_blk, q_block, 0, unroll=4)


def _swa_attention(q, k, v, sinks, *, batch, seq):
    t = q.shape[0]
    pairs = B_HEADS // 2
    rep = B_HEADS // B_KV_HEADS
    kv_block = lambda b, p, sinks: (b, (2 * p) // rep // 2)
    return pl.pallas_call(
        functools.partial(_swa_kernel, scale=B_HEAD_DIM ** -0.5),
        grid_spec=pltpu.PrefetchScalarGridSpec(
            num_scalar_prefetch=1,
            grid=(batch, pairs),
            in_specs=[pl.BlockSpec((seq, LANES), lambda b, p, sinks: (b, p)),
                      pl.BlockSpec((seq, LANES), kv_block),
                      pl.BlockSpec((seq, LANES), kv_block)],
            out_specs=pl.BlockSpec((seq, LANES), lambda b, p, sinks: (b, p)),
        ),
        out_shape=jax.ShapeDtypeStruct((t, B_WIDTH), BF16),
        compiler_params=_params("parallel", "parallel"),
        name="swa_attention",
    )(sinks, q, k, v)


def _gated_out_kernel(oa_ref, ob_ref, oc_ref, ga_ref, gb_ref, gc_ref, wa_ref, wb_ref, wc_ref, y_ref,
                      wa_bf, wb_bf, wc_bf):
    @pl.when(pl.program_id(1) == 0)
    def _():
        wa_bf[...] = wa_ref[...].astype(BF16)
        wb_bf[...] = wb_ref[...].astype(BF16)
        wc_bf[...] = wc_ref[...].astype(BF16)

    y = jax.nn.sigmoid(ga_ref[...]) * jnp.dot(oa_ref[...], wa_bf[...], preferred_element_type=F32)
    y += jax.nn.sigmoid(gb_ref[...]) * jnp.dot(ob_ref[...], wb_bf[...], preferred_element_type=F32)
    y += jax.nn.sigmoid(gc_ref[...]) * jnp.dot(oc_ref[...], wc_bf[...], preferred_element_type=F32)
    y_ref[...] = y.astype(y_ref.dtype)


def _gated_out(oa, ob, oc, gates, w_out_a, w_out_b, w_out_c, layer, *, tm=512, tn=512):
    t = oa.shape[0]
    d = w_out_a.shape[2]
    nb = d // tn
    o_spec = lambda width: pl.BlockSpec((tm, width), lambda j, i: (i, 0))
    g_spec = lambda which: pl.BlockSpec((tm, tn), lambda j, i: (i, which * nb + j))
    w_spec = lambda width: pl.BlockSpec((None, width, tn), lambda j, i: (layer, 0, j))
    return pl.pallas_call(
        _gated_out_kernel,
        grid=(nb, t // tm),
        in_specs=[o_spec(A_WIDTH), o_spec(B_WIDTH), o_spec(C_WIDTH), g_spec(0), g_spec(1), g_spec(2),
                  w_spec(A_WIDTH), w_spec(B_WIDTH), w_spec(C_WIDTH)],
        out_specs=pl.BlockSpec((tm, tn), lambda j, i: (i, j)),
        out_shape=jax.ShapeDtypeStruct((t, d), BF16),
        scratch_shapes=[pltpu.VMEM((A_WIDTH, tn), BF16), pltpu.VMEM((B_WIDTH, tn), BF16),
                        pltpu.VMEM((C_WIDTH, tn), BF16)],
        compiler_params=_params("arbitrary", "arbitrary"),
        name="gated_out",
    )(oa, ob, oc, gates, gates, gates, w_out_a, w_out_b, w_out_c)


def _router_kernel(h_ref, g_ref, w_ref, b_ref, ids_ref, wts_ref):
    x = h_ref[...]
    hn = x * lax.rsqrt(jnp.mean(x * x, axis=-1, keepdims=True) + NORM_EPS) * g_ref[...]
    logits = jnp.dot(hn, w_ref[...], precision=lax.Precision.HIGHEST, preferred_element_type=F32) + b_ref[...]
    lane = lax.broadcasted_iota(jnp.int32, logits.shape, 1)
    far = float(LANES)
    is_g = lane < N_GROUPS
    g_id = lane.astype(F32)
    gmax = jnp.max(jnp.where(is_g, logits, NEG_INF), axis=1, keepdims=True)
    gsel = jnp.min(jnp.where(is_g & (logits == gmax), g_id, far), axis=1, keepdims=True)
    p_g = 1.0 / jnp.sum(jnp.where(is_g, jnp.exp(logits - gmax), 0.0), axis=1, keepdims=True)
    e_lane = lane - N_GROUPS
    e_id = e_lane.astype(F32)
    e_group = jnp.right_shift(e_lane, 2).astype(F32)
    in_grp = (e_lane >= 0) & (e_lane < N_EXPERTS) & (e_group == gsel)
    emax = jnp.max(jnp.where(in_grp, logits, NEG_INF), axis=1, keepdims=True)
    ee = jnp.where(in_grp, jnp.exp(jnp.where(in_grp, logits, emax) - emax), 0.0)
    ep = ee / jnp.sum(ee, axis=1, keepdims=True)
    p1 = jnp.max(jnp.where(in_grp, ep, -1.0), axis=1, keepdims=True)
    i1 = jnp.min(jnp.where(in_grp & (ep == p1), e_id, far), axis=1, keepdims=True)
    rest = in_grp & (e_id != i1)
    p2 = jnp.max(jnp.where(rest, ep, -1.0), axis=1, keepdims=True)
    i2 = jnp.min(jnp.where(rest & (ep == p2), e_id, far), axis=1, keepdims=True)
    tot = p1 + p2
    ids_ref[...] = jnp.where(lane == 0, i1, jnp.where(lane == 1, i2, 0.0)).astype(jnp.int32)
    wts_ref[...] = jnp.where(lane == 0, p_g * p1 / tot, jnp.where(lane == 1, p_g * p2 / tot, 0.0))


def _router(h, g, w_r, b_r, layer, *, tm=256):
    t, d = h.shape
    out = pl.BlockSpec((tm, LANES), lambda i: (i, 0))
    return pl.pallas_call(
        _router_kernel,
        grid=(t // tm,),
        in_specs=[pl.BlockSpec((tm, d), lambda i: (i, 0)),
                  pl.BlockSpec((None, 1, d), lambda i: (layer, 0, 0)),
                  pl.BlockSpec((None, d, LANES), lambda i: (layer, 0, 0)),
                  pl.BlockSpec((None, 1, LANES), lambda i: (layer, 0, 0))],
        out_specs=[out, out],
        out_shape=[jax.ShapeDtypeStruct((t, LANES), jnp.int32), jax.ShapeDtypeStruct((t, LANES), F32)],
        compiler_params=_params("parallel"),
        name="moe_router",
    )(h, g, w_r, b_r)


def _row_copy(src_hbm, tok, dst_ref, r, sem):
    return pltpu.make_async_copy(src_hbm.at[pl.ds(tok, 1), :], dst_ref.at[pl.ds(r, 1), :], sem)


GATHER_UNROLL = 8
NORM_CHUNK = 16


def _dispatch_kernel(tok_ref, nvalid_ref, h_hbm, g_ref, o_ref, rows_ref, sem):
    i = pl.program_id(0)
    n_valid = nvalid_ref[0]
    n_rows = rows_ref.shape[1]
    slot = lax.rem(i, 2)

    def fetch(blk, slot):
        base = blk * n_rows

        def issue(r, _):
            _row_copy(h_hbm, tok_ref[base + r], rows_ref.at[slot], r, sem.at[slot]).start()
            return 0

        lax.fori_loop(0, n_rows, issue, 0, unroll=GATHER_UNROLL)

    @pl.when(i == 0)
    def _():
        fetch(0, 0)

    @pl.when(i + 1 < n_valid)
    def _():
        fetch(i + 1, 1 - slot)

    @pl.when(i < n_valid)
    def _():
        def drain(r, _):
            _row_copy(h_hbm, 0, rows_ref.at[slot], r, sem.at[slot]).wait()
            return 0

        lax.fori_loop(0, n_rows, drain, 0, unroll=GATHER_UNROLL)

        def norm(c, _):
            rows = pl.ds(pl.multiple_of(jnp.asarray(c, jnp.int32) * NORM_CHUNK, NORM_CHUNK), NORM_CHUNK)
            x = rows_ref[slot, rows, :]
            hn = x * lax.rsqrt(jnp.mean(x * x, axis=-1, keepdims=True) + NORM_EPS) * g_ref[...]
            o_ref[rows, :] = hn.astype(o_ref.dtype)
            return 0

        lax.fori_loop(0, n_rows // NORM_CHUNK, norm, 0)


def _dispatch(h, g, layer, buf_tok, nvalid, *, n_blocks):
    t, d = h.shape
    blk = lambda i, tok, nv: (jnp.minimum(i, nv[0] - 1), 0)
    return pl.pallas_call(
        _dispatch_kernel,
        grid_spec=pltpu.PrefetchScalarGridSpec(
            num_scalar_prefetch=2,
            grid=(n_blocks,),
            in_specs=[pl.BlockSpec(memory_space=pl.ANY),
                      pl.BlockSpec((None, 1, d), lambda i, tok, nv: (layer, 0, 0))],
            out_specs=pl.BlockSpec((MOE_ROWS, d), blk),
            scratch_shapes=[pltpu.VMEM((2, MOE_ROWS, d), F32), pltpu.SemaphoreType.DMA((2,))],
        ),
        out_shape=jax.ShapeDtypeStruct((n_blocks * MOE_ROWS, d), BF16),
        compiler_params=_params("arbitrary"),
        name="moe_dispatch",
    )(buf_tok, nvalid, h, g)


def _expert_up_kernel(be_ref, nvalid_ref, x_ref, wg_ref, wu_ref, o_ref, wg_bf, wu_bf):
    i = pl.program_id(1)

    @pl.when(i < nvalid_ref[0])
    def _():
        @pl.when((i == 0) | (be_ref[i] != be_ref[jnp.maximum(i - 1, 0)]))
        def _():
            wg_bf[...] = wg_ref[...].astype(BF16)
            wu_bf[...] = wu_ref[...].astype(BF16)

        x = x_ref[...]
        hg = jnp.dot(x, wg_bf[...], preferred_element_type=F32)
        hu = jnp.dot(x, wu_bf[...], preferred_element_type=F32)
        o_ref[...] = (jax.nn.silu(hg) * hu).astype(o_ref.dtype)


def _expert_up(xb, w_gate, w_up, layer, block_expert, nvalid, *, n_blocks):
    d = xb.shape[1]
    ff = w_gate.shape[3]
    tf = MOE_FF_TILE
    row_blk = lambda f, i, be, nv: (jnp.minimum(i, nv[0] - 1), 0)
    w_spec = pl.BlockSpec((None, None, d, tf), lambda f, i, be, nv: (layer, be[i], 0, f))
    return pl.pallas_call(
        _expert_up_kernel,
        grid_spec=pltpu.PrefetchScalarGridSpec(
            num_scalar_prefetch=2,
            grid=(ff // tf, n_blocks),
            in_specs=[pl.BlockSpec((MOE_ROWS, d), row_blk), w_spec, w_spec],
            out_specs=pl.BlockSpec((MOE_ROWS, tf), lambda f, i, be, nv: (jnp.minimum(i, nv[0] - 1), f)),
            scratch_shapes=[pltpu.VMEM((d, tf), BF16), pltpu.VMEM((d, tf), BF16)],
        ),
        out_shape=jax.ShapeDtypeStruct((n_blocks * MOE_ROWS, ff), BF16),
        compiler_params=_params("arbitrary", "arbitrary"),
        name="moe_expert_up",
    )(block_expert, nvalid, xb, w_gate, w_up)


def _expert_down_kernel(be_ref, nvalid_ref, a_ref, wd_ref, o_ref, wd_bf):
    i = pl.program_id(0)

    @pl.when(i < nvalid_ref[0])
    def _():
        @pl.when((i == 0) | (be_ref[i] != be_ref[jnp.maximum(i - 1, 0)]))
        def _():
            wd_bf[...] = wd_ref[...].astype(BF16)

        o_ref[...] = jnp.dot(a_ref[...], wd_bf[...], preferred_element_type=F32)


def _expert_down(act, w_down, layer, block_expert, nvalid, *, n_blocks):
    ff, d = w_down.shape[2], w_down.shape[3]
    row_blk = lambda i, be, nv: (jnp.minimum(i, nv[0] - 1), 0)
    return pl.pallas_call(
        _expert_down_kernel,
        grid_spec=pltpu.PrefetchScalarGridSpec(
            num_scalar_prefetch=2,
            grid=(n_blocks,),
            in_specs=[pl.BlockSpec((MOE_ROWS, ff), row_blk),
                      pl.BlockSpec((None, None, ff, d), lambda i, be, nv: (layer, be[i], 0, 0))],
            out_specs=pl.BlockSpec((MOE_ROWS, d), row_blk),
            scratch_shapes=[pltpu.VMEM((ff, d), BF16)],
        ),
        out_shape=jax.ShapeDtypeStruct((n_blocks * MOE_ROWS, d), F32),
        compiler_params=_params("arbitrary"),
        name="moe_expert_down",
    )(block_expert, nvalid, act, w_down)


def _combine_kernel(pos_ref, h_ref, wts_ref, y_hbm, o_ref, y0_ref, y1_ref, sem):
    i = pl.program_id(0)
    tm = h_ref.shape[0]
    slot = lax.rem(i, 2)

    def fetch(blk, slot):
        base = blk * tm * EXPERT_TOPK

        def issue(r, _):
            _row_copy(y_hbm, pos_ref[base + EXPERT_TOPK * r], y0_ref.at[slot], r, sem.at[slot]).start()
            _row_copy(y_hbm, pos_ref[base + EXPERT_TOPK * r + 1], y1_ref.at[slot], r, sem.at[slot]).start()
            return 0

        lax.fori_loop(0, tm, issue, 0, unroll=GATHER_UNROLL)

    @pl.when(i == 0)
    def _():
        fetch(0, 0)

    @pl.when(i + 1 < pl.num_programs(0))
    def _():
        fetch(i + 1, 1 - slot)

    def drain(r, _):
        _row_copy(y_hbm, 0, y0_ref.at[slot], r, sem.at[slot]).wait()
        _row_copy(y_hbm, 0, y1_ref.at[slot], r, sem.at[slot]).wait()
        return 0

    lax.fori_loop(0, tm, drain, 0, unroll=GATHER_UNROLL)
    w = wts_ref[...]
    o_ref[...] = h_ref[...] + w[:, 0:1] * y0_ref[slot] + w[:, 1:2] * y1_ref[slot]


def _combine(h, wts, y, pos, *, tm=128):
    t, d = h.shape
    return pl.pallas_call(
        _combine_kernel,
        grid_spec=pltpu.PrefetchScalarGridSpec(
            num_scalar_prefetch=1,
            grid=(t // tm,),
            in_specs=[pl.BlockSpec((tm, d), lambda i, pos: (i, 0)),
                      pl.BlockSpec((tm, LANES), lambda i, pos: (i, 0)),
                      pl.BlockSpec(memory_space=pl.ANY)],
            out_specs=pl.BlockSpec((tm, d), lambda i, pos: (i, 0)),
            scratch_shapes=[pltpu.VMEM((2, tm, d), F32), pltpu.VMEM((2, tm, d), F32),
                            pltpu.SemaphoreType.DMA((2,))],
        ),
        out_shape=jax.ShapeDtypeStruct((t, d), F32),
        compiler_params=_params("arbitrary"),
        name="moe_combine",
    )(pos, h, wts, y)


def _hier_moe(h, ffn_norm_g, w_r, b_r, w_gate, w_up, w_down, layer):
    t, d = h.shape
    tk = t * EXPERT_TOPK
    n_blocks = tk // MOE_ROWS + N_EXPERTS
    ids, wts = _router(h, ffn_norm_g, w_r, b_r, layer)
    flat_e = ids[:, :EXPERT_TOPK].reshape(tk)
    onehot = (flat_e[:, None] == jnp.arange(N_EXPERTS, dtype=jnp.int32)[None, :]).astype(jnp.int32)
    csum = jnp.cumsum(onehot, axis=0)
    rank = jnp.sum(onehot * csum, axis=1) - 1
    counts = csum[-1]
    padded = (counts + MOE_ROWS - 1) // MOE_ROWS * MOE_ROWS
    pad_end = jnp.cumsum(padded)
    pad_start = pad_end - padded
    dest = (pad_start[flat_e] + rank).astype(jnp.int32)
    flat_tok = jnp.arange(tk, dtype=jnp.int32) // EXPERT_TOPK
    buf_tok = jnp.zeros((n_blocks * MOE_ROWS,), jnp.int32).at[dest].set(flat_tok)
    nvalid = (pad_end[-1] // MOE_ROWS).astype(jnp.int32).reshape(1)
    blk_start = jnp.minimum(jnp.arange(n_blocks, dtype=jnp.int32), nvalid[0] - 1) * MOE_ROWS
    block_expert = jnp.minimum(jnp.searchsorted(pad_end, blk_start, side='right'), N_EXPERTS - 1).astype(jnp.int32)

    xb = _dispatch(h, ffn_norm_g, layer, buf_tok, nvalid, n_blocks=n_blocks)
    act = _expert_up(xb, w_gate, w_up, layer, block_expert, nvalid, n_blocks=n_blocks)
    y = _expert_down(act, w_down, layer, block_expert, nvalid, n_blocks=n_blocks)
    return _combine(h, wts, y, dest)


def kernel(x, positions, attn_norm_g, w_in, q_norm_g, kv_norm_g, wq_b, wkv_b, sinks, w_out_a, w_out_b,
           w_out_c, w_o, ffn_norm_g, w_group, b_group, w_expert, b_expert, w_gate, w_up, w_down,
           final_norm_g):
    batch, seq, d = x.shape
    depth = w_in.shape[0]
    t = batch * seq
    h = x.reshape(t, d)
    pos = positions.reshape(t).astype(F32)

    w_in_t = jnp.swapaxes(w_in, 1, 2)
    q_head = C_NOPE_DIM + C_ROPE_DIM
    wq_pad = jnp.pad(wq_b.reshape(depth, C_Q_RANK, C_HEADS, q_head),
                     ((0, 0), (0, 0), (0, 0), (0, 2 * LANES - q_head))
                     ).reshape(depth, C_Q_RANK, C_HEADS * 2 * LANES).astype(BF16)
    w_r = jnp.concatenate([w_group, w_expert,
                           jnp.zeros((depth, d, LANES - N_GROUPS - N_EXPERTS), F32)], axis=2)
    b_r = jnp.concatenate([b_group, b_expert,
                           jnp.zeros((depth, LANES - N_GROUPS - N_EXPERTS), F32)], axis=1).reshape(depth, 1, LANES)
    attn_g = attn_norm_g.reshape(depth, 1, d)
    ffn_g = ffn_norm_g.reshape(depth, 1, d)

    c_a, s_a = _rope_tables(pos, A_HEAD_DIM)
    c_b, s_b = _rope_tables(pos, B_HEAD_DIM)
    c_pe, s_pe = _rope_tables(pos, C_ROPE_DIM, keep_lanes=C_ROPE_DIM)

    for l in range(depth):
        hn = _rmsnorm(h, attn_norm_g[l])
        z1 = _matmul(hn, w_in_t, l, n_out=Z1_WIDTH, w_is_nk=True, name="in_proj")
        gates = _matmul(hn, w_in_t, l, n_out=3 * d, col_off=OFF_GATES, w_is_nk=True, name="gate_proj")
        qk_a = _rope(z1, c_a, s_a, col_off=OFF_QA, width=2 * A_WIDTH, half=A_HEAD_DIM // 2, out_dtype=F32)
        v_a = _rope(z1, c_a, s_a, col_off=OFF_VA, width=A_WIDTH, half=None, out_dtype=BF16)
        o_a = _moba_attention(qk_a, v_a, batch=batch, seq=seq)
        q_b = _rope(z1, c_b, s_b, col_off=OFF_QB, width=B_WIDTH, half=B_HEAD_DIM // 2, out_dtype=BF16)
        k_b = _rope(z1, c_b, s_b, col_off=OFF_KB, width=B_KV_WIDTH, half=B_HEAD_DIM // 2, out_dtype=BF16)
        v_b = _rope(z1, c_b, s_b, col_off=OFF_VB, width=B_KV_WIDTH, half=None, out_dtype=BF16)
        o_b = _swa_attention(q_b, k_b, v_b, sinks[l], batch=batch, seq=seq)
        cq_n = _rmsnorm(z1, q_norm_g[l], col_off=OFF_CQ, width=C_Q_RANK, piece=512)
        ckv_n = _rmsnorm(z1, kv_norm_g[l], col_off=OFF_CKV, width=C_KV_RANK, piece=512)
        q_c = _matmul(cq_n, wq_pad, l, n_out=C_HEADS * 2 * LANES, tn=1024, name="mla_q_proj")
        q_c = _rope(q_c, c_pe, s_pe, col_off=0, width=C_HEADS * 2 * LANES, half=C_ROPE_DIM // 2,
                    out_dtype=BF16, rope_groups=(False, True))
        kv_c = _matmul(ckv_n, wkv_b, l, n_out=C_HEADS * (C_NOPE_DIM + C_V_DIM), tn=1024, out_dtype=BF16,
                       name="mla_kv_proj")
        kpe = _rope(z1, c_pe, s_pe, col_off=OFF_KPE, width=LANES, half=C_ROPE_DIM // 2, out_dtype=BF16)
        o_c = _mla_attention(q_c, kv_c, kpe, batch=batch, seq=seq)
        y = _gated_out(o_a, o_b, o_c, gates, w_out_a, w_out_b, w_out_c, l)
        h = _matmul(y, w_o, l, n_out=d, residual=h, name="out_proj")
        h = _hier_moe(h, ffn_g, w_r, b_r, w_gate, w_up, w_down, l)
    out = _rmsnorm(h, final_norm_g, out_dtype=F32)
    return out.reshape(batch, seq, d)
```

```python
import functools

import jax
import jax.numpy as jnp
from jax import lax
from jax.experimental import pallas as pl
from jax.experimental.pallas import tpu as pltpu

F32 = jnp.float32
BF16 = jnp.bfloat16

ROPE_THETA = 10000.0
NORM_EPS = 1e-6
NEG_INF = -1e30
PICKED = -3e38
LOG2E = 1.4426950408889634

A_HEADS, A_HEAD_DIM = 16, 128
MOBA_BLOCK, MOBA_TOPK = 256, 3
B_HEADS, B_KV_HEADS, B_HEAD_DIM, SWA_WINDOW = 32, 4, 64, 128
C_HEADS, C_Q_RANK, C_KV_RANK, C_NOPE_DIM, C_ROPE_DIM, C_V_DIM = 16, 1024, 512, 128, 64, 128
N_GROUPS, EXPERTS_PER_GROUP, EXPERT_TOPK, D_FF_EXPERT = 8, 4, 2, 768
N_EXPERTS = N_GROUPS * EXPERTS_PER_GROUP
assert EXPERTS_PER_GROUP == 4

LANES = 128
VMEM_LIMIT_BYTES = 58 * 1024 * 1024

A_WIDTH = A_HEADS * A_HEAD_DIM
B_WIDTH = B_HEADS * B_HEAD_DIM
B_KV_WIDTH = B_KV_HEADS * B_HEAD_DIM
C_WIDTH = C_HEADS * C_V_DIM
OFF_QA = 0
OFF_KA = OFF_QA + A_WIDTH
OFF_VA = OFF_KA + A_WIDTH
OFF_QB = OFF_VA + A_WIDTH
OFF_KB = OFF_QB + B_WIDTH
OFF_VB = OFF_KB + B_KV_WIDTH
OFF_CQ = OFF_VB + B_KV_WIDTH
OFF_CKV = OFF_CQ + C_Q_RANK
OFF_KPE = OFF_CKV + C_KV_RANK
OFF_GATES = OFF_KPE + C_ROPE_DIM
Z1_WIDTH = 10752

MOE_ROWS = 256
MOE_FF_TILE = 256
SWA_UNROLL = 8


def _params(*sem):
    return pltpu.CompilerParams(dimension_semantics=sem, vmem_limit_bytes=VMEM_LIMIT_BYTES)


def _rmsnorm_kernel(*refs, n_pieces, width):
    x_refs, g_ref, o_ref = refs[:n_pieces], refs[n_pieces], refs[n_pieces + 1]
    xs = [r[...].astype(F32) for r in x_refs]
    ss = sum(jnp.sum(x * x, axis=-1, keepdims=True) for x in xs)
    inv = lax.rsqrt(ss * (1.0 / width) + NORM_EPS)
    pw = xs[0].shape[1]
    for p, x in enumerate(xs):
        o_ref[:, p * pw:(p + 1) * pw] = (x * inv * g_ref[:, p * pw:(p + 1) * pw]).astype(o_ref.dtype)


def _rmsnorm(x, g, *, col_off=0, width=None, piece=None, tm=256, out_dtype=BF16):
    t = x.shape[0]
    width = width or x.shape[1]
    piece = piece or width
    n_pieces = width // piece
    off = col_off // piece
    assert col_off % piece == 0 and width % piece == 0
    in_specs = [pl.BlockSpec((tm, piece), functools.partial(lambda i, p: (i, off + p), p=p))
                for p in range(n_pieces)]
    in_specs.append(pl.BlockSpec((1, width), lambda i: (0, 0)))
    return pl.pallas_call(
        functools.partial(_rmsnorm_kernel, n_pieces=n_pieces, width=width),
        grid=(t // tm,),
        in_specs=in_specs,
        out_specs=pl.BlockSpec((tm, width), lambda i: (i, 0)),
        out_shape=jax.ShapeDtypeStruct((t, width), out_dtype),
        compiler_params=_params("parallel"),
        name="rmsnorm",
    )(*([x] * n_pieces), g.reshape(1, width))


def _stage_weight(w_ref, wbf_ref, w_is_nk):
    if not w_is_nk:
        wbf_ref[...] = w_ref[...].astype(BF16)
        return
    _, tn, k = w_ref.shape
    for c in range(k // tn):
        wbf_ref[c * tn:(c + 1) * tn, :] = w_ref[0, :, c * tn:(c + 1) * tn].T.astype(BF16)


def _swap_halves(x, half):
    if half == 64:
        return pltpu.roll(x, 64, axis=1)
    lane = lax.broadcasted_iota(jnp.int32, x.shape, 1)
    return jnp.where((lane & half) == 0, pltpu.roll(x, LANES - half, axis=1), pltpu.roll(x, half, axis=1))


def _matmul_kernel(*refs, stage_w, w_is_nk, has_res, rope_modes, rope_halves):
    x_ref, w_ref = refs[0], refs[1]
    n_in = 2 + has_res + 2 * len(rope_halves)
    r_ref = refs[2] if has_res else None
    table_refs = refs[2 + has_res:n_in]
    o_ref = refs[n_in]
    if stage_w:
        wbf_ref = refs[n_in + 1]

        @pl.when(pl.program_id(1) == 0)
        def _():
            _stage_weight(w_ref, wbf_ref, w_is_nk)

        w = wbf_ref[...]
    else:
        w = w_ref[...]
    acc = jnp.dot(x_ref[...], w, preferred_element_type=F32)
    if has_res:
        acc = acc + r_ref[...]
    if rope_modes is None:
        o_ref[...] = acc.astype(o_ref.dtype)
        return
    j = pl.program_id(0)
    for pattern in sorted(set(rope_modes), key=str):
        blocks = [b for b, p in enumerate(rope_modes) if p == pattern]

        @pl.when(functools.reduce(jnp.logical_or, [j == b for b in blocks]))
        def _(pattern=pattern):
            for g, table in enumerate(pattern):
                xg = acc[:, g * LANES:(g + 1) * LANES]
                if table is not None:
                    c_ref, s_ref = table_refs[2 * table], table_refs[2 * table + 1]
                    xg = xg * c_ref[...] + _swap_halves(xg, rope_halves[table]) * s_ref[...]
                o_ref[:, g * LANES:(g + 1) * LANES] = xg.astype(o_ref.dtype)


def _matmul(x, w, layer, *, n_out, col_off=0, w_is_nk=False, tm=1024, tn=512, out_dtype=F32, residual=None,
            rope_tables=(), rope_group_table=None, name="matmul"):
    m, k = x.shape
    assert w.shape[2 if w_is_nk else 1] == k and m % tm == 0 and n_out % tn == 0
    stage_w = w_is_nk or w.dtype != BF16
    rope_modes = None
    if rope_group_table is not None:
        rope_modes = tuple(tuple(rope_group_table(b * tn + g * LANES) for g in range(tn // LANES))
                           for b in range(n_out // tn))
    if w_is_nk:
        assert k % tn == 0 and col_off % 8 == 0
        w_spec = pl.BlockSpec((pl.Element(1), pl.Element(tn), pl.Element(k)),
                              lambda j, i: (layer, pl.multiple_of(col_off + j * tn, 8), 0))
    else:
        assert col_off % tn == 0
        w_spec = pl.BlockSpec((None, k, tn), lambda j, i: (layer, 0, j + col_off // tn))
    in_specs = [pl.BlockSpec((tm, k), lambda j, i: (i, 0)), w_spec]
    args = [x, w]
    if residual is not None:
        in_specs.append(pl.BlockSpec((tm, tn), lambda j, i: (i, j)))
        args.append(residual)
    for c, s, _ in rope_tables:
        in_specs += [pl.BlockSpec((tm, LANES), lambda j, i: (i, 0))] * 2
        args += [c, s]
    return pl.pallas_call(
        functools.partial(_matmul_kernel, stage_w=stage_w, w_is_nk=w_is_nk, has_res=residual is not None,
                          rope_modes=rope_modes, rope_halves=tuple(half for _, _, half in rope_tables)),
        grid=(n_out // tn, m // tm),
        in_specs=in_specs,
        out_specs=pl.BlockSpec((tm, tn), lambda j, i: (i, j)),
        out_shape=jax.ShapeDtypeStruct((m, n_out), out_dtype),
        scratch_shapes=[pltpu.VMEM((k, tn), BF16)] if stage_w else [],
        compiler_params=_params("arbitrary", "arbitrary"),
        name=name,
    )(*args)


def _rope_tables_kernel(pos_ref, invf_ref, sign_ref, keep_ref, c_ref, s_ref):
    ang = pos_ref[...] * invf_ref[...]
    c_ref[...] = jnp.cos(ang) * keep_ref[...]
    s_ref[...] = jnp.sin(ang) * sign_ref[...]


def _rope_tables(pos, dim, *, keep_lanes=LANES):
    t = pos.shape[0]
    half = dim // 2
    lane = jnp.arange(LANES)
    invf = (ROPE_THETA ** (-(2.0 * (lane % half)).astype(F32) / dim)).reshape(1, LANES)
    keep = (lane < keep_lanes).astype(F32).reshape(1, LANES)
    sign = jnp.where((lane % dim) < half, -1.0, 1.0).astype(F32).reshape(1, LANES) * keep
    tm = 1024
    row = pl.BlockSpec((1, LANES), lambda i: (0, 0))
    out = pl.BlockSpec((tm, LANES), lambda i: (i, 0))
    return pl.pallas_call(
        _rope_tables_kernel,
        grid=(t // tm,),
        in_specs=[pl.BlockSpec((tm, 1), lambda i: (i, 0)), row, row, row],
        out_specs=[out, out],
        out_shape=[jax.ShapeDtypeStruct((t, LANES), F32)] * 2,
        compiler_params=_params("parallel"),
        name="rope_tables",
    )(pos.reshape(t, 1), invf, sign, keep)


def _online_block(q, k, v, s_mask, carry, c_exp):
    m, l, acc = carry
    s = s_mask(lax.dot_general(q, k, (((1,), (1,)), ((), ())), preferred_element_type=F32))
    m_new = jnp.maximum(m, jnp.max(s, axis=1, keepdims=True))
    alpha = jnp.exp2((m - m_new) * c_exp)
    p = jnp.exp2((s - m_new) * c_exp)
    l = alpha * l + jnp.sum(p, axis=1, keepdims=True)
    acc = alpha * acc + jnp.dot(p.astype(BF16), v, preferred_element_type=F32)
    return m_new, l, acc


def _causal_attend(qi, qs, load_kvs, tq, dv, scale, diag_mask, past_mask):
    heads = range(len(qs))
    c_exp = scale * LOG2E
    init = (jnp.full((tq, 1), NEG_INF, F32), jnp.zeros((tq, 1), F32), jnp.zeros((tq, dv), F32))
    kvs = load_kvs(qi)
    carries = tuple(_online_block(qs[h], *kvs[h], functools.partial(diag_mask, h), init, c_exp) for h in heads)

    def past(j, carries):
        j = jnp.asarray(j, jnp.int32)
        kvs = load_kvs(j)
        return tuple(_online_block(qs[h], *kvs[h], functools.partial(past_mask, h, j), carries[h], c_exp)
                     for h in heads)

    carries = lax.fori_loop(0, qi, past, carries)
    return [acc / l for _, l, acc in carries]


def _mla_kernel(q_ref, kv_ref, kpe_ref, o_ref, *, tq, heads, scale):
    n_q = q_ref.shape[0] // tq
    qw, dv = 2 * LANES, C_V_DIM
    row = lax.broadcasted_iota(jnp.int32, (tq, tq), 0)
    col = lax.broadcasted_iota(jnp.int32, (tq, tq), 1)
    causal = col <= row

    def load_kvs(j):
        rows = pl.ds(pl.multiple_of(j * tq, tq), tq)
        kpe = kpe_ref[rows, :].astype(BF16)
        return [(jnp.concatenate([kv_ref[rows, h * qw:h * qw + C_NOPE_DIM], kpe], axis=1),
                 kv_ref[rows, h * qw + C_NOPE_DIM:(h + 1) * qw]) for h in range(heads)]

    def q_tile(qi, _):
        qi = jnp.asarray(qi, jnp.int32)
        rows = pl.ds(pl.multiple_of(qi * tq, tq), tq)
        qs = [q_ref[rows, h * qw:(h + 1) * qw] for h in range(heads)]
        outs = _causal_attend(qi, qs, load_kvs, tq, dv, scale,
                              lambda h, s: jnp.where(causal, s, NEG_INF), lambda h, j, s: s)
        for h in range(heads):
            o_ref[rows, h * dv:(h + 1) * dv] = outs[h].astype(o_ref.dtype)
        return 0

    lax.fori_loop(0, n_q, q_tile, 0)


def _mla_attention(q, kv, z1, *, batch, seq, tq=512, heads=2):
    t = q.shape[0]
    scale = (C_NOPE_DIM + C_ROPE_DIM) ** -0.5
    wide = pl.BlockSpec((seq, heads * 2 * LANES), lambda b, g: (b, g))
    return pl.pallas_call(
        functools.partial(_mla_kernel, tq=tq, heads=heads, scale=scale),
        grid=(batch, C_HEADS // heads),
        in_specs=[wide, wide, pl.BlockSpec((seq, LANES), lambda b, g: (b, OFF_KPE // LANES))],
        out_specs=pl.BlockSpec((seq, heads * C_V_DIM), lambda b, g: (b, g)),
        out_shape=jax.ShapeDtypeStruct((t, C_WIDTH), BF16),
        compiler_params=_params("parallel", "parallel"),
        name="mla_attention",
    )(q, kv, z1)


def _moba_kernel(q_ref, k_ref, v_ref, o_ref, kmean_ref, *, tq, heads, scale):
    bs, d = MOBA_BLOCK, A_HEAD_DIM
    n_blk = q_ref.shape[0] // bs
    per_tile = tq // bs
    for h in range(heads):
        for n in range(n_blk):
            kmean_ref[h, n:n + 1, :] = jnp.mean(k_ref[n * bs:(n + 1) * bs, h * d:(h + 1) * d], axis=0, keepdims=True)
    blk = lax.broadcasted_iota(jnp.int32, (tq, n_blk), 1).astype(F32)
    row1 = lax.broadcasted_iota(jnp.int32, (tq, 1), 0)
    row_sub = sum(((row1 >= c * bs).astype(F32) for c in range(1, per_tile)), jnp.zeros((tq, 1), F32))
    col_local = lax.broadcasted_iota(jnp.int32, (tq, bs), 1)

    def load_kvs(j):
        rows = pl.ds(pl.multiple_of(j * tq, tq), tq)
        return [(k_ref[rows, h * d:(h + 1) * d].astype(BF16), v_ref[rows, h * d:(h + 1) * d].astype(BF16))
                for h in range(heads)]

    def q_tile(qi, _):
        qi = jnp.asarray(qi, jnp.int32)
        rows = pl.ds(pl.multiple_of(qi * tq, tq), tq)
        first = (qi * per_tile).astype(F32)
        own = first + row_sub
        qs, sels = [], []
        for h in range(heads):
            qf = q_ref[rows, h * d:(h + 1) * d]
            gate = lax.dot_general(qf, kmean_ref[h], (((1,), (1,)), ((), ())),
                                   precision=lax.Precision.HIGHEST, preferred_element_type=F32)
            gate = jnp.where(blk < own, gate, NEG_INF)
            sel = jnp.zeros((tq, n_blk), F32)
            for _ in range(MOBA_TOPK):
                best = jnp.max(gate, axis=1, keepdims=True)
                idx = jnp.min(jnp.where(gate == best, blk, float(n_blk)), axis=1, keepdims=True)
                pick = blk == idx
                sel = jnp.where(pick & (blk < own), 1.0, sel)
                gate = jnp.where(pick, PICKED, gate)
            qs.append(qf.astype(BF16))
            sels.append(sel)

        def chosen(h, b):
            return jnp.sum(jnp.where(blk == b, sels[h], 0.0), axis=1, keepdims=True) > 0.5

        def by_block(s, mask_block):
            parts = [mask_block(c, s[:, c * bs:(c + 1) * bs]) for c in range(per_tile)]
            return parts[0] if per_tile == 1 else jnp.concatenate(parts, axis=1)

        def diag_mask(h, s):
            def mask_block(c, sc):
                limit = jnp.where(row_sub == c, row1 - c * bs + 1,
                                  jnp.where((row_sub > c) & chosen(h, first + c), bs, 0))
                return jnp.where(col_local < limit, sc, NEG_INF)
            return by_block(s, mask_block)

        def past_mask(h, j, s):
            jb = (j * per_tile).astype(F32)
            return by_block(s, lambda c, sc: jnp.where(chosen(h, jb + c), sc, NEG_INF))

        outs = _causal_attend(qi, qs, load_kvs, tq, d, scale, diag_mask, past_mask)
        for h in range(heads):
            o_ref[rows, h * d:(h + 1) * d] = outs[h].astype(o_ref.dtype)
        return 0

    lax.fori_loop(0, q_ref.shape[0] // tq, q_tile, 0)


def _moba_attention(z1, *, batch, seq, tq=512, heads=2):
    t = z1.shape[0]
    assert seq % tq == 0 and tq % MOBA_BLOCK == 0
    n_groups = A_HEADS // heads
    wide = heads * A_HEAD_DIM
    assert OFF_QA == 0 and OFF_KA == A_WIDTH and OFF_VA == 2 * A_WIDTH
    return pl.pallas_call(
        functools.partial(_moba_kernel, tq=tq, heads=heads, scale=A_HEAD_DIM ** -0.5),
        grid=(batch, n_groups),
        in_specs=[pl.BlockSpec((seq, wide), lambda b, g: (b, g)),
                  pl.BlockSpec((seq, wide), lambda b, g: (b, n_groups + g)),
                  pl.BlockSpec((seq, wide), lambda b, g: (b, 2 * n_groups + g))],
        out_specs=pl.BlockSpec((seq, wide), lambda b, g: (b, g)),
        out_shape=jax.ShapeDtypeStruct((t, A_WIDTH), BF16),
        scratch_shapes=[pltpu.VMEM((heads, seq // MOBA_BLOCK, A_HEAD_DIM), F32)],
        compiler_params=_params("parallel", "parallel"),
        name="moba_attention",
    )(z1, z1, z1)


def _swa_kernel(sinks_ref, q_ref, k_ref, v_ref, o_ref, *, scale):
    w = SWA_WINDOW
    n_blk = q_ref.shape[0] // w
    pair = pl.program_id(1)
    kv_odd = ((pair * 2) // (B_HEADS // B_KV_HEADS)) % 2
    lane = lax.broadcasted_iota(jnp.int32, (w, LANES), 1)
    low = lane < B_HEAD_DIM
    keep_orig = jnp.where(low, 0, 1) == kv_odd
    row = lax.broadcasted_iota(jnp.int32, (2 * w, 2 * w), 0)
    col = lax.broadcasted_iota(jnp.int32, (2 * w, 2 * w), 1)
    rel = (row & (w - 1)) + w - col
    band = (rel >= 0) & (rel < w)
    sink = jnp.where(lax.broadcasted_iota(jnp.int32, (2 * w, 1), 0) < w,
                     sinks_ref[2 * pair], sinks_ref[2 * pair + 1])

    def dup(x):
        return jnp.where(keep_orig, x, pltpu.roll(x, B_HEAD_DIM, axis=1)).astype(BF16)

    def q_block(n, first):
        start = n * w
        cur = pl.ds(start if first else pl.multiple_of(start, w), w)
        prev = cur if first else pl.ds(pl.multiple_of(start - w, w), w)
        q = q_ref[cur, :]
        zero = jnp.zeros_like(q)
        q2 = jnp.concatenate([jnp.where(low, q, zero), jnp.where(low, zero, q)], axis=0).astype(BF16)
        k2 = jnp.concatenate([dup(k_ref[prev, :]), dup(k_ref[cur, :])], axis=0)
        v2 = jnp.concatenate([dup(v_ref[prev, :]), dup(v_ref[cur, :])], axis=0)
        s = lax.dot_general(q2, k2, (((1,), (1,)), ((), ())), preferred_element_type=F32) * scale
        s = jnp.where(band & (col >= w) if first else band, s, NEG_INF)
        m = jnp.maximum(jnp.max(s, axis=1, keepdims=True), sink)
        p = jnp.exp(s - m)
        denom = jnp.sum(p, axis=1, keepdims=True) + jnp.exp(sink - m)
        o2 = jnp.dot((p / denom).astype(BF16), v2, preferred_element_type=F32)
        o_ref[cur, :] = jnp.where(low, o2[:w], o2[w:]).astype(o_ref.dtype)

    for u in range(SWA_UNROLL):
        q_block(u, u == 0)

    def group(g, _):
        g = jnp.asarray(g, jnp.int32)
        for u in range(SWA_UNROLL):
            q_block(g * SWA_UNROLL + u, False)
        return 0

    lax.fori_loop(1, n_blk // SWA_UNROLL, group, 0)


def _swa_attention(z1, sinks, *, batch, seq):
    t = z1.shape[0]
    pairs = B_HEADS // 2
    rep = B_HEADS // B_KV_HEADS
    assert (seq // SWA_WINDOW) % SWA_UNROLL == 0
    kv_group = lambda p: (2 * p) // rep // 2
    return pl.pallas_call(
        functools.partial(_swa_kernel, scale=B_HEAD_DIM ** -0.5),
        grid_spec=pltpu.PrefetchScalarGridSpec(
            num_scalar_prefetch=1,
            grid=(batch, pairs),
            in_specs=[pl.BlockSpec((seq, LANES), lambda b, p, sinks: (b, OFF_QB // LANES + p)),
                      pl.BlockSpec((seq, LANES), lambda b, p, sinks: (b, OFF_KB // LANES + kv_group(p))),
                      pl.BlockSpec((seq, LANES), lambda b, p, sinks: (b, OFF_VB // LANES + kv_group(p)))],
            out_specs=pl.BlockSpec((seq, LANES), lambda b, p, sinks: (b, p)),
        ),
        out_shape=jax.ShapeDtypeStruct((t, B_WIDTH), BF16),
        compiler_params=_params("parallel", "parallel"),
        name="swa_attention",
    )(sinks, z1, z1, z1)


def _gated_out_kernel(oa_ref, ob_ref, oc_ref, ga_ref, gb_ref, gc_ref, wa_ref, wb_ref, wc_ref, y_ref,
                      wa_bf, wb_bf, wc_bf):
    @pl.when(pl.program_id(1) == 0)
    def _():
        wa_bf[...] = wa_ref[...].astype(BF16)
        wb_bf[...] = wb_ref[...].astype(BF16)
        wc_bf[...] = wc_ref[...].astype(BF16)

    y = jax.nn.sigmoid(ga_ref[...]) * jnp.dot(oa_ref[...], wa_bf[...], preferred_element_type=F32)
    y += jax.nn.sigmoid(gb_ref[...]) * jnp.dot(ob_ref[...], wb_bf[...], preferred_element_type=F32)
    y += jax.nn.sigmoid(gc_ref[...]) * jnp.dot(oc_ref[...], wc_bf[...], preferred_element_type=F32)
    y_ref[...] = y.astype(y_ref.dtype)


def _gated_out(oa, ob, oc, gates, w_out_a, w_out_b, w_out_c, layer, *, tm=512, tn=512):
    t = oa.shape[0]
    d = w_out_a.shape[2]
    nb = d // tn
    o_spec = lambda width: pl.BlockSpec((tm, width), lambda j, i: (i, 0))
    g_spec = lambda which: pl.BlockSpec((tm, tn), lambda j, i: (i, which * nb + j))
    w_spec = lambda width: pl.BlockSpec((None, width, tn), lambda j, i: (layer, 0, j))
    return pl.pallas_call(
        _gated_out_kernel,
        grid=(nb, t // tm),
        in_specs=[o_spec(A_WIDTH), o_spec(B_WIDTH), o_spec(C_WIDTH), g_spec(0), g_spec(1), g_spec(2),
                  w_spec(A_WIDTH), w_spec(B_WIDTH), w_spec(C_WIDTH)],
        out_specs=pl.BlockSpec((tm, tn), lambda j, i: (i, j)),
        out_shape=jax.ShapeDtypeStruct((t, d), BF16),
        scratch_shapes=[pltpu.VMEM((A_WIDTH, tn), BF16), pltpu.VMEM((B_WIDTH, tn), BF16),
                        pltpu.VMEM((C_WIDTH, tn), BF16)],
        compiler_params=_params("arbitrary", "arbitrary"),
        name="gated_out",
    )(oa, ob, oc, gates, gates, gates, w_out_a, w_out_b, w_out_c)


def _router_kernel(h_ref, g_ref, w_ref, b_ref, ids_ref, wts_ref):
    x = h_ref[...]
    hn = x * lax.rsqrt(jnp.mean(x * x, axis=-1, keepdims=True) + NORM_EPS) * g_ref[...]
    logits = jnp.dot(hn, w_ref[...], precision=lax.Precision.HIGHEST, preferred_element_type=F32) + b_ref[...]
    lane = lax.broadcasted_iota(jnp.int32, logits.shape, 1)
    far = float(LANES)
    is_g = lane < N_GROUPS
    g_id = lane.astype(F32)
    gmax = jnp.max(jnp.where(is_g, logits, NEG_INF), axis=1, keepdims=True)
    gsel = jnp.min(jnp.where(is_g & (logits == gmax), g_id, far), axis=1, keepdims=True)
    p_g = 1.0 / jnp.sum(jnp.where(is_g, jnp.exp(logits - gmax), 0.0), axis=1, keepdims=True)
    e_lane = lane - N_GROUPS
    e_id = e_lane.astype(F32)
    e_group = jnp.right_shift(e_lane, 2).astype(F32)
    in_grp = (e_lane >= 0) & (e_lane < N_EXPERTS) & (e_group == gsel)
    emax = jnp.max(jnp.where(in_grp, logits, NEG_INF), axis=1, keepdims=True)
    ee = jnp.where(in_grp, jnp.exp(jnp.where(in_grp, logits, emax) - emax), 0.0)
    ep = ee / jnp.sum(ee, axis=1, keepdims=True)
    p1 = jnp.max(jnp.where(in_grp, ep, -1.0), axis=1, keepdims=True)
    i1 = jnp.min(jnp.where(in_grp & (ep == p1), e_id, far), axis=1, keepdims=True)
    rest = in_grp & (e_id != i1)
    p2 = jnp.max(jnp.where(rest, ep, -1.0), axis=1, keepdims=True)
    i2 = jnp.min(jnp.where(rest & (ep == p2), e_id, far), axis=1, keepdims=True)
    tot = p1 + p2
    ids_ref[...] = jnp.where(lane == 0, i1, jnp.where(lane == 1, i2, 0.0)).astype(jnp.int32)
    wts_ref[...] = jnp.where(lane == 0, p_g * p1 / tot, jnp.where(lane == 1, p_g * p2 / tot, 0.0))


def _router(h, g, w_r, b_r, layer, *, tm=256):
    t, d = h.shape
    out = pl.BlockSpec((tm, LANES), lambda i: (i, 0))
    return pl.pallas_call(
        _router_kernel,
        grid=(t // tm,),
        in_specs=[pl.BlockSpec((tm, d), lambda i: (i, 0)),
                  pl.BlockSpec((None, 1, d), lambda i: (layer, 0, 0)),
                  pl.BlockSpec((None, d, LANES), lambda i: (layer, 0, 0)),
                  pl.BlockSpec((None, 1, LANES), lambda i: (layer, 0, 0))],
        out_specs=[out, out],
        out_shape=[jax.ShapeDtypeStruct((t, LANES), jnp.int32), jax.ShapeDtypeStruct((t, LANES), F32)],
        compiler_params=_params("parallel"),
        name="moe_router",
    )(h, g, w_r, b_r)


def _row_copy(src_hbm, tok, dst_ref, r, sem):
    return pltpu.make_async_copy(src_hbm.at[pl.ds(tok, 1), :], dst_ref.at[pl.ds(r, 1), :], sem)


GATHER_UNROLL = 8
NORM_CHUNK = 16


def _dispatch_kernel(tok_ref, nvalid_ref, h_hbm, g_ref, o_ref, rows_ref, sem):
    i = pl.program_id(0)
    n_valid = nvalid_ref[0]
    n_rows = rows_ref.shape[1]
    slot = lax.rem(i, 2)

    def fetch(blk, slot):
        base = blk * n_rows

        def issue(g, _):
            for u in range(GATHER_UNROLL):
                r = g * GATHER_UNROLL + u
                _row_copy(h_hbm, tok_ref[base + r], rows_ref.at[slot], r, sem.at[slot]).start(priority=u % 2)
            return 0

        lax.fori_loop(0, n_rows // GATHER_UNROLL, issue, 0)

    @pl.when(i == 0)
    def _():
        fetch(0, 0)

    @pl.when(i + 1 < n_valid)
    def _():
        fetch(i + 1, 1 - slot)

    @pl.when(i < n_valid)
    def _():
        def drain(r, _):
            _row_copy(h_hbm, 0, rows_ref.at[slot], r, sem.at[slot]).wait()
            return 0

        lax.fori_loop(0, n_rows, drain, 0, unroll=GATHER_UNROLL)

        def norm(c, _):
            rows = pl.ds(pl.multiple_of(jnp.asarray(c, jnp.int32) * NORM_CHUNK, NORM_CHUNK), NORM_CHUNK)
            x = rows_ref[slot, rows, :]
            hn = x * lax.rsqrt(jnp.mean(x * x, axis=-1, keepdims=True) + NORM_EPS) * g_ref[...]
            o_ref[rows, :] = hn.astype(o_ref.dtype)
            return 0

        lax.fori_loop(0, n_rows // NORM_CHUNK, norm, 0)


def _dispatch(h, g, layer, buf_tok, nvalid, *, n_blocks):
    t, d = h.shape
    blk = lambda i, tok, nv: (jnp.minimum(i, nv[0] - 1), 0)
    return pl.pallas_call(
        _dispatch_kernel,
        grid_spec=pltpu.PrefetchScalarGridSpec(
            num_scalar_prefetch=2,
            grid=(n_blocks,),
            in_specs=[pl.BlockSpec(memory_space=pl.ANY),
                      pl.BlockSpec((None, 1, d), lambda i, tok, nv: (layer, 0, 0))],
            out_specs=pl.BlockSpec((MOE_ROWS, d), blk),
            scratch_shapes=[pltpu.VMEM((2, MOE_ROWS, d), F32), pltpu.SemaphoreType.DMA((2,))],
        ),
        out_shape=jax.ShapeDtypeStruct((n_blocks * MOE_ROWS, d), BF16),
        compiler_params=_params("arbitrary"),
        name="moe_dispatch",
    )(buf_tok, nvalid, h, g)


def _expert_up_kernel(be_ref, nvalid_ref, x_ref, wg_ref, wu_ref, o_ref, wg_bf, wu_bf):
    i = pl.program_id(1)

    @pl.when(i < nvalid_ref[0])
    def _():
        @pl.when((i == 0) | (be_ref[i] != be_ref[jnp.maximum(i - 1, 0)]))
        def _():
            wg_bf[...] = wg_ref[...].astype(BF16)
            wu_bf[...] = wu_ref[...].astype(BF16)

        x = x_ref[...]
        hg = jnp.dot(x, wg_bf[...], preferred_element_type=F32)
        hu = jnp.dot(x, wu_bf[...], preferred_element_type=F32)
        o_ref[...] = (jax.nn.silu(hg) * hu).astype(o_ref.dtype)


def _expert_up(xb, w_gate, w_up, layer, block_expert, nvalid, *, n_blocks):
    d = xb.shape[1]
    ff = w_gate.shape[3]
    tf = MOE_FF_TILE
    row_blk = lambda f, i, be, nv: (jnp.minimum(i, nv[0] - 1), 0)
    w_spec = pl.BlockSpec((None, None, d, tf), lambda f, i, be, nv: (layer, be[i], 0, f))
    return pl.pallas_call(
        _expert_up_kernel,
        grid_spec=pltpu.PrefetchScalarGridSpec(
            num_scalar_prefetch=2,
            grid=(ff // tf, n_blocks),
            in_specs=[pl.BlockSpec((MOE_ROWS, d), row_blk), w_spec, w_spec],
            out_specs=pl.BlockSpec((MOE_ROWS, tf), lambda f, i, be, nv: (jnp.minimum(i, nv[0] - 1), f)),
            scratch_shapes=[pltpu.VMEM((d, tf), BF16), pltpu.VMEM((d, tf), BF16)],
        ),
        out_shape=jax.ShapeDtypeStruct((n_blocks * MOE_ROWS, ff), BF16),
        compiler_params=_params("arbitrary", "arbitrary"),
        name="moe_expert_up",
    )(block_expert, nvalid, xb, w_gate, w_up)


def _expert_down_kernel(be_ref, nvalid_ref, a_ref, wd_ref, o_ref, wd_bf):
    i = pl.program_id(0)

    @pl.when(i < nvalid_ref[0])
    def _():
        @pl.when((i == 0) | (be_ref[i] != be_ref[jnp.maximum(i - 1, 0)]))
        def _():
            wd_bf[...] = wd_ref[...].astype(BF16)

        o_ref[...] = jnp.dot(a_ref[...], wd_bf[...], preferred_element_type=F32)


def _expert_down(act, w_down, layer, block_expert, nvalid, *, n_blocks):
    ff, d = w_down.shape[2], w_down.shape[3]
    row_blk = lambda i, be, nv: (jnp.minimum(i, nv[0] - 1), 0)
    return pl.pallas_call(
        _expert_down_kernel,
        grid_spec=pltpu.PrefetchScalarGridSpec(
            num_scalar_prefetch=2,
            grid=(n_blocks,),
            in_specs=[pl.BlockSpec((MOE_ROWS, ff), row_blk),
                      pl.BlockSpec((None, None, ff, d), lambda i, be, nv: (layer, be[i], 0, 0))],
            out_specs=pl.BlockSpec((MOE_ROWS, d), row_blk),
            scratch_shapes=[pltpu.VMEM((ff, d), BF16)],
        ),
        out_shape=jax.ShapeDtypeStruct((n_blocks * MOE_ROWS, d), F32),
        compiler_params=_params("arbitrary"),
        name="moe_expert_down",
    )(block_expert, nvalid, act, w_down)


def _combine_kernel(pos_ref, h_ref, wts_ref, y_hbm, o_ref, y0_ref, y1_ref, sem):
    i = pl.program_id(0)
    tm = h_ref.shape[0]
    slot = lax.rem(i, 2)

    def fetch(blk, slot):
        base = blk * tm * EXPERT_TOPK

        def issue(g, _):
            for u in range(GATHER_UNROLL):
                r = g * GATHER_UNROLL + u
                _row_copy(y_hbm, pos_ref[base + EXPERT_TOPK * r], y0_ref.at[slot], r, sem.at[slot]).start(priority=0)
                _row_copy(y_hbm, pos_ref[base + EXPERT_TOPK * r + 1], y1_ref.at[slot], r,
                          sem.at[slot]).start(priority=1)
            return 0

        lax.fori_loop(0, tm // GATHER_UNROLL, issue, 0)

    @pl.when(i == 0)
    def _():
        fetch(0, 0)

    @pl.when(i + 1 < pl.num_programs(0))
    def _():
        fetch(i + 1, 1 - slot)

    def drain(r, _):
        _row_copy(y_hbm, 0, y0_ref.at[slot], r, sem.at[slot]).wait()
        _row_copy(y_hbm, 0, y1_ref.at[slot], r, sem.at[slot]).wait()
        return 0

    lax.fori_loop(0, tm, drain, 0, unroll=GATHER_UNROLL)
    w = wts_ref[...]
    o_ref[...] = h_ref[...] + w[:, 0:1] * y0_ref[slot] + w[:, 1:2] * y1_ref[slot]


def _combine(h, wts, y, pos, *, tm=128):
    t, d = h.shape
    return pl.pallas_call(
        _combine_kernel,
        grid_spec=pltpu.PrefetchScalarGridSpec(
            num_scalar_prefetch=1,
            grid=(t // tm,),
            in_specs=[pl.BlockSpec((tm, d), lambda i, pos: (i, 0)),
                      pl.BlockSpec((tm, LANES), lambda i, pos: (i, 0)),
                      pl.BlockSpec(memory_space=pl.ANY)],
            out_specs=pl.BlockSpec((tm, d), lambda i, pos: (i, 0)),
            scratch_shapes=[pltpu.VMEM((2, tm, d), F32), pltpu.VMEM((2, tm, d), F32),
                            pltpu.SemaphoreType.DMA((2,))],
        ),
        out_shape=jax.ShapeDtypeStruct((t, d), F32),
        compiler_params=_params("arbitrary"),
        name="moe_combine",
    )(pos, h, wts, y)


def _hier_moe(h, ffn_norm_g, w_r, b_r, w_gate, w_up, w_down, layer):
    t, d = h.shape
    tk = t * EXPERT_TOPK
    n_blocks = tk // MOE_ROWS + N_EXPERTS
    ids, wts = _router(h, ffn_norm_g, w_r, b_r, layer)
    flat_e = ids[:, :EXPERT_TOPK].reshape(tk)
    onehot = (flat_e[:, None] == jnp.arange(N_EXPERTS, dtype=jnp.int32)[None, :]).astype(jnp.int32)
    csum = jnp.cumsum(onehot, axis=0)
    rank = jnp.sum(onehot * csum, axis=1) - 1
    counts = csum[-1]
    padded = (counts + MOE_ROWS - 1) // MOE_ROWS * MOE_ROWS
    pad_end = jnp.cumsum(padded)
    pad_start = pad_end - padded
    dest = (pad_start[flat_e] + rank).astype(jnp.int32)
    flat_tok = jnp.arange(tk, dtype=jnp.int32) // EXPERT_TOPK
    buf_tok = jnp.zeros((n_blocks * MOE_ROWS,), jnp.int32).at[dest].set(flat_tok)
    nvalid = (pad_end[-1] // MOE_ROWS).astype(jnp.int32).reshape(1)
    blk_start = jnp.minimum(jnp.arange(n_blocks, dtype=jnp.int32), nvalid[0] - 1) * MOE_ROWS
    block_expert = jnp.minimum(jnp.searchsorted(pad_end, blk_start, side='right'), N_EXPERTS - 1).astype(jnp.int32)

    xb = _dispatch(h, ffn_norm_g, layer, buf_tok, nvalid, n_blocks=n_blocks)
    act = _expert_up(xb, w_gate, w_up, layer, block_expert, nvalid, n_blocks=n_blocks)
    y = _expert_down(act, w_down, layer, block_expert, nvalid, n_blocks=n_blocks)
    return _combine(h, wts, y, dest)


ROPE_A, ROPE_B, ROPE_PE = 0, 1, 2


def _in_proj_rope_table(col):
    if OFF_QA <= col < OFF_VA:
        return ROPE_A
    if OFF_QB <= col < OFF_VB:
        return ROPE_B
    if col == OFF_KPE:
        return ROPE_PE
    return None


def _mla_q_rope_table(col):
    return 0 if (col // LANES) % 2 == 1 else None


def kernel(x, positions, attn_norm_g, w_in, q_norm_g, kv_norm_g, wq_b, wkv_b, sinks, w_out_a, w_out_b,
           w_out_c, w_o, ffn_norm_g, w_group, b_group, w_expert, b_expert, w_gate, w_up, w_down,
           final_norm_g):
    batch, seq, d = x.shape
    depth = w_in.shape[0]
    t = batch * seq
    h = x.reshape(t, d)
    pos = positions.reshape(t).astype(F32)

    w_in_t = jnp.swapaxes(w_in, 1, 2)
    q_head = C_NOPE_DIM + C_ROPE_DIM
    wq_pad = jnp.pad(wq_b.reshape(depth, C_Q_RANK, C_HEADS, q_head),
                     ((0, 0), (0, 0), (0, 0), (0, 2 * LANES - q_head))
                     ).reshape(depth, C_Q_RANK, C_HEADS * 2 * LANES).astype(BF16)
    w_r = jnp.concatenate([w_group, w_expert,
                           jnp.zeros((depth, d, LANES - N_GROUPS - N_EXPERTS), F32)], axis=2)
    b_r = jnp.concatenate([b_group, b_expert,
                           jnp.zeros((depth, LANES - N_GROUPS - N_EXPERTS), F32)], axis=1).reshape(depth, 1, LANES)
    attn_g = attn_norm_g.reshape(depth, 1, d)
    ffn_g = ffn_norm_g.reshape(depth, 1, d)

    c_a, s_a = _rope_tables(pos, A_HEAD_DIM)
    c_b, s_b = _rope_tables(pos, B_HEAD_DIM)
    c_pe, s_pe = _rope_tables(pos, C_ROPE_DIM, keep_lanes=C_ROPE_DIM)
    rope_tables = ((c_a, s_a, A_HEAD_DIM // 2), (c_b, s_b, B_HEAD_DIM // 2), (c_pe, s_pe, C_ROPE_DIM // 2))

    for l in range(depth):
        hn = _rmsnorm(h, attn_norm_g[l])
        z1 = _matmul(hn, w_in_t, l, n_out=Z1_WIDTH, w_is_nk=True, rope_tables=rope_tables,
                     rope_group_table=_in_proj_rope_table, name="in_proj")
        gates = _matmul(hn, w_in_t, l, n_out=3 * d, col_off=OFF_GATES, w_is_nk=True, name="gate_proj")
        o_a = _moba_attention(z1, batch=batch, seq=seq)
        o_b = _swa_attention(z1, sinks[l], batch=batch, seq=seq)
        cq_n = _rmsnorm(z1, q_norm_g[l], col_off=OFF_CQ, width=C_Q_RANK, piece=512)
        ckv_n = _rmsnorm(z1, kv_norm_g[l], col_off=OFF_CKV, width=C_KV_RANK, piece=512)
        q_c = _matmul(cq_n, wq_pad, l, n_out=C_HEADS * 2 * LANES, tn=1024, out_dtype=BF16,
                      rope_tables=rope_tables[ROPE_PE:], rope_group_table=_mla_q_rope_table, name="mla_q_proj")
        kv_c = _matmul(ckv_n, wkv_b, l, n_out=C_HEADS * (C_NOPE_DIM + C_V_DIM), tn=1024, out_dtype=BF16,
                       name="mla_kv_proj")
        o_c = _mla_attention(q_c, kv_c, z1, batch=batch, seq=seq)
        y = _gated_out(o_a, o_b, o_c, gates, w_out_a, w_out_b, w_out_c, l)
        h = _matmul(y, w_o, l, n_out=d, residual=h, name="out_proj")
        h = _hier_moe(h, ffn_g, w_r, b_r, w_gate, w_up, w_down, l)
    out = _rmsnorm(h, final_norm_g, out_dtype=F32)
    return out.reshape(batch, seq, d)
```

```python
import functools

import jax
import jax.numpy as jnp
from jax import lax
from jax.experimental import pallas as pl
from jax.experimental.pallas import tpu as pltpu

F32 = jnp.float32
BF16 = jnp.bfloat16

ROPE_THETA = 10000.0
NORM_EPS = 1e-6
NEG_INF = -1e30
PICKED = -3e38
LOG2E = 1.4426950408889634

A_HEADS, A_HEAD_DIM = 16, 128
MOBA_BLOCK, MOBA_TOPK = 256, 3
B_HEADS, B_KV_HEADS, B_HEAD_DIM, SWA_WINDOW = 32, 4, 64, 128
C_HEADS, C_Q_RANK, C_KV_RANK, C_NOPE_DIM, C_ROPE_DIM, C_V_DIM = 16, 1024, 512, 128, 64, 128
N_GROUPS, EXPERTS_PER_GROUP, EXPERT_TOPK, D_FF_EXPERT = 8, 4, 2, 768
N_EXPERTS = N_GROUPS * EXPERTS_PER_GROUP
assert EXPERTS_PER_GROUP == 4

LANES = 128
VMEM_LIMIT_BYTES = 58 * 1024 * 1024

A_WIDTH = A_HEADS * A_HEAD_DIM
B_WIDTH = B_HEADS * B_HEAD_DIM
B_KV_WIDTH = B_KV_HEADS * B_HEAD_DIM
C_WIDTH = C_HEADS * C_V_DIM
OFF_QA = 0
OFF_KA = OFF_QA + A_WIDTH
OFF_VA = OFF_KA + A_WIDTH
OFF_QB = OFF_VA + A_WIDTH
OFF_KB = OFF_QB + B_WIDTH
OFF_VB = OFF_KB + B_KV_WIDTH
OFF_CQ = OFF_VB + B_KV_WIDTH
OFF_CKV = OFF_CQ + C_Q_RANK
OFF_KPE = OFF_CKV + C_KV_RANK
OFF_GATES = OFF_KPE + C_ROPE_DIM
Z1_WIDTH = 10752

MOE_ROWS = 256
MOE_FF_TILE = 256
SWA_UNROLL = 8


def _params(*sem):
    return pltpu.CompilerParams(dimension_semantics=sem, vmem_limit_bytes=VMEM_LIMIT_BYTES)


def _rmsnorm_kernel(*refs, n_pieces, width):
    x_refs, g_ref, o_ref = refs[:n_pieces], refs[n_pieces], refs[n_pieces + 1]
    xs = [r[...].astype(F32) for r in x_refs]
    ss = sum(jnp.sum(x * x, axis=-1, keepdims=True) for x in xs)
    inv = lax.rsqrt(ss * (1.0 / width) + NORM_EPS)
    pw = xs[0].shape[1]
    for p, x in enumerate(xs):
        o_ref[:, p * pw:(p + 1) * pw] = (x * inv * g_ref[:, p * pw:(p + 1) * pw]).astype(o_ref.dtype)


def _rmsnorm(x, g, *, col_off=0, width=None, piece=None, tm=256, out_dtype=BF16):
    t = x.shape[0]
    width = width or x.shape[1]
    piece = piece or width
    n_pieces = width // piece
    off = col_off // piece
    assert col_off % piece == 0 and width % piece == 0
    in_specs = [pl.BlockSpec((tm, piece), functools.partial(lambda i, p: (i, off + p), p=p))
                for p in range(n_pieces)]
    in_specs.append(pl.BlockSpec((1, width), lambda i: (0, 0)))
    return pl.pallas_call(
        functools.partial(_rmsnorm_kernel, n_pieces=n_pieces, width=width),
        grid=(t // tm,),
        in_specs=in_specs,
        out_specs=pl.BlockSpec((tm, width), lambda i: (i, 0)),
        out_shape=jax.ShapeDtypeStruct((t, width), out_dtype),
        compiler_params=_params("parallel"),
        name="rmsnorm",
    )(*([x] * n_pieces), g.reshape(1, width))


def _stage_weight(w_ref, wbf_ref, w_is_nk):
    if not w_is_nk:
        wbf_ref[...] = w_ref[...].astype(BF16)
        return
    _, tn, k = w_ref.shape
    for c in range(k // tn):
        wbf_ref[c * tn:(c + 1) * tn, :] = w_ref[0, :, c * tn:(c + 1) * tn].T.astype(BF16)


def _swap_halves(x, half):
    if half == 64:
        return pltpu.roll(x, 64, axis=1)
    lane = lax.broadcasted_iota(jnp.int32, x.shape, 1)
    return jnp.where((lane & half) == 0, pltpu.roll(x, LANES - half, axis=1), pltpu.roll(x, half, axis=1))


def _matmul_kernel(*refs, stage_w, w_is_nk, has_res, rope_modes, rope_halves):
    x_ref, w_ref = refs[0], refs[1]
    n_in = 2 + has_res + 2 * len(rope_halves)
    r_ref = refs[2] if has_res else None
    table_refs = refs[2 + has_res:n_in]
    o_ref = refs[n_in]
    if stage_w:
        wbf_ref = refs[n_in + 1]

        @pl.when(pl.program_id(1) == 0)
        def _():
            _stage_weight(w_ref, wbf_ref, w_is_nk)

        w = wbf_ref[...]
    else:
        w = w_ref[...]
    acc = jnp.dot(x_ref[...], w, preferred_element_type=F32)
    if has_res:
        acc = acc + r_ref[...]
    if rope_modes is None:
        o_ref[...] = acc.astype(o_ref.dtype)
        return
    j = pl.program_id(0)
    for pattern in sorted(set(rope_modes), key=str):
        blocks = [b for b, p in enumerate(rope_modes) if p == pattern]

        @pl.when(functools.reduce(jnp.logical_or, [j == b for b in blocks]))
        def _(pattern=pattern):
            for g, table in enumerate(pattern):
                xg = acc[:, g * LANES:(g + 1) * LANES]
                if table is not None:
                    c_ref, s_ref = table_refs[2 * table], table_refs[2 * table + 1]
                    xg = xg * c_ref[...] + _swap_halves(xg, rope_halves[table]) * s_ref[...]
                o_ref[:, g * LANES:(g + 1) * LANES] = xg.astype(o_ref.dtype)


def _matmul(x, w, layer, *, n_out, col_off=0, w_is_nk=False, tm=1024, tn=512, out_dtype=F32, residual=None,
            rope_tables=(), rope_group_table=None, name="matmul"):
    m, k = x.shape
    assert w.shape[2 if w_is_nk else 1] == k and m % tm == 0 and n_out % tn == 0
    stage_w = w_is_nk or w.dtype != BF16
    rope_modes = None
    if rope_group_table is not None:
        rope_modes = tuple(tuple(rope_group_table(b * tn + g * LANES) for g in range(tn // LANES))
                           for b in range(n_out // tn))
    if w_is_nk:
        assert k % tn == 0 and col_off % 8 == 0
        w_spec = pl.BlockSpec((pl.Element(1), pl.Element(tn), pl.Element(k)),
                              lambda j, i: (layer, pl.multiple_of(col_off + j * tn, 8), 0))
    else:
        assert col_off % tn == 0
        w_spec = pl.BlockSpec((None, k, tn), lambda j, i: (layer, 0, j + col_off // tn))
    in_specs = [pl.BlockSpec((tm, k), lambda j, i: (i, 0)), w_spec]
    args = [x, w]
    if residual is not None:
        in_specs.append(pl.BlockSpec((tm, tn), lambda j, i: (i, j)))
        args.append(residual)
    for c, s, _ in rope_tables:
        in_specs += [pl.BlockSpec((tm, LANES), lambda j, i: (i, 0))] * 2
        args += [c, s]
    return pl.pallas_call(
        functools.partial(_matmul_kernel, stage_w=stage_w, w_is_nk=w_is_nk, has_res=residual is not None,
                          rope_modes=rope_modes, rope_halves=tuple(half for _, _, half in rope_tables)),
        grid=(n_out // tn, m // tm),
        in_specs=in_specs,
        out_specs=pl.BlockSpec((tm, tn), lambda j, i: (i, j)),
        out_shape=jax.ShapeDtypeStruct((m, n_out), out_dtype),
        scratch_shapes=[pltpu.VMEM((k, tn), BF16)] if stage_w else [],
        compiler_params=_params("arbitrary", "arbitrary"),
        name=name,
    )(*args)


def _rope_tables_kernel(pos_ref, invf_ref, sign_ref, keep_ref, c_ref, s_ref):
    ang = pos_ref[...] * invf_ref[...]
    c_ref[...] = jnp.cos(ang) * keep_ref[...]
    s_ref[...] = jnp.sin(ang) * sign_ref[...]


def _rope_tables(pos, dim, *, keep_lanes=LANES):
    t = pos.shape[0]
    half = dim // 2
    lane = jnp.arange(LANES)
    invf = (ROPE_THETA ** (-(2.0 * (lane % half)).astype(F32) / dim)).reshape(1, LANES)
    keep = (lane < keep_lanes).astype(F32).reshape(1, LANES)
    sign = jnp.where((lane % dim) < half, -1.0, 1.0).astype(F32).reshape(1, LANES) * keep
    tm = 1024
    row = pl.BlockSpec((1, LANES), lambda i: (0, 0))
    out = pl.BlockSpec((tm, LANES), lambda i: (i, 0))
    return pl.pallas_call(
        _rope_tables_kernel,
        grid=(t // tm,),
        in_specs=[pl.BlockSpec((tm, 1), lambda i: (i, 0)), row, row, row],
        out_specs=[out, out],
        out_shape=[jax.ShapeDtypeStruct((t, LANES), F32)] * 2,
        compiler_params=_params("parallel"),
        name="rope_tables",
    )(pos.reshape(t, 1), invf, sign, keep)


def _online_block(q, k, v, s_mask, carry, c_exp):
    m, l, acc = carry
    s = s_mask(lax.dot_general(q, k, (((1,), (1,)), ((), ())), preferred_element_type=F32))
    m_new = jnp.maximum(m, jnp.max(s, axis=1, keepdims=True))
    alpha = jnp.exp2((m - m_new) * c_exp)
    p = jnp.exp2((s - m_new) * c_exp)
    l = alpha * l + jnp.sum(p, axis=1, keepdims=True)
    acc = alpha * acc + jnp.dot(p.astype(BF16), v, preferred_element_type=F32)
    return m_new, l, acc


def _causal_attend(qi, qs, load_kvs, tq, dv, scale, diag_mask, past_mask):
    heads = range(len(qs))
    c_exp = scale * LOG2E
    init = (jnp.full((tq, 1), NEG_INF, F32), jnp.zeros((tq, 1), F32), jnp.zeros((tq, dv), F32))
    kvs = load_kvs(qi)
    carries = tuple(_online_block(qs[h], *kvs[h], functools.partial(diag_mask, h), init, c_exp) for h in heads)

    def past(j, carries):
        j = jnp.asarray(j, jnp.int32)
        kvs = load_kvs(j)
        return tuple(_online_block(qs[h], *kvs[h], functools.partial(past_mask, h, j), carries[h], c_exp)
                     for h in heads)

    carries = lax.fori_loop(0, qi, past, carries)
    return [acc / l for _, l, acc in carries]


def _mla_kernel(q_ref, kv_ref, kpe_ref, o_ref, *, tq, heads, scale):
    n_q = q_ref.shape[0] // tq
    qw, dv = 2 * LANES, C_V_DIM
    row = lax.broadcasted_iota(jnp.int32, (tq, tq), 0)
    col = lax.broadcasted_iota(jnp.int32, (tq, tq), 1)
    causal = col <= row

    def load_kvs(j):
        rows = pl.ds(pl.multiple_of(j * tq, tq), tq)
        kpe = kpe_ref[rows, :].astype(BF16)
        return [(jnp.concatenate([kv_ref[rows, h * qw:h * qw + C_NOPE_DIM], kpe], axis=1),
                 kv_ref[rows, h * qw + C_NOPE_DIM:(h + 1) * qw]) for h in range(heads)]

    def q_tile(qi, _):
        qi = jnp.asarray(qi, jnp.int32)
        rows = pl.ds(pl.multiple_of(qi * tq, tq), tq)
        qs = [q_ref[rows, h * qw:(h + 1) * qw] for h in range(heads)]
        outs = _causal_attend(qi, qs, load_kvs, tq, dv, scale,
                              lambda h, s: jnp.where(causal, s, NEG_INF), lambda h, j, s: s)
        for h in range(heads):
            o_ref[rows, h * dv:(h + 1) * dv] = outs[h].astype(o_ref.dtype)
        return 0

    lax.fori_loop(0, n_q, q_tile, 0)


def _mla_attention(q, kv, z1, *, batch, seq, tq=1024, heads=2):
    t = q.shape[0]
    scale = (C_NOPE_DIM + C_ROPE_DIM) ** -0.5
    wide = pl.BlockSpec((seq, heads * 2 * LANES), lambda b, g: (b, g))
    return pl.pallas_call(
        functools.partial(_mla_kernel, tq=tq, heads=heads, scale=scale),
        grid=(batch, C_HEADS // heads),
        in_specs=[wide, wide, pl.BlockSpec((seq, LANES), lambda b, g: (b, OFF_KPE // LANES))],
        out_specs=pl.BlockSpec((seq, heads * C_V_DIM), lambda b, g: (b, g)),
        out_shape=jax.ShapeDtypeStruct((t, C_WIDTH), BF16),
        compiler_params=_params("parallel", "parallel"),
        name="mla_attention",
    )(q, kv, z1)


def _moba_kernel(q_ref, k_ref, v_ref, o_ref, kmean_ref, *, tq, heads, scale):
    bs, d = MOBA_BLOCK, A_HEAD_DIM
    n_blk = q_ref.shape[0] // bs
    per_tile = tq // bs
    for h in range(heads):
        for n in range(n_blk):
            kmean_ref[h, n:n + 1, :] = jnp.mean(k_ref[n * bs:(n + 1) * bs, h * d:(h + 1) * d], axis=0, keepdims=True)
    blk = lax.broadcasted_iota(jnp.int32, (tq, n_blk), 1).astype(F32)
    row1 = lax.broadcasted_iota(jnp.int32, (tq, 1), 0)
    row_sub = sum(((row1 >= c * bs).astype(F32) for c in range(1, per_tile)), jnp.zeros((tq, 1), F32))
    col_local = lax.broadcasted_iota(jnp.int32, (tq, bs), 1)

    def load_kvs(j):
        rows = pl.ds(pl.multiple_of(j * tq, tq), tq)
        return [(k_ref[rows, h * d:(h + 1) * d].astype(BF16), v_ref[rows, h * d:(h + 1) * d].astype(BF16))
                for h in range(heads)]

    def q_tile(qi, _):
        qi = jnp.asarray(qi, jnp.int32)
        rows = pl.ds(pl.multiple_of(qi * tq, tq), tq)
        first = (qi * per_tile).astype(F32)
        own = first + row_sub
        qs, sels = [], []
        for h in range(heads):
            qf = q_ref[rows, h * d:(h + 1) * d]
            gate = lax.dot_general(qf, kmean_ref[h], (((1,), (1,)), ((), ())),
                                   precision=lax.Precision.HIGHEST, preferred_element_type=F32)
            gate = jnp.where(blk < own, gate, NEG_INF)
            sel = jnp.zeros((tq, n_blk), F32)
            for _ in range(MOBA_TOPK):
                best = jnp.max(gate, axis=1, keepdims=True)
                idx = jnp.min(jnp.where(gate == best, blk, float(n_blk)), axis=1, keepdims=True)
                pick = blk == idx
                sel = jnp.where(pick & (blk < own), 1.0, sel)
                gate = jnp.where(pick, PICKED, gate)
            qs.append(qf.astype(BF16))
            sels.append(sel)

        def chosen(h, b):
            return jnp.sum(jnp.where(blk == b, sels[h], 0.0), axis=1, keepdims=True) > 0.5

        def by_block(s, mask_block):
            parts = [mask_block(c, s[:, c * bs:(c + 1) * bs]) for c in range(per_tile)]
            return parts[0] if per_tile == 1 else jnp.concatenate(parts, axis=1)

        def diag_mask(h, s):
            def mask_block(c, sc):
                limit = jnp.where(row_sub == c, row1 - c * bs + 1,
                                  jnp.where((row_sub > c) & chosen(h, first + c), bs, 0))
                return jnp.where(col_local < limit, sc, NEG_INF)
            return by_block(s, mask_block)

        def past_mask(h, j, s):
            jb = (j * per_tile).astype(F32)
            return by_block(s, lambda c, sc: jnp.where(chosen(h, jb + c), sc, NEG_INF))

        outs = _causal_attend(qi, qs, load_kvs, tq, d, scale, diag_mask, past_mask)
        for h in range(heads):
            o_ref[rows, h * d:(h + 1) * d] = outs[h].astype(o_ref.dtype)
        return 0

    lax.fori_loop(0, q_ref.shape[0] // tq, q_tile, 0)


def _moba_attention(z1, *, batch, seq, tq=1024, heads=2):
    t = z1.shape[0]
    assert seq % tq == 0 and tq % MOBA_BLOCK == 0
    n_groups = A_HEADS // heads
    wide = heads * A_HEAD_DIM
    assert OFF_QA == 0 and OFF_KA == A_WIDTH and OFF_VA == 2 * A_WIDTH
    return pl.pallas_call(
        functools.partial(_moba_kernel, tq=tq, heads=heads, scale=A_HEAD_DIM ** -0.5),
        grid=(batch, n_groups),
        in_specs=[pl.BlockSpec((seq, wide), lambda b, g: (b, g)),
                  pl.BlockSpec((seq, wide), lambda b, g: (b, n_groups + g)),
                  pl.BlockSpec((seq, wide), lambda b, g: (b, 2 * n_groups + g))],
        out_specs=pl.BlockSpec((seq, wide), lambda b, g: (b, g)),
        out_shape=jax.ShapeDtypeStruct((t, A_WIDTH), BF16),
        scratch_shapes=[pltpu.VMEM((heads, seq // MOBA_BLOCK, A_HEAD_DIM), F32)],
        compiler_params=_params("parallel", "parallel"),
        name="moba_attention",
    )(z1, z1, z1)


def _swa_kernel(sinks_ref, q_ref, k_ref, v_ref, o_ref, *, scale):
    w = SWA_WINDOW
    n_blk = q_ref.shape[0] // w
    pair = pl.program_id(1)
    kv_odd = ((pair * 2) // (B_HEADS // B_KV_HEADS)) % 2
    lane = lax.broadcasted_iota(jnp.int32, (w, LANES), 1)
    low = lane < B_HEAD_DIM
    keep_orig = jnp.where(low, 0, 1) == kv_odd
    row = lax.broadcasted_iota(jnp.int32, (2 * w, 2 * w), 0)
    col = lax.broadcasted_iota(jnp.int32, (2 * w, 2 * w), 1)
    rel = (row & (w - 1)) + w - col
    band = (rel >= 0) & (rel < w)
    sink = jnp.where(lax.broadcasted_iota(jnp.int32, (2 * w, 1), 0) < w,
                     sinks_ref[2 * pair], sinks_ref[2 * pair + 1])

    def dup(x):
        return jnp.where(keep_orig, x, pltpu.roll(x, B_HEAD_DIM, axis=1)).astype(BF16)

    def q_block(n, first):
        start = n * w
        cur = pl.ds(start if first else pl.multiple_of(start, w), w)
        prev = cur if first else pl.ds(pl.multiple_of(start - w, w), w)
        q = q_ref[cur, :]
        zero = jnp.zeros_like(q)
        q2 = jnp.concatenate([jnp.where(low, q, zero), jnp.where(low, zero, q)], axis=0).astype(BF16)
        k2 = jnp.concatenate([dup(k_ref[prev, :]), dup(k_ref[cur, :])], axis=0)
        v2 = jnp.concatenate([dup(v_ref[prev, :]), dup(v_ref[cur, :])], axis=0)
        s = lax.dot_general(q2, k2, (((1,), (1,)), ((), ())), preferred_element_type=F32) * scale
        s = jnp.where(band & (col >= w) if first else band, s, NEG_INF)
        m = jnp.maximum(jnp.max(s, axis=1, keepdims=True), sink)
        p = jnp.exp(s - m)
        denom = jnp.sum(p, axis=1, keepdims=True) + jnp.exp(sink - m)
        o2 = jnp.dot((p / denom).astype(BF16), v2, preferred_element_type=F32)
        o_ref[cur, :] = jnp.where(low, o2[:w], o2[w:]).astype(o_ref.dtype)

    for u in range(SWA_UNROLL):
        q_block(u, u == 0)

    def group(g, _):
        g = jnp.asarray(g, jnp.int32)
        for u in range(SWA_UNROLL):
            q_block(g * SWA_UNROLL + u, False)
        return 0

    lax.fori_loop(1, n_blk // SWA_UNROLL, group, 0)


def _swa_attention(z1, sinks, *, batch, seq):
    t = z1.shape[0]
    pairs = B_HEADS // 2
    rep = B_HEADS // B_KV_HEADS
    assert (seq // SWA_WINDOW) % SWA_UNROLL == 0
    kv_group = lambda p: (2 * p) // rep // 2
    return pl.pallas_call(
        functools.partial(_swa_kernel, scale=B_HEAD_DIM ** -0.5),
        grid_spec=pltpu.PrefetchScalarGridSpec(
            num_scalar_prefetch=1,
            grid=(batch, pairs),
            in_specs=[pl.BlockSpec((seq, LANES), lambda b, p, sinks: (b, OFF_QB // LANES + p)),
                      pl.BlockSpec((seq, LANES), lambda b, p, sinks: (b, OFF_KB // LANES + kv_group(p))),
                      pl.BlockSpec((seq, LANES), lambda b, p, sinks: (b, OFF_VB // LANES + kv_group(p)))],
            out_specs=pl.BlockSpec((seq, LANES), lambda b, p, sinks: (b, p)),
        ),
        out_shape=jax.ShapeDtypeStruct((t, B_WIDTH), BF16),
        compiler_params=_params("parallel", "parallel"),
        name="swa_attention",
    )(sinks, z1, z1, z1)


def _gated_out_kernel(oa_ref, ob_ref, oc_ref, ga_ref, gb_ref, gc_ref, wa_ref, wb_ref, wc_ref, y_ref,
                      wa_bf, wb_bf, wc_bf):
    @pl.when(pl.program_id(1) == 0)
    def _():
        wa_bf[...] = wa_ref[...].astype(BF16)
        wb_bf[...] = wb_ref[...].astype(BF16)
        wc_bf[...] = wc_ref[...].astype(BF16)

    y = jax.nn.sigmoid(ga_ref[...]) * jnp.dot(oa_ref[...], wa_bf[...], preferred_element_type=F32)
    y += jax.nn.sigmoid(gb_ref[...]) * jnp.dot(ob_ref[...], wb_bf[...], preferred_element_type=F32)
    y += jax.nn.sigmoid(gc_ref[...]) * jnp.dot(oc_ref[...], wc_bf[...], preferred_element_type=F32)
    y_ref[...] = y.astype(y_ref.dtype)


def _gated_out(oa, ob, oc, gates, w_out_a, w_out_b, w_out_c, layer, *, tm=512, tn=512):
    t = oa.shape[0]
    d = w_out_a.shape[2]
    nb = d // tn
    o_spec = lambda width: pl.BlockSpec((tm, width), lambda j, i: (i, 0))
    g_spec = lambda which: pl.BlockSpec((tm, tn), lambda j, i: (i, which * nb + j))
    w_spec = lambda width: pl.BlockSpec((None, width, tn), lambda j, i: (layer, 0, j))
    return pl.pallas_call(
        _gated_out_kernel,
        grid=(nb, t // tm),
        in_specs=[o_spec(A_WIDTH), o_spec(B_WIDTH), o_spec(C_WIDTH), g_spec(0), g_spec(1), g_spec(2),
                  w_spec(A_WIDTH), w_spec(B_WIDTH), w_spec(C_WIDTH)],
        out_specs=pl.BlockSpec((tm, tn), lambda j, i: (i, j)),
        out_shape=jax.ShapeDtypeStruct((t, d), BF16),
        scratch_shapes=[pltpu.VMEM((A_WIDTH, tn), BF16), pltpu.VMEM((B_WIDTH, tn), BF16),
                        pltpu.VMEM((C_WIDTH, tn), BF16)],
        compiler_params=_params("arbitrary", "arbitrary"),
        name="gated_out",
    )(oa, ob, oc, gates, gates, gates, w_out_a, w_out_b, w_out_c)


def _router_kernel(h_ref, g_ref, w_ref, b_ref, ids_ref, wts_ref):
    x = h_ref[...]
    hn = x * lax.rsqrt(jnp.mean(x * x, axis=-1, keepdims=True) + NORM_EPS) * g_ref[...]
    logits = jnp.dot(hn, w_ref[...], precision=lax.Precision.HIGHEST, preferred_element_type=F32) + b_ref[...]
    lane = lax.broadcasted_iota(jnp.int32, logits.shape, 1)
    far = float(LANES)
    is_g = lane < N_GROUPS
    g_id = lane.astype(F32)
    gmax = jnp.max(jnp.where(is_g, logits, NEG_INF), axis=1, keepdims=True)
    gsel = jnp.min(jnp.where(is_g & (logits == gmax), g_id, far), axis=1, keepdims=True)
    p_g = 1.0 / jnp.sum(jnp.where(is_g, jnp.exp(logits - gmax), 0.0), axis=1, keepdims=True)
    e_lane = lane - N_GROUPS
    e_id = e_lane.astype(F32)
    e_group = jnp.right_shift(e_lane, 2).astype(F32)
    in_grp = (e_lane >= 0) & (e_lane < N_EXPERTS) & (e_group == gsel)
    emax = jnp.max(jnp.where(in_grp, logits, NEG_INF), axis=1, keepdims=True)
    ee = jnp.where(in_grp, jnp.exp(jnp.where(in_grp, logits, emax) - emax), 0.0)
    ep = ee / jnp.sum(ee, axis=1, keepdims=True)
    p1 = jnp.max(jnp.where(in_grp, ep, -1.0), axis=1, keepdims=True)
    i1 = jnp.min(jnp.where(in_grp & (ep == p1), e_id, far), axis=1, keepdims=True)
    rest = in_grp & (e_id != i1)
    p2 = jnp.max(jnp.where(rest, ep, -1.0), axis=1, keepdims=True)
    i2 = jnp.min(jnp.where(rest & (ep == p2), e_id, far), axis=1, keepdims=True)
    tot = p1 + p2
    ids_ref[...] = jnp.where(lane == 0, i1, jnp.where(lane == 1, i2, 0.0)).astype(jnp.int32)
    wts_ref[...] = jnp.where(lane == 0, p_g * p1 / tot, jnp.where(lane == 1, p_g * p2 / tot, 0.0))


def _router(h, g, w_r, b_r, layer, *, tm=256):
    t, d = h.shape
    out = pl.BlockSpec((tm, LANES), lambda i: (i, 0))
    return pl.pallas_call(
        _router_kernel,
        grid=(t // tm,),
        in_specs=[pl.BlockSpec((tm, d), lambda i: (i, 0)),
                  pl.BlockSpec((None, 1, d), lambda i: (layer, 0, 0)),
                  pl.BlockSpec((None, d, LANES), lambda i: (layer, 0, 0)),
                  pl.BlockSpec((None, 1, LANES), lambda i: (layer, 0, 0))],
        out_specs=[out, out],
        out_shape=[jax.ShapeDtypeStruct((t, LANES), jnp.int32), jax.ShapeDtypeStruct((t, LANES), F32)],
        compiler_params=_params("parallel"),
        name="moe_router",
    )(h, g, w_r, b_r)


def _row_copy(src_hbm, tok, dst_ref, r, sem):
    return pltpu.make_async_copy(src_hbm.at[pl.ds(tok, 1), :], dst_ref.at[pl.ds(r, 1), :], sem)


GATHER_UNROLL = 8
NORM_CHUNK = 16


def _dispatch_kernel(tok_ref, nvalid_ref, h_hbm, g_ref, o_ref, rows_ref, sem):
    i = pl.program_id(0)
    n_valid = nvalid_ref[0]
    n_rows = rows_ref.shape[1]
    slot = lax.rem(i, 2)

    def fetch(blk, slot):
        base = blk * n_rows

        def issue(g, _):
            for u in range(GATHER_UNROLL):
                r = g * GATHER_UNROLL + u
                _row_copy(h_hbm, tok_ref[base + r], rows_ref.at[slot], r, sem.at[slot]).start()
            return 0

        lax.fori_loop(0, n_rows // GATHER_UNROLL, issue, 0)

    @pl.when(i == 0)
    def _():
        fetch(0, 0)

    @pl.when(i + 1 < n_valid)
    def _():
        fetch(i + 1, 1 - slot)

    @pl.when(i < n_valid)
    def _():
        def drain(r, _):
            _row_copy(h_hbm, 0, rows_ref.at[slot], r, sem.at[slot]).wait()
            return 0

        lax.fori_loop(0, n_rows, drain, 0, unroll=GATHER_UNROLL)

        def norm(c, _):
            rows = pl.ds(pl.multiple_of(jnp.asarray(c, jnp.int32) * NORM_CHUNK, NORM_CHUNK), NORM_CHUNK)
            x = rows_ref[slot, rows, :]
            hn = x * lax.rsqrt(jnp.mean(x * x, axis=-1, keepdims=True) + NORM_EPS) * g_ref[...]
            o_ref[rows, :] = hn.astype(o_ref.dtype)
            return 0

        lax.fori_loop(0, n_rows // NORM_CHUNK, norm, 0)


def _dispatch(h, g, layer, buf_tok, nvalid, *, n_blocks):
    t, d = h.shape
    blk = lambda i, tok, nv: (jnp.minimum(i, nv[0] - 1), 0)
    return pl.pallas_call(
        _dispatch_kernel,
        grid_spec=pltpu.PrefetchScalarGridSpec(
            num_scalar_prefetch=2,
            grid=(n_blocks,),
            in_specs=[pl.BlockSpec(memory_space=pl.ANY),
                      pl.BlockSpec((None, 1, d), lambda i, tok, nv: (layer, 0, 0))],
            out_specs=pl.BlockSpec((MOE_ROWS, d), blk),
            scratch_shapes=[pltpu.VMEM((2, MOE_ROWS, d), F32), pltpu.SemaphoreType.DMA((2,))],
        ),
        out_shape=jax.ShapeDtypeStruct((n_blocks * MOE_ROWS, d), BF16),
        compiler_params=_params("arbitrary"),
        name="moe_dispatch",
    )(buf_tok, nvalid, h, g)


def _run_weights(be_ref, first_ref, next_ref, i, pass_id, n_passes, n_blocks, copies, stage):
    @pl.when((pass_id == 0) & (i == 0))
    def _():
        for c in copies(be_ref[0], 0):
            c.start()

    @pl.when(first_ref[i] == 1)
    def _():
        for c in copies(be_ref[i], pass_id):
            c.wait()
        stage()
        nxt = next_ref[i]
        same_pass = nxt < n_blocks
        e_next = jnp.where(same_pass, be_ref[jnp.minimum(nxt, n_blocks - 1)], be_ref[0])
        pass_next = jnp.where(same_pass, pass_id, pass_id + 1)

        @pl.when(same_pass | (pass_id + 1 < n_passes))
        def _():
            for c in copies(e_next, pass_next):
                c.start()


def _expert_up_kernel(be_ref, nvalid_ref, first_ref, next_ref, x_ref, wg_hbm, wu_hbm, o_ref,
                      wg_land, wu_land, wg_bf, wu_bf, sem, *, layer):
    f, i = pl.program_id(0), pl.program_id(1)
    tf = wg_land.shape[1]

    def copies(e, ff_tile):
        cols = pl.ds(pl.multiple_of(ff_tile * tf, tf), tf)
        return (pltpu.make_async_copy(wg_hbm.at[layer, e, :, cols], wg_land, sem.at[0]),
                pltpu.make_async_copy(wu_hbm.at[layer, e, :, cols], wu_land, sem.at[1]))

    def stage():
        wg_bf[...] = wg_land[...].astype(BF16)
        wu_bf[...] = wu_land[...].astype(BF16)

    _run_weights(be_ref, first_ref, next_ref, i, f, pl.num_programs(0), pl.num_programs(1), copies, stage)

    @pl.when(i < nvalid_ref[0])
    def _():
        x = x_ref[...]
        hg = jnp.dot(x, wg_bf[...], preferred_element_type=F32)
        hu = jnp.dot(x, wu_bf[...], preferred_element_type=F32)
        o_ref[...] = (jax.nn.silu(hg) * hu).astype(o_ref.dtype)


def _expert_up(xb, w_gate, w_up, layer, runs, *, n_blocks):
    d = xb.shape[1]
    ff = w_gate.shape[3]
    tf = MOE_FF_TILE
    row_blk = lambda f, i, be, nv, first, nxt: (jnp.minimum(i, nv[0] - 1), 0)
    return pl.pallas_call(
        functools.partial(_expert_up_kernel, layer=layer),
        grid_spec=pltpu.PrefetchScalarGridSpec(
            num_scalar_prefetch=4,
            grid=(ff // tf, n_blocks),
            in_specs=[pl.BlockSpec((MOE_ROWS, d), row_blk), pl.BlockSpec(memory_space=pl.ANY),
                      pl.BlockSpec(memory_space=pl.ANY)],
            out_specs=pl.BlockSpec((MOE_ROWS, tf),
                                   lambda f, i, be, nv, first, nxt: (jnp.minimum(i, nv[0] - 1), f)),
            scratch_shapes=[pltpu.VMEM((d, tf), F32), pltpu.VMEM((d, tf), F32),
                            pltpu.VMEM((d, tf), BF16), pltpu.VMEM((d, tf), BF16),
                            pltpu.SemaphoreType.DMA((2,))],
        ),
        out_shape=jax.ShapeDtypeStruct((n_blocks * MOE_ROWS, ff), BF16),
        compiler_params=_params("arbitrary", "arbitrary"),
        name="moe_expert_up",
    )(*runs, xb, w_gate, w_up)


def _expert_down_kernel(be_ref, nvalid_ref, first_ref, next_ref, a_ref, wd_hbm, o_ref, wd_land, wd_bf, sem,
                        *, layer):
    i = pl.program_id(0)

    def copies(e, _):
        return (pltpu.make_async_copy(wd_hbm.at[layer, e], wd_land, sem.at[0]),)

    def stage():
        wd_bf[...] = wd_land[...].astype(BF16)

    _run_weights(be_ref, first_ref, next_ref, i, 0, 1, pl.num_programs(0), copies, stage)

    @pl.when(i < nvalid_ref[0])
    def _():
        o_ref[...] = jnp.dot(a_ref[...], wd_bf[...], preferred_element_type=F32)


def _expert_down(act, w_down, layer, runs, *, n_blocks):
    ff, d = w_down.shape[2], w_down.shape[3]
    row_blk = lambda i, be, nv, first, nxt: (jnp.minimum(i, nv[0] - 1), 0)
    return pl.pallas_call(
        functools.partial(_expert_down_kernel, layer=layer),
        grid_spec=pltpu.PrefetchScalarGridSpec(
            num_scalar_prefetch=4,
            grid=(n_blocks,),
            in_specs=[pl.BlockSpec((MOE_ROWS, ff), row_blk), pl.BlockSpec(memory_space=pl.ANY)],
            out_specs=pl.BlockSpec((MOE_ROWS, d), row_blk),
            scratch_shapes=[pltpu.VMEM((ff, d), F32), pltpu.VMEM((ff, d), BF16), pltpu.SemaphoreType.DMA((1,))],
        ),
        out_shape=jax.ShapeDtypeStruct((n_blocks * MOE_ROWS, d), F32),
        compiler_params=_params("arbitrary"),
        name="moe_expert_down",
    )(*runs, act, w_down)


def _combine_kernel(pos_ref, h_ref, wts_ref, y_hbm, o_ref, y0_ref, y1_ref, sem):
    i = pl.program_id(0)
    tm = h_ref.shape[0]
    slot = lax.rem(i, 2)

    def fetch(blk, slot):
        base = blk * tm * EXPERT_TOPK

        def issue(g, _):
            for u in range(GATHER_UNROLL):
                r = g * GATHER_UNROLL + u
                _row_copy(y_hbm, pos_ref[base + EXPERT_TOPK * r], y0_ref.at[slot], r, sem.at[slot]).start(priority=0)
                _row_copy(y_hbm, pos_ref[base + EXPERT_TOPK * r + 1], y1_ref.at[slot], r,
                          sem.at[slot]).start(priority=1)
            return 0

        lax.fori_loop(0, tm // GATHER_UNROLL, issue, 0)

    @pl.when(i == 0)
    def _():
        fetch(0, 0)

    @pl.when(i + 1 < pl.num_programs(0))
    def _():
        fetch(i + 1, 1 - slot)

    def drain(r, _):
        _row_copy(y_hbm, 0, y0_ref.at[slot], r, sem.at[slot]).wait()
        _row_copy(y_hbm, 0, y1_ref.at[slot], r, sem.at[slot]).wait()
        return 0

    lax.fori_loop(0, tm, drain, 0, unroll=GATHER_UNROLL)
    w = wts_ref[...]
    o_ref[...] = h_ref[...] + w[:, 0:1] * y0_ref[slot] + w[:, 1:2] * y1_ref[slot]


def _combine(h, wts, y, pos, *, tm=128):
    t, d = h.shape
    return pl.pallas_call(
        _combine_kernel,
        grid_spec=pltpu.PrefetchScalarGridSpec(
            num_scalar_prefetch=1,
            grid=(t // tm,),
            in_specs=[pl.BlockSpec((tm, d), lambda i, pos: (i, 0)),
                      pl.BlockSpec((tm, LANES), lambda i, pos: (i, 0)),
                      pl.BlockSpec(memory_space=pl.ANY)],
            out_specs=pl.BlockSpec((tm, d), lambda i, pos: (i, 0)),
            scratch_shapes=[pltpu.VMEM((2, tm, d), F32), pltpu.VMEM((2, tm, d), F32),
                            pltpu.SemaphoreType.DMA((2,))],
        ),
        out_shape=jax.ShapeDtypeStruct((t, d), F32),
        compiler_params=_params("arbitrary"),
        name="moe_combine",
    )(pos, h, wts, y)


def _hier_moe(h, ffn_norm_g, w_r, b_r, w_gate, w_up, w_down, layer):
    t, d = h.shape
    tk = t * EXPERT_TOPK
    n_blocks = tk // MOE_ROWS + N_EXPERTS
    ids, wts = _router(h, ffn_norm_g, w_r, b_r, layer)
    flat_e = ids[:, :EXPERT_TOPK].reshape(tk)
    onehot = (flat_e[:, None] == jnp.arange(N_EXPERTS, dtype=jnp.int32)[None, :]).astype(jnp.int32)
    csum = jnp.cumsum(onehot, axis=0)
    rank = jnp.sum(onehot * csum, axis=1) - 1
    counts = csum[-1]
    padded = (counts + MOE_ROWS - 1) // MOE_ROWS * MOE_ROWS
    pad_end = jnp.cumsum(padded)
    pad_start = pad_end - padded
    dest = (pad_start[flat_e] + rank).astype(jnp.int32)
    flat_tok = jnp.arange(tk, dtype=jnp.int32) // EXPERT_TOPK
    buf_tok = jnp.zeros((n_blocks * MOE_ROWS,), jnp.int32).at[dest].set(flat_tok)
    nvalid = (pad_end[-1] // MOE_ROWS).astype(jnp.int32).reshape(1)
    blk_start = jnp.minimum(jnp.arange(n_blocks, dtype=jnp.int32), nvalid[0] - 1) * MOE_ROWS
    block_expert = jnp.minimum(jnp.searchsorted(pad_end, blk_start, side='right'), N_EXPERTS - 1).astype(jnp.int32)

    blk = jnp.arange(n_blocks, dtype=jnp.int32)
    prev_expert = jnp.concatenate([jnp.full((1,), -1, jnp.int32), block_expert[:-1]])
    run_first = (blk < nvalid[0]) & (block_expert != prev_expert)
    first_at_or_after = lax.cummin(jnp.where(run_first, blk, n_blocks), reverse=True)
    run_next = jnp.concatenate([first_at_or_after[1:], jnp.full((1,), n_blocks, jnp.int32)])
    runs = (block_expert, nvalid, run_first.astype(jnp.int32), run_next)

    xb = _dispatch(h, ffn_norm_g, layer, buf_tok, nvalid, n_blocks=n_blocks)
    act = _expert_up(xb, w_gate, w_up, layer, runs, n_blocks=n_blocks)
    y = _expert_down(act, w_down, layer, runs, n_blocks=n_blocks)
    return _combine(h, wts, y, dest)


ROPE_A, ROPE_B, ROPE_PE = 0, 1, 2


def _in_proj_rope_table(col):
    if OFF_QA <= col < OFF_VA:
        return ROPE_A
    if OFF_QB <= col < OFF_VB:
        return ROPE_B
    if col == OFF_KPE:
        return ROPE_PE
    return None


def _mla_q_rope_table(col):
    return 0 if (col // LANES) % 2 == 1 else None


def kernel(x, positions, attn_norm_g, w_in, q_norm_g, kv_norm_g, wq_b, wkv_b, sinks, w_out_a, w_out_b,
           w_out_c, w_o, ffn_norm_g, w_group, b_group, w_expert, b_expert, w_gate, w_up, w_down,
           final_norm_g):
    batch, seq, d = x.shape
    depth = w_in.shape[0]
    t = batch * seq
    h = x.reshape(t, d)
    pos = positions.reshape(t).astype(F32)

    w_in_t = jnp.swapaxes(w_in, 1, 2)
    q_head = C_NOPE_DIM + C_ROPE_DIM
    wq_pad = jnp.pad(wq_b.reshape(depth, C_Q_RANK, C_HEADS, q_head),
                     ((0, 0), (0, 0), (0, 0), (0, 2 * LANES - q_head))
                     ).reshape(depth, C_Q_RANK, C_HEADS * 2 * LANES).astype(BF16)
    w_r = jnp.concatenate([w_group, w_expert,
                           jnp.zeros((depth, d, LANES - N_GROUPS - N_EXPERTS), F32)], axis=2)
    b_r = jnp.concatenate([b_group, b_expert,
                           jnp.zeros((depth, LANES - N_GROUPS - N_EXPERTS), F32)], axis=1).reshape(depth, 1, LANES)
    attn_g = attn_norm_g.reshape(depth, 1, d)
    ffn_g = ffn_norm_g.reshape(depth, 1, d)

    c_a, s_a = _rope_tables(pos, A_HEAD_DIM)
    c_b, s_b = _rope_tables(pos, B_HEAD_DIM)
    c_pe, s_pe = _rope_tables(pos, C_ROPE_DIM, keep_lanes=C_ROPE_DIM)
    rope_tables = ((c_a, s_a, A_HEAD_DIM // 2), (c_b, s_b, B_HEAD_DIM // 2), (c_pe, s_pe, C_ROPE_DIM // 2))

    for l in range(depth):
        hn = _rmsnorm(h, attn_norm_g[l])
        z1 = _matmul(hn, w_in_t, l, n_out=Z1_WIDTH, w_is_nk=True, rope_tables=rope_tables,
                     rope_group_table=_in_proj_rope_table, name="in_proj")
        gates = _matmul(hn, w_in_t, l, n_out=3 * d, col_off=OFF_GATES, w_is_nk=True, name="gate_proj")
        o_a = _moba_attention(z1, batch=batch, seq=seq)
        o_b = _swa_attention(z1, sinks[l], batch=batch, seq=seq)
        cq_n = _rmsnorm(z1, q_norm_g[l], col_off=OFF_CQ, width=C_Q_RANK, piece=512)
        ckv_n = _rmsnorm(z1, kv_norm_g[l], col_off=OFF_CKV, width=C_KV_RANK, piece=512)
        q_c = _matmul(cq_n, wq_pad, l, n_out=C_HEADS * 2 * LANES, tn=1024, out_dtype=BF16,
                      rope_tables=rope_tables[ROPE_PE:], rope_group_table=_mla_q_rope_table, name="mla_q_proj")
        kv_c = _matmul(ckv_n, wkv_b, l, n_out=C_HEADS * (C_NOPE_DIM + C_V_DIM), tn=1024, out_dtype=BF16,
                       name="mla_kv_proj")
        o_c = _mla_attention(q_c, kv_c, z1, batch=batch, seq=seq)
        y = _gated_out(o_a, o_b, o_c, gates, w_out_a, w_out_b, w_out_c, l)
        h = _matmul(y, w_o, l, n_out=d, residual=h, name="out_proj")
        h = _hier_moe(h, ffn_g, w_r, b_r, w_gate, w_up, w_down, l)
    out = _rmsnorm(h, final_norm_g, out_dtype=F32)
    return out.reshape(batch, seq, d)
```

```python
import functools

import jax
import jax.numpy as jnp
from jax import lax
from jax.experimental import pallas as pl
from jax.experimental.pallas import tpu as pltpu

F32 = jnp.float32
BF16 = jnp.bfloat16

ROPE_THETA = 10000.0
NORM_EPS = 1e-6
NEG_INF = -1e30
PICKED = -3e38
LOG2E = 1.4426950408889634

A_HEADS, A_HEAD_DIM = 16, 128
MOBA_BLOCK, MOBA_TOPK = 256, 3
B_HEADS, B_KV_HEADS, B_HEAD_DIM, SWA_WINDOW = 32, 4, 64, 128
C_HEADS, C_Q_RANK, C_KV_RANK, C_NOPE_DIM, C_ROPE_DIM, C_V_DIM = 16, 1024, 512, 128, 64, 128
N_GROUPS, EXPERTS_PER_GROUP, EXPERT_TOPK, D_FF_EXPERT = 8, 4, 2, 768
N_EXPERTS = N_GROUPS * EXPERTS_PER_GROUP
assert EXPERTS_PER_GROUP == 4

LANES = 128
VMEM_LIMIT_BYTES = 58 * 1024 * 1024

A_WIDTH = A_HEADS * A_HEAD_DIM
B_WIDTH = B_HEADS * B_HEAD_DIM
B_KV_WIDTH = B_KV_HEADS * B_HEAD_DIM
C_WIDTH = C_HEADS * C_V_DIM
OFF_QA = 0
OFF_KA = OFF_QA + A_WIDTH
OFF_VA = OFF_KA + A_WIDTH
OFF_QB = OFF_VA + A_WIDTH
OFF_KB = OFF_QB + B_WIDTH
OFF_VB = OFF_KB + B_KV_WIDTH
OFF_CQ = OFF_VB + B_KV_WIDTH
OFF_CKV = OFF_CQ + C_Q_RANK
OFF_KPE = OFF_CKV + C_KV_RANK
OFF_GATES = OFF_KPE + C_ROPE_DIM
Z1_WIDTH = 10752

MOE_ROWS = 256
MOE_FF_TILE = 256
SWA_UNROLL = 8


def _params(*sem):
    return pltpu.CompilerParams(dimension_semantics=sem, vmem_limit_bytes=VMEM_LIMIT_BYTES)


def _rmsnorm_kernel(*refs, n_pieces, width):
    x_refs, g_ref, o_ref = refs[:n_pieces], refs[n_pieces], refs[n_pieces + 1]
    xs = [r[...].astype(F32) for r in x_refs]
    ss = sum(jnp.sum(x * x, axis=-1, keepdims=True) for x in xs)
    inv = lax.rsqrt(ss * (1.0 / width) + NORM_EPS)
    pw = xs[0].shape[1]
    for p, x in enumerate(xs):
        o_ref[:, p * pw:(p + 1) * pw] = (x * inv * g_ref[:, p * pw:(p + 1) * pw]).astype(o_ref.dtype)


def _rmsnorm(x, g, *, col_off=0, width=None, piece=None, tm=256, out_dtype=BF16):
    t = x.shape[0]
    width = width or x.shape[1]
    piece = piece or width
    n_pieces = width // piece
    off = col_off // piece
    assert col_off % piece == 0 and width % piece == 0
    in_specs = [pl.BlockSpec((tm, piece), functools.partial(lambda i, p: (i, off + p), p=p))
                for p in range(n_pieces)]
    in_specs.append(pl.BlockSpec((1, width), lambda i: (0, 0)))
    return pl.pallas_call(
        functools.partial(_rmsnorm_kernel, n_pieces=n_pieces, width=width),
        grid=(t // tm,),
        in_specs=in_specs,
        out_specs=pl.BlockSpec((tm, width), lambda i: (i, 0)),
        out_shape=jax.ShapeDtypeStruct((t, width), out_dtype),
        compiler_params=_params("parallel"),
        name="rmsnorm",
    )(*([x] * n_pieces), g.reshape(1, width))


def _stage_weight(w_ref, wbf_ref, w_is_nk):
    if not w_is_nk:
        wbf_ref[...] = w_ref[...].astype(BF16)
        return
    _, tn, k = w_ref.shape
    for c in range(k // tn):
        wbf_ref[c * tn:(c + 1) * tn, :] = w_ref[0, :, c * tn:(c + 1) * tn].T.astype(BF16)


def _swap_halves(x, half):
    if half == 64:
        return pltpu.roll(x, 64, axis=1)
    lane = lax.broadcasted_iota(jnp.int32, x.shape, 1)
    return jnp.where((lane & half) == 0, pltpu.roll(x, LANES - half, axis=1), pltpu.roll(x, half, axis=1))


def _matmul_kernel(*refs, stage_w, w_is_nk, has_res, rope_modes, rope_halves):
    x_ref, w_ref = refs[0], refs[1]
    n_in = 2 + has_res + 2 * len(rope_halves)
    r_ref = refs[2] if has_res else None
    table_refs = refs[2 + has_res:n_in]
    o_ref = refs[n_in]
    if stage_w:
        wbf_ref = refs[n_in + 1]

        @pl.when(pl.program_id(1) == 0)
        def _():
            _stage_weight(w_ref, wbf_ref, w_is_nk)

        w = wbf_ref[...]
    else:
        w = w_ref[...]
    acc = jnp.dot(x_ref[...], w, preferred_element_type=F32)
    if has_res:
        acc = acc + r_ref[...]
    if rope_modes is None:
        o_ref[...] = acc.astype(o_ref.dtype)
        return
    j = pl.program_id(0)
    for pattern in sorted(set(rope_modes), key=str):
        blocks = [b for b, p in enumerate(rope_modes) if p == pattern]

        @pl.when(functools.reduce(jnp.logical_or, [j == b for b in blocks]))
        def _(pattern=pattern):
            for g, table in enumerate(pattern):
                xg = acc[:, g * LANES:(g + 1) * LANES]
                if table is not None:
                    c_ref, s_ref = table_refs[2 * table], table_refs[2 * table + 1]
                    xg = xg * c_ref[...] + _swap_halves(xg, rope_halves[table]) * s_ref[...]
                o_ref[:, g * LANES:(g + 1) * LANES] = xg.astype(o_ref.dtype)


def _matmul(x, w, layer, *, n_out, col_off=0, w_is_nk=False, tm=1024, tn=512, out_dtype=F32, residual=None,
            rope_tables=(), rope_group_table=None, name="matmul"):
    m, k = x.shape
    assert w.shape[2 if w_is_nk else 1] == k and m % tm == 0 and n_out % tn == 0
    stage_w = w_is_nk or w.dtype != BF16
    rope_modes = None
    if rope_group_table is not None:
        rope_modes = tuple(tuple(rope_group_table(b * tn + g * LANES) for g in range(tn // LANES))
                           for b in range(n_out // tn))
    if w_is_nk:
        assert k % tn == 0 and col_off % 8 == 0
        w_spec = pl.BlockSpec((pl.Element(1), pl.Element(tn), pl.Element(k)),
                              lambda j, i: (layer, pl.multiple_of(col_off + j * tn, 8), 0))
    else:
        assert col_off % tn == 0
        w_spec = pl.BlockSpec((None, k, tn), lambda j, i: (layer, 0, j + col_off // tn))
    in_specs = [pl.BlockSpec((tm, k), lambda j, i: (i, 0)), w_spec]
    args = [x, w]
    if residual is not None:
        in_specs.append(pl.BlockSpec((tm, tn), lambda j, i: (i, j)))
        args.append(residual)
    for c, s, _ in rope_tables:
        in_specs += [pl.BlockSpec((tm, LANES), lambda j, i: (i, 0))] * 2
        args += [c, s]
    return pl.pallas_call(
        functools.partial(_matmul_kernel, stage_w=stage_w, w_is_nk=w_is_nk, has_res=residual is not None,
                          rope_modes=rope_modes, rope_halves=tuple(half for _, _, half in rope_tables)),
        grid=(n_out // tn, m // tm),
        in_specs=in_specs,
        out_specs=pl.BlockSpec((tm, tn), lambda j, i: (i, j)),
        out_shape=jax.ShapeDtypeStruct((m, n_out), out_dtype),
        scratch_shapes=[pltpu.VMEM((k, tn), BF16)] if stage_w else [],
        compiler_params=_params("arbitrary", "arbitrary"),
        name=name,
    )(*args)


def _rope_tables_kernel(pos_ref, invf_ref, sign_ref, keep_ref, c_ref, s_ref):
    ang = pos_ref[...] * invf_ref[...]
    c_ref[...] = jnp.cos(ang) * keep_ref[...]
    s_ref[...] = jnp.sin(ang) * sign_ref[...]


def _rope_tables(pos, dim, *, keep_lanes=LANES):
    t = pos.shape[0]
    half = dim // 2
    lane = jnp.arange(LANES)
    invf = (ROPE_THETA ** (-(2.0 * (lane % half)).astype(F32) / dim)).reshape(1, LANES)
    keep = (lane < keep_lanes).astype(F32).reshape(1, LANES)
    sign = jnp.where((lane % dim) < half, -1.0, 1.0).astype(F32).reshape(1, LANES) * keep
    tm = 1024
    row = pl.BlockSpec((1, LANES), lambda i: (0, 0))
    out = pl.BlockSpec((tm, LANES), lambda i: (i, 0))
    return pl.pallas_call(
        _rope_tables_kernel,
        grid=(t // tm,),
        in_specs=[pl.BlockSpec((tm, 1), lambda i: (i, 0)), row, row, row],
        out_specs=[out, out],
        out_shape=[jax.ShapeDtypeStruct((t, LANES), F32)] * 2,
        compiler_params=_params("parallel"),
        name="rope_tables",
    )(pos.reshape(t, 1), invf, sign, keep)


def _online_block(q, k, v, s_mask, carry, c_exp):
    m, l, acc = carry
    s = s_mask(lax.dot_general(q, k, (((1,), (1,)), ((), ())), preferred_element_type=F32))
    m_new = jnp.maximum(m, jnp.max(s, axis=1, keepdims=True))
    alpha = jnp.exp2((m - m_new) * c_exp)
    p = jnp.exp2((s - m_new) * c_exp)
    l = alpha * l + jnp.sum(p, axis=1, keepdims=True)
    acc = alpha * acc + jnp.dot(p.astype(BF16), v, preferred_element_type=F32)
    return m_new, l, acc


def _causal_attend(qi, qs, load_kvs, tq, dv, scale, diag_mask, past_mask):
    heads = range(len(qs))
    c_exp = scale * LOG2E
    init = (jnp.full((tq, 1), NEG_INF, F32), jnp.zeros((tq, 1), F32), jnp.zeros((tq, dv), F32))
    kvs = load_kvs(qi)
    carries = tuple(_online_block(qs[h], *kvs[h], functools.partial(diag_mask, h), init, c_exp) for h in heads)

    def past(j, carries):
        j = jnp.asarray(j, jnp.int32)
        kvs = load_kvs(j)
        return tuple(_online_block(qs[h], *kvs[h], functools.partial(past_mask, h, j), carries[h], c_exp)
                     for h in heads)

    carries = lax.fori_loop(0, qi, past, carries)
    return [acc / l for _, l, acc in carries]


def _mla_kernel(q_ref, kv_ref, kpe_ref, o_ref, *, tq, heads, scale):
    n_q = q_ref.shape[0] // tq
    qw, dv = 2 * LANES, C_V_DIM
    row = lax.broadcasted_iota(jnp.int32, (tq, tq), 0)
    col = lax.broadcasted_iota(jnp.int32, (tq, tq), 1)
    causal = col <= row

    def load_kvs(j):
        rows = pl.ds(pl.multiple_of(j * tq, tq), tq)
        kpe = kpe_ref[rows, :].astype(BF16)
        return [(jnp.concatenate([kv_ref[rows, h * qw:h * qw + C_NOPE_DIM], kpe], axis=1),
                 kv_ref[rows, h * qw + C_NOPE_DIM:(h + 1) * qw]) for h in range(heads)]

    def q_tile(qi, _):
        qi = jnp.asarray(qi, jnp.int32)
        rows = pl.ds(pl.multiple_of(qi * tq, tq), tq)
        qs = [q_ref[rows, h * qw:(h + 1) * qw] for h in range(heads)]
        outs = _causal_attend(qi, qs, load_kvs, tq, dv, scale,
                              lambda h, s: jnp.where(causal, s, NEG_INF), lambda h, j, s: s)
        for h in range(heads):
            o_ref[rows, h * dv:(h + 1) * dv] = outs[h].astype(o_ref.dtype)
        return 0

    lax.fori_loop(0, n_q, q_tile, 0)


def _mla_attention(q, kv, z1, *, batch, seq, tq=1024, heads=2):
    t = q.shape[0]
    scale = (C_NOPE_DIM + C_ROPE_DIM) ** -0.5
    wide = pl.BlockSpec((seq, heads * 2 * LANES), lambda b, g: (b, g))
    return pl.pallas_call(
        functools.partial(_mla_kernel, tq=tq, heads=heads, scale=scale),
        grid=(batch, C_HEADS // heads),
        in_specs=[wide, wide, pl.BlockSpec((seq, LANES), lambda b, g: (b, OFF_KPE // LANES))],
        out_specs=pl.BlockSpec((seq, heads * C_V_DIM), lambda b, g: (b, g)),
        out_shape=jax.ShapeDtypeStruct((t, C_WIDTH), BF16),
        compiler_params=_params("parallel", "parallel"),
        name="mla_attention",
    )(q, kv, z1)


def _moba_kernel(q_ref, k_ref, v_ref, o_ref, kmean_ref, *, tq, heads, scale):
    bs, d = MOBA_BLOCK, A_HEAD_DIM
    n_blk = q_ref.shape[0] // bs
    per_tile = tq // bs
    for h in range(heads):
        for n in range(n_blk):
            kmean_ref[h, n:n + 1, :] = jnp.mean(k_ref[n * bs:(n + 1) * bs, h * d:(h + 1) * d], axis=0, keepdims=True)
    blk_i = lax.broadcasted_iota(jnp.int32, (tq, n_blk), 1)
    blk = blk_i.astype(F32)
    bit_of_blk = lax.shift_left(jnp.ones_like(blk_i), blk_i).astype(F32)
    row1 = lax.broadcasted_iota(jnp.int32, (tq, 1), 0)
    row_sub = sum(((row1 >= c * bs).astype(F32) for c in range(1, per_tile)), jnp.zeros((tq, 1), F32))
    col_local = lax.broadcasted_iota(jnp.int32, (tq, bs), 1)

    def load_kvs(j):
        rows = pl.ds(pl.multiple_of(j * tq, tq), tq)
        return [(k_ref[rows, h * d:(h + 1) * d].astype(BF16), v_ref[rows, h * d:(h + 1) * d].astype(BF16))
                for h in range(heads)]

    def q_tile(qi, _):
        qi = jnp.asarray(qi, jnp.int32)
        rows = pl.ds(pl.multiple_of(qi * tq, tq), tq)
        first = qi * per_tile
        own = first.astype(F32) + row_sub
        qs, sels = [], []
        for h in range(heads):
            qf = q_ref[rows, h * d:(h + 1) * d]
            gate = lax.dot_general(qf, kmean_ref[h], (((1,), (1,)), ((), ())),
                                   precision=lax.Precision.HIGHEST, preferred_element_type=F32)
            gate = jnp.where(blk < own, gate, NEG_INF)
            sel = jnp.zeros((tq, n_blk), F32)
            for _ in range(MOBA_TOPK):
                best = jnp.max(gate, axis=1, keepdims=True)
                idx = jnp.min(jnp.where(gate == best, blk, float(n_blk)), axis=1, keepdims=True)
                pick = blk == idx
                sel = jnp.where(pick & (blk < own), 1.0, sel)
                gate = jnp.where(pick, PICKED, gate)
            qs.append(qf.astype(BF16))
            sels.append(jnp.sum(sel * bit_of_blk, axis=1, keepdims=True).astype(jnp.int32))

        def chosen(h, b):
            return (lax.shift_right_logical(sels[h], jnp.broadcast_to(b, sels[h].shape)) & 1) == 1

        def by_block(s, mask_block):
            parts = [mask_block(c, s[:, c * bs:(c + 1) * bs]) for c in range(per_tile)]
            return parts[0] if per_tile == 1 else jnp.concatenate(parts, axis=1)

        def diag_mask(h, s):
            def mask_block(c, sc):
                limit = jnp.where(row_sub == c, row1 - c * bs + 1,
                                  jnp.where((row_sub > c) & chosen(h, first + c), bs, 0))
                return jnp.where(col_local < limit, sc, NEG_INF)
            return by_block(s, mask_block)

        def past_mask(h, j, s):
            jb = j * per_tile
            return by_block(s, lambda c, sc: jnp.where(chosen(h, jb + c), sc, NEG_INF))

        outs = _causal_attend(qi, qs, load_kvs, tq, d, scale, diag_mask, past_mask)
        for h in range(heads):
            o_ref[rows, h * d:(h + 1) * d] = outs[h].astype(o_ref.dtype)
        return 0

    lax.fori_loop(0, q_ref.shape[0] // tq, q_tile, 0)


def _moba_attention(z1, *, batch, seq, tq=1024, heads=2):
    t = z1.shape[0]
    assert seq % tq == 0 and tq % MOBA_BLOCK == 0
    n_groups = A_HEADS // heads
    wide = heads * A_HEAD_DIM
    assert OFF_QA == 0 and OFF_KA == A_WIDTH and OFF_VA == 2 * A_WIDTH
    return pl.pallas_call(
        functools.partial(_moba_kernel, tq=tq, heads=heads, scale=A_HEAD_DIM ** -0.5),
        grid=(batch, n_groups),
        in_specs=[pl.BlockSpec((seq, wide), lambda b, g: (b, g)),
                  pl.BlockSpec((seq, wide), lambda b, g: (b, n_groups + g)),
                  pl.BlockSpec((seq, wide), lambda b, g: (b, 2 * n_groups + g))],
        out_specs=pl.BlockSpec((seq, wide), lambda b, g: (b, g)),
        out_shape=jax.ShapeDtypeStruct((t, A_WIDTH), BF16),
        scratch_shapes=[pltpu.VMEM((heads, seq // MOBA_BLOCK, A_HEAD_DIM), F32)],
        compiler_params=_params("parallel", "parallel"),
        name="moba_attention",
    )(z1, z1, z1)


def _swa_kernel(sinks_ref, q_ref, k_ref, v_ref, o_ref, *, scale):
    w = SWA_WINDOW
    n_blk = q_ref.shape[0] // w
    pair = pl.program_id(1)
    kv_odd = ((pair * 2) // (B_HEADS // B_KV_HEADS)) % 2
    lane = lax.broadcasted_iota(jnp.int32, (w, LANES), 1)
    low = lane < B_HEAD_DIM
    keep_orig = jnp.where(low, 0, 1) == kv_odd
    row = lax.broadcasted_iota(jnp.int32, (2 * w, 2 * w), 0)
    col = lax.broadcasted_iota(jnp.int32, (2 * w, 2 * w), 1)
    rel = (row & (w - 1)) + w - col
    band = (rel >= 0) & (rel < w)
    c_exp = scale * LOG2E
    sink_raw = jnp.where(lax.broadcasted_iota(jnp.int32, (2 * w, 1), 0) < w,
                         sinks_ref[2 * pair], sinks_ref[2 * pair + 1]) / scale

    def dup(x):
        return jnp.where(keep_orig, x, pltpu.roll(x, B_HEAD_DIM, axis=1)).astype(BF16)

    def q_block(n, first):
        start = n * w
        cur = pl.ds(start if first else pl.multiple_of(start, w), w)
        prev = cur if first else pl.ds(pl.multiple_of(start - w, w), w)
        q = q_ref[cur, :]
        zero = jnp.zeros_like(q)
        q2 = jnp.concatenate([jnp.where(low, q, zero), jnp.where(low, zero, q)], axis=0).astype(BF16)
        k2 = jnp.concatenate([dup(k_ref[prev, :]), dup(k_ref[cur, :])], axis=0)
        v2 = jnp.concatenate([dup(v_ref[prev, :]), dup(v_ref[cur, :])], axis=0)
        s = lax.dot_general(q2, k2, (((1,), (1,)), ((), ())), preferred_element_type=F32)
        s = jnp.where(band & (col >= w) if first else band, s, NEG_INF)
        m = jnp.maximum(jnp.max(s, axis=1, keepdims=True), sink_raw)
        p = jnp.exp2((s - m) * c_exp)
        denom = jnp.sum(p, axis=1, keepdims=True) + jnp.exp2((sink_raw - m) * c_exp)
        o2 = jnp.dot(p.astype(BF16), v2, preferred_element_type=F32) / denom
        o_ref[cur, :] = jnp.where(low, o2[:w], o2[w:]).astype(o_ref.dtype)

    for u in range(SWA_UNROLL):
        q_block(u, u == 0)

    def group(g, _):
        g = jnp.asarray(g, jnp.int32)
        for u in range(SWA_UNROLL):
            q_block(g * SWA_UNROLL + u, False)
        return 0

    lax.fori_loop(1, n_blk // SWA_UNROLL, group, 0)


def _swa_attention(z1, sinks, *, batch, seq):
    t = z1.shape[0]
    pairs = B_HEADS // 2
    rep = B_HEADS // B_KV_HEADS
    assert (seq // SWA_WINDOW) % SWA_UNROLL == 0
    kv_group = lambda p: (2 * p) // rep // 2
    return pl.pallas_call(
        functools.partial(_swa_kernel, scale=B_HEAD_DIM ** -0.5),
        grid_spec=pltpu.PrefetchScalarGridSpec(
            num_scalar_prefetch=1,
            grid=(batch, pairs),
            in_specs=[pl.BlockSpec((seq, LANES), lambda b, p, sinks: (b, OFF_QB // LANES + p)),
                      pl.BlockSpec((seq, LANES), lambda b, p, sinks: (b, OFF_KB // LANES + kv_group(p))),
                      pl.BlockSpec((seq, LANES), lambda b, p, sinks: (b, OFF_VB // LANES + kv_group(p)))],
            out_specs=pl.BlockSpec((seq, LANES), lambda b, p, sinks: (b, p)),
        ),
        out_shape=jax.ShapeDtypeStruct((t, B_WIDTH), BF16),
        compiler_params=_params("parallel", "parallel"),
        name="swa_attention",
    )(sinks, z1, z1, z1)


def _gated_out_kernel(oa_ref, ob_ref, oc_ref, ga_ref, gb_ref, gc_ref, wa_ref, wb_ref, wc_ref, y_ref,
                      wa_bf, wb_bf, wc_bf):
    @pl.when(pl.program_id(1) == 0)
    def _():
        wa_bf[...] = wa_ref[...].astype(BF16)
        wb_bf[...] = wb_ref[...].astype(BF16)
        wc_bf[...] = wc_ref[...].astype(BF16)

    y = jax.nn.sigmoid(ga_ref[...]) * jnp.dot(oa_ref[...], wa_bf[...], preferred_element_type=F32)
    y += jax.nn.sigmoid(gb_ref[...]) * jnp.dot(ob_ref[...], wb_bf[...], preferred_element_type=F32)
    y += jax.nn.sigmoid(gc_ref[...]) * jnp.dot(oc_ref[...], wc_bf[...], preferred_element_type=F32)
    y_ref[...] = y.astype(y_ref.dtype)


def _gated_out(oa, ob, oc, gates, w_out_a, w_out_b, w_out_c, layer, *, tm=512, tn=512):
    t = oa.shape[0]
    d = w_out_a.shape[2]
    nb = d // tn
    o_spec = lambda width: pl.BlockSpec((tm, width), lambda j, i: (i, 0))
    g_spec = lambda which: pl.BlockSpec((tm, tn), lambda j, i: (i, which * nb + j))
    w_spec = lambda width: pl.BlockSpec((None, width, tn), lambda j, i: (layer, 0, j))
    return pl.pallas_call(
        _gated_out_kernel,
        grid=(nb, t // tm),
        in_specs=[o_spec(A_WIDTH), o_spec(B_WIDTH), o_spec(C_WIDTH), g_spec(0), g_spec(1), g_spec(2),
                  w_spec(A_WIDTH), w_spec(B_WIDTH), w_spec(C_WIDTH)],
        out_specs=pl.BlockSpec((tm, tn), lambda j, i: (i, j)),
        out_shape=jax.ShapeDtypeStruct((t, d), BF16),
        scratch_shapes=[pltpu.VMEM((A_WIDTH, tn), BF16), pltpu.VMEM((B_WIDTH, tn), BF16),
                        pltpu.VMEM((C_WIDTH, tn), BF16)],
        compiler_params=_params("arbitrary", "arbitrary"),
        name="gated_out",
    )(oa, ob, oc, gates, gates, gates, w_out_a, w_out_b, w_out_c)


def _router_kernel(h_ref, g_ref, w_ref, b_ref, ids_ref, wts_ref):
    x = h_ref[...]
    hn = x * lax.rsqrt(jnp.mean(x * x, axis=-1, keepdims=True) + NORM_EPS) * g_ref[...]
    logits = jnp.dot(hn, w_ref[...], precision=lax.Precision.HIGHEST, preferred_element_type=F32) + b_ref[...]
    lane = lax.broadcasted_iota(jnp.int32, logits.shape, 1)
    far = float(LANES)
    is_g = lane < N_GROUPS
    g_id = lane.astype(F32)
    gmax = jnp.max(jnp.where(is_g, logits, NEG_INF), axis=1, keepdims=True)
    gsel = jnp.min(jnp.where(is_g & (logits == gmax), g_id, far), axis=1, keepdims=True)
    p_g = 1.0 / jnp.sum(jnp.where(is_g, jnp.exp(logits - gmax), 0.0), axis=1, keepdims=True)
    e_lane = lane - N_GROUPS
    e_id = e_lane.astype(F32)
    e_group = jnp.right_shift(e_lane, 2).astype(F32)
    in_grp = (e_lane >= 0) & (e_lane < N_EXPERTS) & (e_group == gsel)
    emax = jnp.max(jnp.where(in_grp, logits, NEG_INF), axis=1, keepdims=True)
    ee = jnp.where(in_grp, jnp.exp(jnp.where(in_grp, logits, emax) - emax), 0.0)
    ep = ee / jnp.sum(ee, axis=1, keepdims=True)
    p1 = jnp.max(jnp.where(in_grp, ep, -1.0), axis=1, keepdims=True)
    i1 = jnp.min(jnp.where(in_grp & (ep == p1), e_id, far), axis=1, keepdims=True)
    rest = in_grp & (e_id != i1)
    p2 = jnp.max(jnp.where(rest, ep, -1.0), axis=1, keepdims=True)
    i2 = jnp.min(jnp.where(rest & (ep == p2), e_id, far), axis=1, keepdims=True)
    tot = p1 + p2
    ids_ref[...] = jnp.where(lane == 0, i1, jnp.where(lane == 1, i2, 0.0)).astype(jnp.int32)
    wts_ref[...] = jnp.where(lane == 0, p_g * p1 / tot, jnp.where(lane == 1, p_g * p2 / tot, 0.0))


def _router(h, g, w_r, b_r, layer, *, tm=256):
    t, d = h.shape
    out = pl.BlockSpec((tm, LANES), lambda i: (i, 0))
    return pl.pallas_call(
        _router_kernel,
        grid=(t // tm,),
        in_specs=[pl.BlockSpec((tm, d), lambda i: (i, 0)),
                  pl.BlockSpec((None, 1, d), lambda i: (layer, 0, 0)),
                  pl.BlockSpec((None, d, LANES), lambda i: (layer, 0, 0)),
                  pl.BlockSpec((None, 1, LANES), lambda i: (layer, 0, 0))],
        out_specs=[out, out],
        out_shape=[jax.ShapeDtypeStruct((t, LANES), jnp.int32), jax.ShapeDtypeStruct((t, LANES), F32)],
        compiler_params=_params("parallel"),
        name="moe_router",
    )(h, g, w_r, b_r)


def _row_copy(src_hbm, tok, dst_ref, r, sem):
    return pltpu.make_async_copy(src_hbm.at[pl.ds(tok, 1), :], dst_ref.at[pl.ds(r, 1), :], sem)


GATHER_UNROLL = 8
NORM_CHUNK = 16


def _dispatch_kernel(tok_ref, nvalid_ref, h_hbm, g_ref, o_ref, rows_ref, sem):
    i = pl.program_id(0)
    n_valid = nvalid_ref[0]
    n_rows = rows_ref.shape[1]
    slot = lax.rem(i, 2)

    def fetch(blk, slot):
        base = blk * n_rows

        def issue(g, _):
            for u in range(GATHER_UNROLL):
                r = g * GATHER_UNROLL + u
                _row_copy(h_hbm, tok_ref[base + r], rows_ref.at[slot], r, sem.at[slot]).start()
            return 0

        lax.fori_loop(0, n_rows // GATHER_UNROLL, issue, 0)

    @pl.when(i == 0)
    def _():
        fetch(0, 0)

    @pl.when(i + 1 < n_valid)
    def _():
        fetch(i + 1, 1 - slot)

    @pl.when(i < n_valid)
    def _():
        def drain(r, _):
            _row_copy(h_hbm, 0, rows_ref.at[slot], r, sem.at[slot]).wait()
            return 0

        lax.fori_loop(0, n_rows, drain, 0, unroll=GATHER_UNROLL)

        def norm(c, _):
            rows = pl.ds(pl.multiple_of(jnp.asarray(c, jnp.int32) * NORM_CHUNK, NORM_CHUNK), NORM_CHUNK)
            x = rows_ref[slot, rows, :]
            hn = x * lax.rsqrt(jnp.mean(x * x, axis=-1, keepdims=True) + NORM_EPS) * g_ref[...]
            o_ref[rows, :] = hn.astype(o_ref.dtype)
            return 0

        lax.fori_loop(0, n_rows // NORM_CHUNK, norm, 0, unroll=4)


def _dispatch(h, g, layer, buf_tok, nvalid, *, n_blocks):
    t, d = h.shape
    blk = lambda i, tok, nv: (jnp.minimum(i, nv[0] - 1), 0)
    return pl.pallas_call(
        _dispatch_kernel,
        grid_spec=pltpu.PrefetchScalarGridSpec(
            num_scalar_prefetch=2,
            grid=(n_blocks,),
            in_specs=[pl.BlockSpec(memory_space=pl.ANY),
                      pl.BlockSpec((None, 1, d), lambda i, tok, nv: (layer, 0, 0))],
            out_specs=pl.BlockSpec((MOE_ROWS, d), blk),
            scratch_shapes=[pltpu.VMEM((2, MOE_ROWS, d), F32), pltpu.SemaphoreType.DMA((2,))],
        ),
        out_shape=jax.ShapeDtypeStruct((n_blocks * MOE_ROWS, d), BF16),
        compiler_params=_params("arbitrary"),
        name="moe_dispatch",
    )(buf_tok, nvalid, h, g)


def _run_weights(be_ref, first_ref, next_ref, i, pass_id, n_passes, n_blocks, copies, stage):
    @pl.when((pass_id == 0) & (i == 0))
    def _():
        for c in copies(be_ref[0], 0):
            c.start()

    @pl.when(first_ref[i] == 1)
    def _():
        for c in copies(be_ref[i], pass_id):
            c.wait()
        stage()
        nxt = next_ref[i]
        same_pass = nxt < n_blocks
        e_next = jnp.where(same_pass, be_ref[jnp.minimum(nxt, n_blocks - 1)], be_ref[0])
        pass_next = jnp.where(same_pass, pass_id, pass_id + 1)

        @pl.when(same_pass | (pass_id + 1 < n_passes))
        def _():
            for c in copies(e_next, pass_next):
                c.start()


def _expert_up_kernel(be_ref, nvalid_ref, first_ref, next_ref, x_ref, wg_hbm, wu_hbm, o_ref,
                      wg_land, wu_land, wg_bf, wu_bf, sem, *, layer):
    f, i = pl.program_id(0), pl.program_id(1)
    tf = wg_land.shape[1]

    def copies(e, ff_tile):
        cols = pl.ds(pl.multiple_of(ff_tile * tf, tf), tf)
        return (pltpu.make_async_copy(wg_hbm.at[layer, e, :, cols], wg_land, sem.at[0]),
                pltpu.make_async_copy(wu_hbm.at[layer, e, :, cols], wu_land, sem.at[1]))

    def stage():
        wg_bf[...] = wg_land[...].astype(BF16)
        wu_bf[...] = wu_land[...].astype(BF16)

    _run_weights(be_ref, first_ref, next_ref, i, f, pl.num_programs(0), pl.num_programs(1), copies, stage)

    @pl.when(i < nvalid_ref[0])
    def _():
        x = x_ref[...]
        hg = jnp.dot(x, wg_bf[...], preferred_element_type=F32)
        hu = jnp.dot(x, wu_bf[...], preferred_element_type=F32)
        o_ref[...] = (jax.nn.silu(hg) * hu).astype(o_ref.dtype)


def _expert_up(xb, w_gate, w_up, layer, runs, *, n_blocks):
    d = xb.shape[1]
    ff = w_gate.shape[3]
    tf = MOE_FF_TILE
    row_blk = lambda f, i, be, nv, first, nxt: (jnp.minimum(i, nv[0] - 1), 0)
    return pl.pallas_call(
        functools.partial(_expert_up_kernel, layer=layer),
        grid_spec=pltpu.PrefetchScalarGridSpec(
            num_scalar_prefetch=4,
            grid=(ff // tf, n_blocks),
            in_specs=[pl.BlockSpec((MOE_ROWS, d), row_blk), pl.BlockSpec(memory_space=pl.ANY),
                      pl.BlockSpec(memory_space=pl.ANY)],
            out_specs=pl.BlockSpec((MOE_ROWS, tf),
                                   lambda f, i, be, nv, first, nxt: (jnp.minimum(i, nv[0] - 1), f)),
            scratch_shapes=[pltpu.VMEM((d, tf), F32), pltpu.VMEM((d, tf), F32),
                            pltpu.VMEM((d, tf), BF16), pltpu.VMEM((d, tf), BF16),
                            pltpu.SemaphoreType.DMA((2,))],
        ),
        out_shape=jax.ShapeDtypeStruct((n_blocks * MOE_ROWS, ff), BF16),
        compiler_params=_params("arbitrary", "arbitrary"),
        name="moe_expert_up",
    )(*runs, xb, w_gate, w_up)


def _expert_down_kernel(be_ref, nvalid_ref, first_ref, next_ref, a_ref, wd_hbm, o_ref, wd_land, wd_bf, sem,
                        *, layer):
    i = pl.program_id(0)

    def copies(e, _):
        return (pltpu.make_async_copy(wd_hbm.at[layer, e], wd_land, sem.at[0]),)

    def stage():
        wd_bf[...] = wd_land[...].astype(BF16)

    _run_weights(be_ref, first_ref, next_ref, i, 0, 1, pl.num_programs(0), copies, stage)

    @pl.when(i < nvalid_ref[0])
    def _():
        o_ref[...] = jnp.dot(a_ref[...], wd_bf[...], preferred_element_type=F32)


def _expert_down(act, w_down, layer, runs, *, n_blocks):
    ff, d = w_down.shape[2], w_down.shape[3]
    row_blk = lambda i, be, nv, first, nxt: (jnp.minimum(i, nv[0] - 1), 0)
    return pl.pallas_call(
        functools.partial(_expert_down_kernel, layer=layer),
        grid_spec=pltpu.PrefetchScalarGridSpec(
            num_scalar_prefetch=4,
            grid=(n_blocks,),
            in_specs=[pl.BlockSpec((MOE_ROWS, ff), row_blk), pl.BlockSpec(memory_space=pl.ANY)],
            out_specs=pl.BlockSpec((MOE_ROWS, d), row_blk),
            scratch_shapes=[pltpu.VMEM((ff, d), F32), pltpu.VMEM((ff, d), BF16), pltpu.SemaphoreType.DMA((1,))],
        ),
        out_shape=jax.ShapeDtypeStruct((n_blocks * MOE_ROWS, d), F32),
        compiler_params=_params("arbitrary"),
        name="moe_expert_down",
    )(*runs, act, w_down)


def _combine_kernel(pos_ref, h_ref, wts_ref, g_ref, y_hbm, *refs, emit_h):
    o_ref = refs[0] if emit_h else None
    n_ref, y0_ref, y1_ref, sem = refs[int(emit_h):]
    i = pl.program_id(0)
    tm = h_ref.shape[0]
    slot = lax.rem(i, 2)

    def fetch(blk, slot):
        base = blk * tm * EXPERT_TOPK

        def issue(g, _):
            for u in range(GATHER_UNROLL):
                r = g * GATHER_UNROLL + u
                _row_copy(y_hbm, pos_ref[base + EXPERT_TOPK * r], y0_ref.at[slot], r, sem.at[slot]).start(priority=0)
                _row_copy(y_hbm, pos_ref[base + EXPERT_TOPK * r + 1], y1_ref.at[slot], r,
                          sem.at[slot]).start(priority=1)
            return 0

        lax.fori_loop(0, tm // GATHER_UNROLL, issue, 0)

    @pl.when(i == 0)
    def _():
        fetch(0, 0)

    @pl.when(i + 1 < pl.num_programs(0))
    def _():
        fetch(i + 1, 1 - slot)

    def drain(r, _):
        _row_copy(y_hbm, 0, y0_ref.at[slot], r, sem.at[slot]).wait()
        _row_copy(y_hbm, 0, y1_ref.at[slot], r, sem.at[slot]).wait()
        return 0

    lax.fori_loop(0, tm, drain, 0, unroll=GATHER_UNROLL)

    def rows_chunk(c, _):
        rows = pl.ds(pl.multiple_of(jnp.asarray(c, jnp.int32) * NORM_CHUNK, NORM_CHUNK), NORM_CHUNK)
        w = wts_ref[rows, :]
        hv = h_ref[rows, :] + w[:, 0:1] * y0_ref[slot, rows, :] + w[:, 1:2] * y1_ref[slot, rows, :]
        if emit_h:
            o_ref[rows, :] = hv
        hn = hv * lax.rsqrt(jnp.mean(hv * hv, axis=-1, keepdims=True) + NORM_EPS) * g_ref[...]
        n_ref[rows, :] = hn.astype(n_ref.dtype)
        return 0

    lax.fori_loop(0, tm // NORM_CHUNK, rows_chunk, 0, unroll=2)


def _combine(h, wts, y, pos, g, *, norm_dtype, emit_h, tm=128):
    t, d = h.shape
    row = pl.BlockSpec((tm, d), lambda i, pos: (i, 0))
    outs = pl.pallas_call(
        functools.partial(_combine_kernel, emit_h=emit_h),
        grid_spec=pltpu.PrefetchScalarGridSpec(
            num_scalar_prefetch=1,
            grid=(t // tm,),
            in_specs=[row, pl.BlockSpec((tm, LANES), lambda i, pos: (i, 0)),
                      pl.BlockSpec((1, d), lambda i, pos: (0, 0)), pl.BlockSpec(memory_space=pl.ANY)],
            out_specs=[row] * (1 + emit_h),
            scratch_shapes=[pltpu.VMEM((2, tm, d), F32), pltpu.VMEM((2, tm, d), F32),
                            pltpu.SemaphoreType.DMA((2,))],
        ),
        out_shape=([jax.ShapeDtypeStruct((t, d), F32)] if emit_h else [])
        + [jax.ShapeDtypeStruct((t, d), norm_dtype)],
        compiler_params=_params("arbitrary"),
        name="moe_combine",
    )(pos, h, wts, g.reshape(1, d), y)
    return (outs[0], outs[1]) if emit_h else (None, outs[0])


def _hier_moe(h, ffn_norm_g, w_r, b_r, w_gate, w_up, w_down, layer, next_norm_g, *, norm_dtype, emit_h):
    t, d = h.shape
    tk = t * EXPERT_TOPK
    n_blocks = tk // MOE_ROWS + N_EXPERTS
    ids, wts = _router(h, ffn_norm_g, w_r, b_r, layer)
    flat_e = ids[:, :EXPERT_TOPK].reshape(tk)
    onehot = (flat_e[:, None] == jnp.arange(N_EXPERTS, dtype=jnp.int32)[None, :]).astype(jnp.int32)
    csum = jnp.cumsum(onehot, axis=0)
    rank = jnp.sum(onehot * csum, axis=1) - 1
    counts = csum[-1]
    padded = (counts + MOE_ROWS - 1) // MOE_ROWS * MOE_ROWS
    pad_end = jnp.cumsum(padded)
    pad_start = pad_end - padded
    dest = (pad_start[flat_e] + rank).astype(jnp.int32)
    flat_tok = jnp.arange(tk, dtype=jnp.int32) // EXPERT_TOPK
    buf_tok = jnp.zeros((n_blocks * MOE_ROWS,), jnp.int32).at[dest].set(flat_tok)
    nvalid = (pad_end[-1] // MOE_ROWS).astype(jnp.int32).reshape(1)
    blk_start = jnp.minimum(jnp.arange(n_blocks, dtype=jnp.int32), nvalid[0] - 1) * MOE_ROWS
    block_expert = jnp.minimum(jnp.searchsorted(pad_end, blk_start, side='right'), N_EXPERTS - 1).astype(jnp.int32)

    blk = jnp.arange(n_blocks, dtype=jnp.int32)
    prev_expert = jnp.concatenate([jnp.full((1,), -1, jnp.int32), block_expert[:-1]])
    run_first = (blk < nvalid[0]) & (block_expert != prev_expert)
    first_at_or_after = lax.cummin(jnp.where(run_first, blk, n_blocks), reverse=True)
    run_next = jnp.concatenate([first_at_or_after[1:], jnp.full((1,), n_blocks, jnp.int32)])
    runs = (block_expert, nvalid, run_first.astype(jnp.int32), run_next)

    xb = _dispatch(h, ffn_norm_g, layer, buf_tok, nvalid, n_blocks=n_blocks)
    act = _expert_up(xb, w_gate, w_up, layer, runs, n_blocks=n_blocks)
    y = _expert_down(act, w_down, layer, runs, n_blocks=n_blocks)
    return _combine(h, wts, y, dest, next_norm_g, norm_dtype=norm_dtype, emit_h=emit_h)


ROPE_A, ROPE_B, ROPE_PE = 0, 1, 2


def _in_proj_rope_table(col):
    if OFF_QA <= col < OFF_VA:
        return ROPE_A
    if OFF_QB <= col < OFF_VB:
        return ROPE_B
    if col == OFF_KPE:
        return ROPE_PE
    return None


def _mla_q_rope_table(col):
    return 0 if (col // LANES) % 2 == 1 else None


def kernel(x, positions, attn_norm_g, w_in, q_norm_g, kv_norm_g, wq_b, wkv_b, sinks, w_out_a, w_out_b,
           w_out_c, w_o, ffn_norm_g, w_group, b_group, w_expert, b_expert, w_gate, w_up, w_down,
           final_norm_g):
    batch, seq, d = x.shape
    depth = w_in.shape[0]
    t = batch * seq
    h = x.reshape(t, d)
    pos = positions.reshape(t).astype(F32)

    w_in_t = jnp.swapaxes(w_in, 1, 2)
    q_head = C_NOPE_DIM + C_ROPE_DIM
    wq_pad = jnp.pad(wq_b.reshape(depth, C_Q_RANK, C_HEADS, q_head),
                     ((0, 0), (0, 0), (0, 0), (0, 2 * LANES - q_head))
                     ).reshape(depth, C_Q_RANK, C_HEADS * 2 * LANES).astype(BF16)
    w_r = jnp.concatenate([w_group, w_expert,
                           jnp.zeros((depth, d, LANES - N_GROUPS - N_EXPERTS), F32)], axis=2)
    b_r = jnp.concatenate([b_group, b_expert,
                           jnp.zeros((depth, LANES - N_GROUPS - N_EXPERTS), F32)], axis=1).reshape(depth, 1, LANES)
    attn_g = attn_norm_g.reshape(depth, 1, d)
    ffn_g = ffn_norm_g.reshape(depth, 1, d)

    c_a, s_a = _rope_tables(pos, A_HEAD_DIM)
    c_b, s_b = _rope_tables(pos, B_HEAD_DIM)
    c_pe, s_pe = _rope_tables(pos, C_ROPE_DIM, keep_lanes=C_ROPE_DIM)
    rope_tables = ((c_a, s_a, A_HEAD_DIM // 2), (c_b, s_b, B_HEAD_DIM // 2), (c_pe, s_pe, C_ROPE_DIM // 2))

    hn = _rmsnorm(h, attn_norm_g[0])
    for l in range(depth):
        z1 = _matmul(hn, w_in_t, l, n_out=Z1_WIDTH, w_is_nk=True, rope_tables=rope_tables,
                     rope_group_table=_in_proj_rope_table, name="in_proj")
        gates = _matmul(hn, w_in_t, l, n_out=3 * d, col_off=OFF_GATES, w_is_nk=True, name="gate_proj")
        o_a = _moba_attention(z1, batch=batch, seq=seq)
        o_b = _swa_attention(z1, sinks[l], batch=batch, seq=seq)
        cq_n = _rmsnorm(z1, q_norm_g[l], col_off=OFF_CQ, width=C_Q_RANK, piece=512)
        ckv_n = _rmsnorm(z1, kv_norm_g[l], col_off=OFF_CKV, width=C_KV_RANK, piece=512)
        q_c = _matmul(cq_n, wq_pad, l, n_out=C_HEADS * 2 * LANES, tn=1024, out_dtype=BF16,
                      rope_tables=rope_tables[ROPE_PE:], rope_group_table=_mla_q_rope_table, name="mla_q_proj")
        kv_c = _matmul(ckv_n, wkv_b, l, n_out=C_HEADS * (C_NOPE_DIM + C_V_DIM), tn=1024, out_dtype=BF16,
                       name="mla_kv_proj")
        o_c = _mla_attention(q_c, kv_c, z1, batch=batch, seq=seq)
        y = _gated_out(o_a, o_b, o_c, gates, w_out_a, w_out_b, w_out_c, l)
        h = _matmul(y, w_o, l, n_out=d, residual=h, name="out_proj")
        last = l == depth - 1
        h, hn = _hier_moe(h, ffn_g, w_r, b_r, w_gate, w_up, w_down, l,
                          final_norm_g if last else attn_norm_g[l + 1],
                          norm_dtype=F32 if last else BF16, emit_h=not last)
    return hn.reshape(batch, seq, d)
```

```python
import functools

import jax
import jax.numpy as jnp
from jax import lax
from jax.experimental import pallas as pl
from jax.experimental.pallas import tpu as pltpu

F32 = jnp.float32
BF16 = jnp.bfloat16

ROPE_THETA = 10000.0
NORM_EPS = 1e-6
NEG_INF = -1e30
PICKED = -3e38
LOG2E = 1.4426950408889634

A_HEADS, A_HEAD_DIM = 16, 128
MOBA_BLOCK, MOBA_TOPK = 256, 3
B_HEADS, B_KV_HEADS, B_HEAD_DIM, SWA_WINDOW = 32, 4, 64, 128
C_HEADS, C_Q_RANK, C_KV_RANK, C_NOPE_DIM, C_ROPE_DIM, C_V_DIM = 16, 1024, 512, 128, 64, 128
N_GROUPS, EXPERTS_PER_GROUP, EXPERT_TOPK, D_FF_EXPERT = 8, 4, 2, 768
N_EXPERTS = N_GROUPS * EXPERTS_PER_GROUP
assert EXPERTS_PER_GROUP == 4

LANES = 128
VMEM_LIMIT_BYTES = 58 * 1024 * 1024

A_WIDTH = A_HEADS * A_HEAD_DIM
B_WIDTH = B_HEADS * B_HEAD_DIM
B_KV_WIDTH = B_KV_HEADS * B_HEAD_DIM
C_WIDTH = C_HEADS * C_V_DIM
OFF_QA = 0
OFF_KA = OFF_QA + A_WIDTH
OFF_VA = OFF_KA + A_WIDTH
OFF_QB = OFF_VA + A_WIDTH
OFF_KB = OFF_QB + B_WIDTH
OFF_VB = OFF_KB + B_KV_WIDTH
OFF_CQ = OFF_VB + B_KV_WIDTH
OFF_CKV = OFF_CQ + C_Q_RANK
OFF_KPE = OFF_CKV + C_KV_RANK
OFF_GATES = OFF_KPE + C_ROPE_DIM
Z1_WIDTH = 10752

MOE_ROWS = 256
MOE_FF_TILE = 256
SWA_UNROLL = 8


def _params(*sem):
    return pltpu.CompilerParams(dimension_semantics=sem, vmem_limit_bytes=VMEM_LIMIT_BYTES)


def _rmsnorm_kernel(*refs, n_pieces, width):
    x_refs, g_ref, o_ref = refs[:n_pieces], refs[n_pieces], refs[n_pieces + 1]
    xs = [r[...].astype(F32) for r in x_refs]
    ss = sum(jnp.sum(x * x, axis=-1, keepdims=True) for x in xs)
    inv = lax.rsqrt(ss * (1.0 / width) + NORM_EPS)
    pw = xs[0].shape[1]
    for p, x in enumerate(xs):
        o_ref[:, p * pw:(p + 1) * pw] = (x * inv * g_ref[:, p * pw:(p + 1) * pw]).astype(o_ref.dtype)


def _rmsnorm(x, g, *, col_off=0, width=None, piece=None, tm=256, out_dtype=BF16):
    t = x.shape[0]
    width = width or x.shape[1]
    piece = piece or width
    n_pieces = width // piece
    off = col_off // piece
    assert col_off % piece == 0 and width % piece == 0
    in_specs = [pl.BlockSpec((tm, piece), functools.partial(lambda i, p: (i, off + p), p=p))
                for p in range(n_pieces)]
    in_specs.append(pl.BlockSpec((1, width), lambda i: (0, 0)))
    return pl.pallas_call(
        functools.partial(_rmsnorm_kernel, n_pieces=n_pieces, width=width),
        grid=(t // tm,),
        in_specs=in_specs,
        out_specs=pl.BlockSpec((tm, width), lambda i: (i, 0)),
        out_shape=jax.ShapeDtypeStruct((t, width), out_dtype),
        compiler_params=_params("parallel"),
        name="rmsnorm",
    )(*([x] * n_pieces), g.reshape(1, width))


def _stage_weight(w_ref, wbf_ref, w_is_nk):
    if not w_is_nk:
        wbf_ref[...] = w_ref[...].astype(BF16)
        return
    _, tn, k = w_ref.shape
    for c in range(k // tn):
        wbf_ref[c * tn:(c + 1) * tn, :] = w_ref[0, :, c * tn:(c + 1) * tn].T.astype(BF16)


def _swap_halves(x, half):
    if half == 64:
        return pltpu.roll(x, 64, axis=1)
    lane = lax.broadcasted_iota(jnp.int32, x.shape, 1)
    return jnp.where((lane & half) == 0, pltpu.roll(x, LANES - half, axis=1), pltpu.roll(x, half, axis=1))


def _matmul_kernel(*refs, stage_w, w_is_nk, has_res, rope_modes, rope_halves):
    x_ref, w_ref = refs[0], refs[1]
    n_in = 2 + has_res + 2 * len(rope_halves)
    r_ref = refs[2] if has_res else None
    table_refs = refs[2 + has_res:n_in]
    o_ref = refs[n_in]
    if stage_w:
        wbf_ref = refs[n_in + 1]

        @pl.when(pl.program_id(1) == 0)
        def _():
            _stage_weight(w_ref, wbf_ref, w_is_nk)

        w = wbf_ref[...]
    else:
        w = w_ref[...]
    acc = jnp.dot(x_ref[...], w, preferred_element_type=F32)
    if has_res:
        acc = acc + r_ref[...]
    if rope_modes is None:
        o_ref[...] = acc.astype(o_ref.dtype)
        return
    j = pl.program_id(0)
    for pattern in sorted(set(rope_modes), key=str):
        blocks = [b for b, p in enumerate(rope_modes) if p == pattern]

        @pl.when(functools.reduce(jnp.logical_or, [j == b for b in blocks]))
        def _(pattern=pattern):
            for g, table in enumerate(pattern):
                xg = acc[:, g * LANES:(g + 1) * LANES]
                if table is not None:
                    c_ref, s_ref = table_refs[2 * table], table_refs[2 * table + 1]
                    xg = xg * c_ref[...] + _swap_halves(xg, rope_halves[table]) * s_ref[...]
                o_ref[:, g * LANES:(g + 1) * LANES] = xg.astype(o_ref.dtype)


def _matmul(x, w, layer, *, n_out, col_off=0, w_is_nk=False, tm=1024, tn=512, out_dtype=F32, residual=None,
            rope_tables=(), rope_group_table=None, name="matmul"):
    m, k = x.shape
    assert w.shape[2 if w_is_nk else 1] == k and m % tm == 0 and n_out % tn == 0
    stage_w = w_is_nk or w.dtype != BF16
    rope_modes = None
    if rope_group_table is not None:
        rope_modes = tuple(tuple(rope_group_table(b * tn + g * LANES) for g in range(tn // LANES))
                           for b in range(n_out // tn))
    if w_is_nk:
        assert k % tn == 0 and col_off % 8 == 0
        w_spec = pl.BlockSpec((pl.Element(1), pl.Element(tn), pl.Element(k)),
                              lambda j, i: (layer, pl.multiple_of(col_off + j * tn, 8), 0))
    else:
        assert col_off % tn == 0
        w_spec = pl.BlockSpec((None, k, tn), lambda j, i: (layer, 0, j + col_off // tn))
    in_specs = [pl.BlockSpec((tm, k), lambda j, i: (i, 0)), w_spec]
    args = [x, w]
    if residual is not None:
        in_specs.append(pl.BlockSpec((tm, tn), lambda j, i: (i, j)))
        args.append(residual)
    for c, s, _ in rope_tables:
        in_specs += [pl.BlockSpec((tm, LANES), lambda j, i: (i, 0))] * 2
        args += [c, s]
    return pl.pallas_call(
        functools.partial(_matmul_kernel, stage_w=stage_w, w_is_nk=w_is_nk, has_res=residual is not None,
                          rope_modes=rope_modes, rope_halves=tuple(half for _, _, half in rope_tables)),
        grid=(n_out // tn, m // tm),
        in_specs=in_specs,
        out_specs=pl.BlockSpec((tm, tn), lambda j, i: (i, j)),
        out_shape=jax.ShapeDtypeStruct((m, n_out), out_dtype),
        scratch_shapes=[pltpu.VMEM((k, tn), BF16)] if stage_w else [],
        compiler_params=_params("arbitrary", "arbitrary"),
        name=name,
    )(*args)


def _rope_tables_kernel(pos_ref, invf_ref, sign_ref, keep_ref, c_ref, s_ref):
    ang = pos_ref[...] * invf_ref[...]
    c_ref[...] = jnp.cos(ang) * keep_ref[...]
    s_ref[...] = jnp.sin(ang) * sign_ref[...]


def _rope_tables(pos, dim, *, keep_lanes=LANES):
    t = pos.shape[0]
    half = dim // 2
    lane = jnp.arange(LANES)
    invf = (ROPE_THETA ** (-(2.0 * (lane % half)).astype(F32) / dim)).reshape(1, LANES)
    keep = (lane < keep_lanes).astype(F32).reshape(1, LANES)
    sign = jnp.where((lane % dim) < half, -1.0, 1.0).astype(F32).reshape(1, LANES) * keep
    tm = 1024
    row = pl.BlockSpec((1, LANES), lambda i: (0, 0))
    out = pl.BlockSpec((tm, LANES), lambda i: (i, 0))
    return pl.pallas_call(
        _rope_tables_kernel,
        grid=(t // tm,),
        in_specs=[pl.BlockSpec((tm, 1), lambda i: (i, 0)), row, row, row],
        out_specs=[out, out],
        out_shape=[jax.ShapeDtypeStruct((t, LANES), F32)] * 2,
        compiler_params=_params("parallel"),
        name="rope_tables",
    )(pos.reshape(t, 1), invf, sign, keep)


def _online_block(q, k, v, s_mask, carry, c_exp):
    m, l, acc = carry
    s = s_mask(lax.dot_general(q, k, (((1,), (1,)), ((), ())), preferred_element_type=F32))
    m_new = jnp.maximum(m, jnp.max(s, axis=1, keepdims=True))
    alpha = jnp.exp2((m - m_new) * c_exp)
    p = jnp.exp2((s - m_new) * c_exp)
    l = alpha * l + jnp.sum(p, axis=1, keepdims=True)
    acc = alpha * acc + jnp.dot(p.astype(BF16), v, preferred_element_type=F32)
    return m_new, l, acc


def _causal_attend(qi, qs, load_kvs, tq, dv, scale, diag_mask, past_mask):
    heads = range(len(qs))
    c_exp = scale * LOG2E
    init = (jnp.full((tq, 1), NEG_INF, F32), jnp.zeros((tq, 1), F32), jnp.zeros((tq, dv), F32))
    kvs = load_kvs(qi)
    carries = tuple(_online_block(qs[h], *kvs[h], functools.partial(diag_mask, h), init, c_exp) for h in heads)

    def past(j, carries):
        j = jnp.asarray(j, jnp.int32)
        kvs = load_kvs(j)
        return tuple(_online_block(qs[h], *kvs[h], functools.partial(past_mask, h, j), carries[h], c_exp)
                     for h in heads)

    carries = lax.fori_loop(0, qi, past, carries)
    return [acc / l for _, l, acc in carries]


def _mla_kernel(q_ref, kv_ref, kpe_ref, o_ref, *, tq, heads, scale):
    n_q = q_ref.shape[0] // tq
    qw, dv = 2 * LANES, C_V_DIM
    row = lax.broadcasted_iota(jnp.int32, (tq, tq), 0)
    col = lax.broadcasted_iota(jnp.int32, (tq, tq), 1)
    causal = col <= row

    def load_kvs(j):
        rows = pl.ds(pl.multiple_of(j * tq, tq), tq)
        kpe = kpe_ref[rows, :].astype(BF16)
        return [(jnp.concatenate([kv_ref[rows, h * qw:h * qw + C_NOPE_DIM], kpe], axis=1),
                 kv_ref[rows, h * qw + C_NOPE_DIM:(h + 1) * qw]) for h in range(heads)]

    def q_tile(qi, _):
        qi = jnp.asarray(qi, jnp.int32)
        rows = pl.ds(pl.multiple_of(qi * tq, tq), tq)
        qs = [q_ref[rows, h * qw:(h + 1) * qw] for h in range(heads)]
        outs = _causal_attend(qi, qs, load_kvs, tq, dv, scale,
                              lambda h, s: jnp.where(causal, s, NEG_INF), lambda h, j, s: s)
        for h in range(heads):
            o_ref[rows, h * dv:(h + 1) * dv] = outs[h].astype(o_ref.dtype)
        return 0

    lax.fori_loop(0, n_q, q_tile, 0)


def _mla_attention(q, kv, z1, *, batch, seq, tq=1024, heads=2):
    t = q.shape[0]
    scale = (C_NOPE_DIM + C_ROPE_DIM) ** -0.5
    wide = pl.BlockSpec((seq, heads * 2 * LANES), lambda b, g: (b, g))
    return pl.pallas_call(
        functools.partial(_mla_kernel, tq=tq, heads=heads, scale=scale),
        grid=(batch, C_HEADS // heads),
        in_specs=[wide, wide, pl.BlockSpec((seq, LANES), lambda b, g: (b, OFF_KPE // LANES))],
        out_specs=pl.BlockSpec((seq, heads * C_V_DIM), lambda b, g: (b, g)),
        out_shape=jax.ShapeDtypeStruct((t, C_WIDTH), BF16),
        compiler_params=_params("parallel", "parallel"),
        name="mla_attention",
    )(q, kv, z1)


def _moba_kernel(q_ref, k_ref, v_ref, o_ref, kmean_ref, *, tq, heads, scale):
    bs, d = MOBA_BLOCK, A_HEAD_DIM
    n_blk = q_ref.shape[0] // bs
    per_tile = tq // bs
    for h in range(heads):
        for n in range(n_blk):
            kmean_ref[h, n:n + 1, :] = jnp.mean(k_ref[n * bs:(n + 1) * bs, h * d:(h + 1) * d], axis=0, keepdims=True)
    blk_i = lax.broadcasted_iota(jnp.int32, (tq, n_blk), 1)
    blk = blk_i.astype(F32)
    bit_of_blk = lax.shift_left(jnp.ones_like(blk_i), blk_i).astype(F32)
    row1 = lax.broadcasted_iota(jnp.int32, (tq, 1), 0)
    row_sub = sum(((row1 >= c * bs).astype(F32) for c in range(1, per_tile)), jnp.zeros((tq, 1), F32))
    col_local = lax.broadcasted_iota(jnp.int32, (tq, bs), 1)

    def load_kvs(j):
        rows = pl.ds(pl.multiple_of(j * tq, tq), tq)
        return [(k_ref[rows, h * d:(h + 1) * d].astype(BF16), v_ref[rows, h * d:(h + 1) * d].astype(BF16))
                for h in range(heads)]

    def q_tile(qi, _):
        qi = jnp.asarray(qi, jnp.int32)
        rows = pl.ds(pl.multiple_of(qi * tq, tq), tq)
        first = qi * per_tile
        own = first.astype(F32) + row_sub
        qs, sels = [], []
        for h in range(heads):
            qf = q_ref[rows, h * d:(h + 1) * d]
            gate = lax.dot_general(qf, kmean_ref[h], (((1,), (1,)), ((), ())),
                                   precision=lax.Precision.HIGHEST, preferred_element_type=F32)
            gate = jnp.where(blk < own, gate, NEG_INF)
            sel = jnp.zeros((tq, n_blk), F32)
            for _ in range(MOBA_TOPK):
                best = jnp.max(gate, axis=1, keepdims=True)
                idx = jnp.min(jnp.where(gate == best, blk, float(n_blk)), axis=1, keepdims=True)
                pick = blk == idx
                sel = jnp.where(pick & (blk < own), 1.0, sel)
                gate = jnp.where(pick, PICKED, gate)
            qs.append(qf.astype(BF16))
            sels.append(jnp.sum(sel * bit_of_blk, axis=1, keepdims=True).astype(jnp.int32))

        def chosen(h, b):
            return (lax.shift_right_logical(sels[h], jnp.broadcast_to(b, sels[h].shape)) & 1) == 1

        def by_block(s, mask_block):
            parts = [mask_block(c, s[:, c * bs:(c + 1) * bs]) for c in range(per_tile)]
            return parts[0] if per_tile == 1 else jnp.concatenate(parts, axis=1)

        def diag_mask(h, s):
            def mask_block(c, sc):
                limit = jnp.where(row_sub == c, row1 - c * bs + 1,
                                  jnp.where((row_sub > c) & chosen(h, first + c), bs, 0))
                return jnp.where(col_local < limit, sc, NEG_INF)
            return by_block(s, mask_block)

        def past_mask(h, j, s):
            jb = j * per_tile
            return by_block(s, lambda c, sc: jnp.where(chosen(h, jb + c), sc, NEG_INF))

        outs = _causal_attend(qi, qs, load_kvs, tq, d, scale, diag_mask, past_mask)
        for h in range(heads):
            o_ref[rows, h * d:(h + 1) * d] = outs[h].astype(o_ref.dtype)
        return 0

    lax.fori_loop(0, q_ref.shape[0] // tq, q_tile, 0)


def _moba_attention(z1, *, batch, seq, tq=1024, heads=2):
    t = z1.shape[0]
    assert seq % tq == 0 and tq % MOBA_BLOCK == 0
    n_groups = A_HEADS // heads
    wide = heads * A_HEAD_DIM
    assert OFF_QA == 0 and OFF_KA == A_WIDTH and OFF_VA == 2 * A_WIDTH
    return pl.pallas_call(
        functools.partial(_moba_kernel, tq=tq, heads=heads, scale=A_HEAD_DIM ** -0.5),
        grid=(batch, n_groups),
        in_specs=[pl.BlockSpec((seq, wide), lambda b, g: (b, g)),
                  pl.BlockSpec((seq, wide), lambda b, g: (b, n_groups + g)),
                  pl.BlockSpec((seq, wide), lambda b, g: (b, 2 * n_groups + g))],
        out_specs=pl.BlockSpec((seq, wide), lambda b, g: (b, g)),
        out_shape=jax.ShapeDtypeStruct((t, A_WIDTH), BF16),
        scratch_shapes=[pltpu.VMEM((heads, seq // MOBA_BLOCK, A_HEAD_DIM), F32)],
        compiler_params=_params("parallel", "parallel"),
        name="moba_attention",
    )(z1, z1, z1)


def _swa_kernel(sinks_ref, q_ref, k_ref, v_ref, o_ref, *, scale):
    w = SWA_WINDOW
    n_blk = q_ref.shape[0] // w
    pair = pl.program_id(1)
    kv_odd = ((pair * 2) // (B_HEADS // B_KV_HEADS)) % 2
    lane = lax.broadcasted_iota(jnp.int32, (w, LANES), 1)
    low = lane < B_HEAD_DIM
    keep_orig = jnp.where(low, 0, 1) == kv_odd
    row = lax.broadcasted_iota(jnp.int32, (2 * w, 2 * w), 0)
    col = lax.broadcasted_iota(jnp.int32, (2 * w, 2 * w), 1)
    rel = (row & (w - 1)) + w - col
    band = (rel >= 0) & (rel < w)
    c_exp = scale * LOG2E
    sink_raw = jnp.where(lax.broadcasted_iota(jnp.int32, (2 * w, 1), 0) < w,
                         sinks_ref[2 * pair], sinks_ref[2 * pair + 1]) / scale

    def dup(x):
        return jnp.where(keep_orig, x, pltpu.roll(x, B_HEAD_DIM, axis=1)).astype(BF16)

    def q_block(n, first):
        start = n * w
        cur = pl.ds(start if first else pl.multiple_of(start, w), w)
        prev = cur if first else pl.ds(pl.multiple_of(start - w, w), w)
        q = q_ref[cur, :]
        zero = jnp.zeros_like(q)
        q2 = jnp.concatenate([jnp.where(low, q, zero), jnp.where(low, zero, q)], axis=0).astype(BF16)
        k2 = jnp.concatenate([dup(k_ref[prev, :]), dup(k_ref[cur, :])], axis=0)
        v2 = jnp.concatenate([dup(v_ref[prev, :]), dup(v_ref[cur, :])], axis=0)
        s = lax.dot_general(q2, k2, (((1,), (1,)), ((), ())), preferred_element_type=F32)
        s = jnp.where(band & (col >= w) if first else band, s, NEG_INF)
        m = jnp.maximum(jnp.max(s, axis=1, keepdims=True), sink_raw)
        p = jnp.exp2((s - m) * c_exp)
        denom = jnp.sum(p, axis=1, keepdims=True) + jnp.exp2((sink_raw - m) * c_exp)
        o2 = jnp.dot(p.astype(BF16), v2, preferred_element_type=F32) / denom
        o_ref[cur, :] = jnp.where(low, o2[:w], o2[w:]).astype(o_ref.dtype)

    for u in range(SWA_UNROLL):
        q_block(u, u == 0)

    def group(g, _):
        g = jnp.asarray(g, jnp.int32)
        for u in range(SWA_UNROLL):
            q_block(g * SWA_UNROLL + u, False)
        return 0

    lax.fori_loop(1, n_blk // SWA_UNROLL, group, 0)


def _swa_attention(z1, sinks, *, batch, seq):
    t = z1.shape[0]
    pairs = B_HEADS // 2
    rep = B_HEADS // B_KV_HEADS
    assert (seq // SWA_WINDOW) % SWA_UNROLL == 0
    kv_group = lambda p: (2 * p) // rep // 2
    return pl.pallas_call(
        functools.partial(_swa_kernel, scale=B_HEAD_DIM ** -0.5),
        grid_spec=pltpu.PrefetchScalarGridSpec(
            num_scalar_prefetch=1,
            grid=(batch, pairs),
            in_specs=[pl.BlockSpec((seq, LANES), lambda b, p, sinks: (b, OFF_QB // LANES + p)),
                      pl.BlockSpec((seq, LANES), lambda b, p, sinks: (b, OFF_KB // LANES + kv_group(p))),
                      pl.BlockSpec((seq, LANES), lambda b, p, sinks: (b, OFF_VB // LANES + kv_group(p)))],
            out_specs=pl.BlockSpec((seq, LANES), lambda b, p, sinks: (b, p)),
        ),
        out_shape=jax.ShapeDtypeStruct((t, B_WIDTH), BF16),
        compiler_params=_params("parallel", "parallel"),
        name="swa_attention",
    )(sinks, z1, z1, z1)


def _gated_out_kernel(oa_ref, ob_ref, oc_ref, ga_ref, gb_ref, gc_ref, wa_ref, wb_ref, wc_ref, y_ref,
                      wa_bf, wb_bf, wc_bf):
    @pl.when(pl.program_id(1) == 0)
    def _():
        wa_bf[...] = wa_ref[...].astype(BF16)
        wb_bf[...] = wb_ref[...].astype(BF16)
        wc_bf[...] = wc_ref[...].astype(BF16)

    y = jax.nn.sigmoid(ga_ref[...]) * jnp.dot(oa_ref[...], wa_bf[...], preferred_element_type=F32)
    y += jax.nn.sigmoid(gb_ref[...]) * jnp.dot(ob_ref[...], wb_bf[...], preferred_element_type=F32)
    y += jax.nn.sigmoid(gc_ref[...]) * jnp.dot(oc_ref[...], wc_bf[...], preferred_element_type=F32)
    y_ref[...] = y.astype(y_ref.dtype)


def _gated_out(oa, ob, oc, gates, w_out_a, w_out_b, w_out_c, layer, *, tm=512, tn=512):
    t = oa.shape[0]
    d = w_out_a.shape[2]
    nb = d // tn
    o_spec = lambda width: pl.BlockSpec((tm, width), lambda j, i: (i, 0))
    g_spec = lambda which: pl.BlockSpec((tm, tn), lambda j, i: (i, which * nb + j))
    w_spec = lambda width: pl.BlockSpec((None, width, tn), lambda j, i: (layer, 0, j))
    return pl.pallas_call(
        _gated_out_kernel,
        grid=(nb, t // tm),
        in_specs=[o_spec(A_WIDTH), o_spec(B_WIDTH), o_spec(C_WIDTH), g_spec(0), g_spec(1), g_spec(2),
                  w_spec(A_WIDTH), w_spec(B_WIDTH), w_spec(C_WIDTH)],
        out_specs=pl.BlockSpec((tm, tn), lambda j, i: (i, j)),
        out_shape=jax.ShapeDtypeStruct((t, d), BF16),
        scratch_shapes=[pltpu.VMEM((A_WIDTH, tn), BF16), pltpu.VMEM((B_WIDTH, tn), BF16),
                        pltpu.VMEM((C_WIDTH, tn), BF16)],
        compiler_params=_params("arbitrary", "arbitrary"),
        name="gated_out",
    )(oa, ob, oc, gates, gates, gates, w_out_a, w_out_b, w_out_c)


HI16 = -65536


def _bf16_bits_hi(v):
    return lax.bitcast_convert_type(v.astype(BF16).astype(F32), jnp.int32)


def _router_kernel(h_ref, g_ref, w_ref, b_ref, ids_ref, wts_ref, packed_ref):
    x = h_ref[...]
    hn = x * lax.rsqrt(jnp.mean(x * x, axis=-1, keepdims=True) + NORM_EPS) * g_ref[...]
    half = hn.shape[1] // 2
    packed_ref[...] = lax.shift_right_logical(_bf16_bits_hi(hn[:, :half]), 16) | _bf16_bits_hi(hn[:, half:])
    logits = jnp.dot(hn, w_ref[...], precision=lax.Precision.HIGHEST, preferred_element_type=F32) + b_ref[...]
    lane = lax.broadcasted_iota(jnp.int32, logits.shape, 1)
    far = float(LANES)
    is_g = lane < N_GROUPS
    g_id = lane.astype(F32)
    gmax = jnp.max(jnp.where(is_g, logits, NEG_INF), axis=1, keepdims=True)
    gsel = jnp.min(jnp.where(is_g & (logits == gmax), g_id, far), axis=1, keepdims=True)
    p_g = 1.0 / jnp.sum(jnp.where(is_g, jnp.exp(logits - gmax), 0.0), axis=1, keepdims=True)
    e_lane = lane - N_GROUPS
    e_id = e_lane.astype(F32)
    e_group = jnp.right_shift(e_lane, 2).astype(F32)
    in_grp = (e_lane >= 0) & (e_lane < N_EXPERTS) & (e_group == gsel)
    emax = jnp.max(jnp.where(in_grp, logits, NEG_INF), axis=1, keepdims=True)
    ee = jnp.where(in_grp, jnp.exp(jnp.where(in_grp, logits, emax) - emax), 0.0)
    ep = ee / jnp.sum(ee, axis=1, keepdims=True)
    p1 = jnp.max(jnp.where(in_grp, ep, -1.0), axis=1, keepdims=True)
    i1 = jnp.min(jnp.where(in_grp & (ep == p1), e_id, far), axis=1, keepdims=True)
    rest = in_grp & (e_id != i1)
    p2 = jnp.max(jnp.where(rest, ep, -1.0), axis=1, keepdims=True)
    i2 = jnp.min(jnp.where(rest & (ep == p2), e_id, far), axis=1, keepdims=True)
    tot = p1 + p2
    ids_ref[...] = jnp.where(lane == 0, i1, jnp.where(lane == 1, i2, 0.0)).astype(jnp.int32)
    wts_ref[...] = jnp.where(lane == 0, p_g * p1 / tot, jnp.where(lane == 1, p_g * p2 / tot, 0.0))


def _router(h, g, w_r, b_r, layer, *, tm=256):
    t, d = h.shape
    out = pl.BlockSpec((tm, LANES), lambda i: (i, 0))
    return pl.pallas_call(
        _router_kernel,
        grid=(t // tm,),
        in_specs=[pl.BlockSpec((tm, d), lambda i: (i, 0)),
                  pl.BlockSpec((None, 1, d), lambda i: (layer, 0, 0)),
                  pl.BlockSpec((None, d, LANES), lambda i: (layer, 0, 0)),
                  pl.BlockSpec((None, 1, LANES), lambda i: (layer, 0, 0))],
        out_specs=[out, out, pl.BlockSpec((tm, d // 2), lambda i: (i, 0))],
        out_shape=[jax.ShapeDtypeStruct((t, LANES), jnp.int32), jax.ShapeDtypeStruct((t, LANES), F32),
                   jax.ShapeDtypeStruct((t, d // 2), jnp.int32)],
        compiler_params=_params("parallel"),
        name="moe_router",
    )(h, g, w_r, b_r)


def _row_copy(src_hbm, tok, dst_ref, r, sem):
    return pltpu.make_async_copy(src_hbm.at[pl.ds(tok, 1), :], dst_ref.at[pl.ds(r, 1), :], sem)


GATHER_UNROLL = 8
NORM_CHUNK = 16


def _dispatch_kernel(tok_ref, nvalid_ref, h_hbm, o_ref, rows_ref, sem):
    i = pl.program_id(0)
    n_valid = nvalid_ref[0]
    n_rows = rows_ref.shape[1]
    slot = lax.rem(i, 2)

    def fetch(blk, slot):
        base = blk * n_rows

        def issue(g, _):
            for u in range(GATHER_UNROLL):
                r = g * GATHER_UNROLL + u
                _row_copy(h_hbm, tok_ref[base + r], rows_ref.at[slot], r, sem.at[slot]).start()
            return 0

        lax.fori_loop(0, n_rows // GATHER_UNROLL, issue, 0)

    @pl.when(i == 0)
    def _():
        fetch(0, 0)

    @pl.when(i + 1 < n_valid)
    def _():
        fetch(i + 1, 1 - slot)

    @pl.when(i < n_valid)
    def _():
        def drain(r, _):
            _row_copy(h_hbm, 0, rows_ref.at[slot], r, sem.at[slot]).wait()
            return 0

        lax.fori_loop(0, n_rows, drain, 0, unroll=GATHER_UNROLL)

        half = rows_ref.shape[2]

        def unpack(c, _):
            rows = pl.ds(pl.multiple_of(jnp.asarray(c, jnp.int32) * NORM_CHUNK, NORM_CHUNK), NORM_CHUNK)
            word = rows_ref[slot, rows, :]
            o_ref[rows, :half] = lax.bitcast_convert_type(lax.shift_left(word, 16), F32).astype(o_ref.dtype)
            o_ref[rows, half:] = lax.bitcast_convert_type(word & HI16, F32).astype(o_ref.dtype)
            return 0

        lax.fori_loop(0, n_rows // NORM_CHUNK, unpack, 0, unroll=4)


def _dispatch(packed, buf_tok, nvalid, *, n_blocks):
    t, half = packed.shape
    blk = lambda i, tok, nv: (jnp.minimum(i, nv[0] - 1), 0)
    return pl.pallas_call(
        _dispatch_kernel,
        grid_spec=pltpu.PrefetchScalarGridSpec(
            num_scalar_prefetch=2,
            grid=(n_blocks,),
            in_specs=[pl.BlockSpec(memory_space=pl.ANY)],
            out_specs=pl.BlockSpec((MOE_ROWS, 2 * half), blk),
            scratch_shapes=[pltpu.VMEM((2, MOE_ROWS, half), jnp.int32), pltpu.SemaphoreType.DMA((2,))],
        ),
        out_shape=jax.ShapeDtypeStruct((n_blocks * MOE_ROWS, 2 * half), BF16),
        compiler_params=_params("arbitrary"),
        name="moe_dispatch",
    )(buf_tok, nvalid, packed)


def _run_weights(be_ref, first_ref, next_ref, i, pass_id, n_passes, n_blocks, copies, stage):
    @pl.when((pass_id == 0) & (i == 0))
    def _():
        for c in copies(be_ref[0], 0):
            c.start()

    @pl.when(first_ref[i] == 1)
    def _():
        for c in copies(be_ref[i], pass_id):
            c.wait()
        stage()
        nxt = next_ref[i]
        same_pass = nxt < n_blocks
        e_next = jnp.where(same_pass, be_ref[jnp.minimum(nxt, n_blocks - 1)], be_ref[0])
        pass_next = jnp.where(same_pass, pass_id, pass_id + 1)

        @pl.when(same_pass | (pass_id + 1 < n_passes))
        def _():
            for c in copies(e_next, pass_next):
                c.start()


def _expert_up_kernel(be_ref, nvalid_ref, first_ref, next_ref, x_ref, wg_hbm, wu_hbm, o_ref,
                      wg_land, wu_land, wg_bf, wu_bf, sem, *, layer):
    f, i = pl.program_id(0), pl.program_id(1)
    tf = wg_land.shape[1]

    def copies(e, ff_tile):
        cols = pl.ds(pl.multiple_of(ff_tile * tf, tf), tf)
        return (pltpu.make_async_copy(wg_hbm.at[layer, e, :, cols], wg_land, sem.at[0]),
                pltpu.make_async_copy(wu_hbm.at[layer, e, :, cols], wu_land, sem.at[1]))

    def stage():
        wg_bf[...] = wg_land[...].astype(BF16)
        wu_bf[...] = wu_land[...].astype(BF16)

    _run_weights(be_ref, first_ref, next_ref, i, f, pl.num_programs(0), pl.num_programs(1), copies, stage)

    @pl.when(i < nvalid_ref[0])
    def _():
        x = x_ref[...]
        hg = jnp.dot(x, wg_bf[...], preferred_element_type=F32)
        hu = jnp.dot(x, wu_bf[...], preferred_element_type=F32)
        o_ref[...] = (jax.nn.silu(hg) * hu).astype(o_ref.dtype)


def _expert_up(xb, w_gate, w_up, layer, runs, *, n_blocks):
    d = xb.shape[1]
    ff = w_gate.shape[3]
    tf = MOE_FF_TILE
    row_blk = lambda f, i, be, nv, first, nxt: (jnp.minimum(i, nv[0] - 1), 0)
    return pl.pallas_call(
        functools.partial(_expert_up_kernel, layer=layer),
        grid_spec=pltpu.PrefetchScalarGridSpec(
            num_scalar_prefetch=4,
            grid=(ff // tf, n_blocks),
            in_specs=[pl.BlockSpec((MOE_ROWS, d), row_blk), pl.BlockSpec(memory_space=pl.ANY),
                      pl.BlockSpec(memory_space=pl.ANY)],
            out_specs=pl.BlockSpec((MOE_ROWS, tf),
                                   lambda f, i, be, nv, first, nxt: (jnp.minimum(i, nv[0] - 1), f)),
            scratch_shapes=[pltpu.VMEM((d, tf), F32), pltpu.VMEM((d, tf), F32),
                            pltpu.VMEM((d, tf), BF16), pltpu.VMEM((d, tf), BF16),
                            pltpu.SemaphoreType.DMA((2,))],
        ),
        out_shape=jax.ShapeDtypeStruct((n_blocks * MOE_ROWS, ff), BF16),
        compiler_params=_params("arbitrary", "arbitrary"),
        name="moe_expert_up",
    )(*runs, xb, w_gate, w_up)


def _expert_down_kernel(be_ref, nvalid_ref, first_ref, next_ref, a_ref, wd_hbm, o_ref, wd_land, wd_bf, sem,
                        *, layer):
    i = pl.program_id(0)

    def copies(e, _):
        return (pltpu.make_async_copy(wd_hbm.at[layer, e], wd_land, sem.at[0]),)

    def stage():
        wd_bf[...] = wd_land[...].astype(BF16)

    _run_weights(be_ref, first_ref, next_ref, i, 0, 1, pl.num_programs(0), copies, stage)

    @pl.when(i < nvalid_ref[0])
    def _():
        o_ref[...] = jnp.dot(a_ref[...], wd_bf[...], preferred_element_type=F32)


def _expert_down(act, w_down, layer, runs, *, n_blocks):
    ff, d = w_down.shape[2], w_down.shape[3]
    row_blk = lambda i, be, nv, first, nxt: (jnp.minimum(i, nv[0] - 1), 0)
    return pl.pallas_call(
        functools.partial(_expert_down_kernel, layer=layer),
        grid_spec=pltpu.PrefetchScalarGridSpec(
            num_scalar_prefetch=4,
            grid=(n_blocks,),
            in_specs=[pl.BlockSpec((MOE_ROWS, ff), row_blk), pl.BlockSpec(memory_space=pl.ANY)],
            out_specs=pl.BlockSpec((MOE_ROWS, d), row_blk),
            scratch_shapes=[pltpu.VMEM((ff, d), F32), pltpu.VMEM((ff, d), BF16), pltpu.SemaphoreType.DMA((1,))],
        ),
        out_shape=jax.ShapeDtypeStruct((n_blocks * MOE_ROWS, d), F32),
        compiler_params=_params("arbitrary"),
        name="moe_expert_down",
    )(*runs, act, w_down)


def _combine_kernel(pos_ref, h_ref, wts_ref, g_ref, y_hbm, *refs, emit_h):
    o_ref = refs[0] if emit_h else None
    n_ref, y0_ref, y1_ref, sem = refs[int(emit_h):]
    i = pl.program_id(0)
    tm = h_ref.shape[0]
    slot = lax.rem(i, 2)

    def fetch(blk, slot):
        base = blk * tm * EXPERT_TOPK

        def issue(g, _):
            for u in range(GATHER_UNROLL):
                r = g * GATHER_UNROLL + u
                _row_copy(y_hbm, pos_ref[base + EXPERT_TOPK * r], y0_ref.at[slot], r, sem.at[slot]).start(priority=0)
                _row_copy(y_hbm, pos_ref[base + EXPERT_TOPK * r + 1], y1_ref.at[slot], r,
                          sem.at[slot]).start(priority=1)
            return 0

        lax.fori_loop(0, tm // GATHER_UNROLL, issue, 0)

    @pl.when(i == 0)
    def _():
        fetch(0, 0)

    @pl.when(i + 1 < pl.num_programs(0))
    def _():
        fetch(i + 1, 1 - slot)

    def drain(r, _):
        _row_copy(y_hbm, 0, y0_ref.at[slot], r, sem.at[slot]).wait()
        _row_copy(y_hbm, 0, y1_ref.at[slot], r, sem.at[slot]).wait()
        return 0

    lax.fori_loop(0, tm, drain, 0, unroll=GATHER_UNROLL)

    def rows_chunk(c, _):
        rows = pl.ds(pl.multiple_of(jnp.asarray(c, jnp.int32) * NORM_CHUNK, NORM_CHUNK), NORM_CHUNK)
        w = wts_ref[rows, :]
        hv = h_ref[rows, :] + w[:, 0:1] * y0_ref[slot, rows, :] + w[:, 1:2] * y1_ref[slot, rows, :]
        if emit_h:
            o_ref[rows, :] = hv
        hn = hv * lax.rsqrt(jnp.mean(hv * hv, axis=-1, keepdims=True) + NORM_EPS) * g_ref[...]
        n_ref[rows, :] = hn.astype(n_ref.dtype)
        return 0

    lax.fori_loop(0, tm // NORM_CHUNK, rows_chunk, 0, unroll=2)


def _combine(h, wts, y, pos, g, *, norm_dtype, emit_h, tm=128):
    t, d = h.shape
    row = pl.BlockSpec((tm, d), lambda i, pos: (i, 0))
    outs = pl.pallas_call(
        functools.partial(_combine_kernel, emit_h=emit_h),
        grid_spec=pltpu.PrefetchScalarGridSpec(
            num_scalar_prefetch=1,
            grid=(t // tm,),
            in_specs=[row, pl.BlockSpec((tm, LANES), lambda i, pos: (i, 0)),
                      pl.BlockSpec((1, d), lambda i, pos: (0, 0)), pl.BlockSpec(memory_space=pl.ANY)],
            out_specs=[row] * (1 + emit_h),
            scratch_shapes=[pltpu.VMEM((2, tm, d), F32), pltpu.VMEM((2, tm, d), F32),
                            pltpu.SemaphoreType.DMA((2,))],
        ),
        out_shape=([jax.ShapeDtypeStruct((t, d), F32)] if emit_h else [])
        + [jax.ShapeDtypeStruct((t, d), norm_dtype)],
        compiler_params=_params("arbitrary"),
        name="moe_combine",
    )(pos, h, wts, g.reshape(1, d), y)
    return (outs[0], outs[1]) if emit_h else (None, outs[0])


def _hier_moe(h, ffn_norm_g, w_r, b_r, w_gate, w_up, w_down, layer, next_norm_g, *, norm_dtype, emit_h):
    t, d = h.shape
    tk = t * EXPERT_TOPK
    n_blocks = tk // MOE_ROWS + N_EXPERTS
    ids, wts, hn_packed = _router(h, ffn_norm_g, w_r, b_r, layer)
    flat_e = ids[:, :EXPERT_TOPK].reshape(tk)
    onehot = (flat_e[:, None] == jnp.arange(N_EXPERTS, dtype=jnp.int32)[None, :]).astype(jnp.int32)
    csum = jnp.cumsum(onehot, axis=0)
    rank = jnp.sum(onehot * csum, axis=1) - 1
    counts = csum[-1]
    padded = (counts + MOE_ROWS - 1) // MOE_ROWS * MOE_ROWS
    pad_end = jnp.cumsum(padded)
    pad_start = pad_end - padded
    dest = (pad_start[flat_e] + rank).astype(jnp.int32)
    flat_tok = jnp.arange(tk, dtype=jnp.int32) // EXPERT_TOPK
    buf_tok = jnp.zeros((n_blocks * MOE_ROWS,), jnp.int32).at[dest].set(flat_tok)
    nvalid = (pad_end[-1] // MOE_ROWS).astype(jnp.int32).reshape(1)
    blk_start = jnp.minimum(jnp.arange(n_blocks, dtype=jnp.int32), nvalid[0] - 1) * MOE_ROWS
    block_expert = jnp.minimum(jnp.searchsorted(pad_end, blk_start, side='right'), N_EXPERTS - 1).astype(jnp.int32)

    blk = jnp.arange(n_blocks, dtype=jnp.int32)
    prev_expert = jnp.concatenate([jnp.full((1,), -1, jnp.int32), block_expert[:-1]])
    run_first = (blk < nvalid[0]) & (block_expert != prev_expert)
    first_at_or_after = lax.cummin(jnp.where(run_first, blk, n_blocks), reverse=True)
    run_next = jnp.concatenate([first_at_or_after[1:], jnp.full((1,), n_blocks, jnp.int32)])
    runs = (block_expert, nvalid, run_first.astype(jnp.int32), run_next)

    xb = _dispatch(hn_packed, buf_tok, nvalid, n_blocks=n_blocks)
    act = _expert_up(xb, w_gate, w_up, layer, runs, n_blocks=n_blocks)
    y = _expert_down(act, w_down, layer, runs, n_blocks=n_blocks)
    return _combine(h, wts, y, dest, next_norm_g, norm_dtype=norm_dtype, emit_h=emit_h)


ROPE_A, ROPE_B, ROPE_PE = 0, 1, 2


def _in_proj_rope_table(col):
    if OFF_QA <= col < OFF_VA:
        return ROPE_A
    if OFF_QB <= col < OFF_VB:
        return ROPE_B
    if col == OFF_KPE:
        return ROPE_PE
    return None


def _mla_q_rope_table(col):
    return 0 if (col // LANES) % 2 == 1 else None


def kernel(x, positions, attn_norm_g, w_in, q_norm_g, kv_norm_g, wq_b, wkv_b, sinks, w_out_a, w_out_b,
           w_out_c, w_o, ffn_norm_g, w_group, b_group, w_expert, b_expert, w_gate, w_up, w_down,
           final_norm_g):
    batch, seq, d = x.shape
    depth = w_in.shape[0]
    t = batch * seq
    h = x.reshape(t, d)
    pos = positions.reshape(t).astype(F32)

    w_in_t = jnp.swapaxes(w_in, 1, 2)
    q_head = C_NOPE_DIM + C_ROPE_DIM
    wq_pad = jnp.pad(wq_b.reshape(depth, C_Q_RANK, C_HEADS, q_head),
                     ((0, 0), (0, 0), (0, 0), (0, 2 * LANES - q_head))
                     ).reshape(depth, C_Q_RANK, C_HEADS * 2 * LANES).astype(BF16)
    w_r = jnp.concatenate([w_group, w_expert,
                           jnp.zeros((depth, d, LANES - N_GROUPS - N_EXPERTS), F32)], axis=2)
    b_r = jnp.concatenate([b_group, b_expert,
                           jnp.zeros((depth, LANES - N_GROUPS - N_EXPERTS), F32)], axis=1).reshape(depth, 1, LANES)
    attn_g = attn_norm_g.reshape(depth, 1, d)
    ffn_g = ffn_norm_g.reshape(depth, 1, d)

    c_a, s_a = _rope_tables(pos, A_HEAD_DIM)
    c_b, s_b = _rope_tables(pos, B_HEAD_DIM)
    c_pe, s_pe = _rope_tables(pos, C_ROPE_DIM, keep_lanes=C_ROPE_DIM)
    rope_tables = ((c_a, s_a, A_HEAD_DIM // 2), (c_b, s_b, B_HEAD_DIM // 2), (c_pe, s_pe, C_ROPE_DIM // 2))

    hn = _rmsnorm(h, attn_norm_g[0])
    for l in range(depth):
        z1 = _matmul(hn, w_in_t, l, n_out=Z1_WIDTH, w_is_nk=True, rope_tables=rope_tables,
                     rope_group_table=_in_proj_rope_table, name="in_proj")
        gates = _matmul(hn, w_in_t, l, n_out=3 * d, col_off=OFF_GATES, w_is_nk=True, name="gate_proj")
        o_a = _moba_attention(z1, batch=batch, seq=seq)
        o_b = _swa_attention(z1, sinks[l], batch=batch, seq=seq)
        cq_n = _rmsnorm(z1, q_norm_g[l], col_off=OFF_CQ, width=C_Q_RANK, piece=512)
        ckv_n = _rmsnorm(z1, kv_norm_g[l], col_off=OFF_CKV, width=C_KV_RANK, piece=512)
        q_c = _matmul(cq_n, wq_pad, l, n_out=C_HEADS * 2 * LANES, tn=1024, out_dtype=BF16,
                      rope_tables=rope_tables[ROPE_PE:], rope_group_table=_mla_q_rope_table, name="mla_q_proj")
        kv_c = _matmul(ckv_n, wkv_b, l, n_out=C_HEADS * (C_NOPE_DIM + C_V_DIM), tn=1024, out_dtype=BF16,
                       name="mla_kv_proj")
        o_c = _mla_attention(q_c, kv_c, z1, batch=batch, seq=seq)
        y = _gated_out(o_a, o_b, o_c, gates, w_out_a, w_out_b, w_out_c, l)
        h = _matmul(y, w_o, l, n_out=d, residual=h, name="out_proj")
        last = l == depth - 1
        h, hn = _hier_moe(h, ffn_g, w_r, b_r, w_gate, w_up, w_down, l,
                          final_norm_g if last else attn_norm_g[l + 1],
                          norm_dtype=F32 if last else BF16, emit_h=not last)
    return hn.reshape(batch, seq, d)
```

```python
import functools

import jax
import jax.numpy as jnp
from jax import lax
from jax.experimental import pallas as pl
from jax.experimental.pallas import tpu as pltpu

F32 = jnp.float32
BF16 = jnp.bfloat16

ROPE_THETA = 10000.0
NORM_EPS = 1e-6
NEG_INF = -1e30
PICKED = -3e38
LOG2E = 1.4426950408889634

A_HEADS, A_HEAD_DIM = 16, 128
MOBA_BLOCK, MOBA_TOPK = 256, 3
B_HEADS, B_KV_HEADS, B_HEAD_DIM, SWA_WINDOW = 32, 4, 64, 128
C_HEADS, C_Q_RANK, C_KV_RANK, C_NOPE_DIM, C_ROPE_DIM, C_V_DIM = 16, 1024, 512, 128, 64, 128
N_GROUPS, EXPERTS_PER_GROUP, EXPERT_TOPK, D_FF_EXPERT = 8, 4, 2, 768
N_EXPERTS = N_GROUPS * EXPERTS_PER_GROUP
assert EXPERTS_PER_GROUP == 4

LANES = 128
VMEM_LIMIT_BYTES = 58 * 1024 * 1024

A_WIDTH = A_HEADS * A_HEAD_DIM
B_WIDTH = B_HEADS * B_HEAD_DIM
B_KV_WIDTH = B_KV_HEADS * B_HEAD_DIM
C_WIDTH = C_HEADS * C_V_DIM
OFF_QA = 0
OFF_KA = OFF_QA + A_WIDTH
OFF_VA = OFF_KA + A_WIDTH
OFF_QB = OFF_VA + A_WIDTH
OFF_KB = OFF_QB + B_WIDTH
OFF_VB = OFF_KB + B_KV_WIDTH
OFF_CQ = OFF_VB + B_KV_WIDTH
OFF_CKV = OFF_CQ + C_Q_RANK
OFF_KPE = OFF_CKV + C_KV_RANK
OFF_GATES = OFF_KPE + C_ROPE_DIM
Z1_WIDTH = 10752

MOE_ROWS = 256
MOE_FF_TILE = 256
SWA_UNROLL = 8


def _params(*sem):
    return pltpu.CompilerParams(dimension_semantics=sem, vmem_limit_bytes=VMEM_LIMIT_BYTES)


def _rmsnorm_kernel(*refs, n_pieces, width):
    x_refs, g_ref, o_ref = refs[:n_pieces], refs[n_pieces], refs[n_pieces + 1]
    xs = [r[...].astype(F32) for r in x_refs]
    ss = sum(jnp.sum(x * x, axis=-1, keepdims=True) for x in xs)
    inv = lax.rsqrt(ss * (1.0 / width) + NORM_EPS)
    pw = xs[0].shape[1]
    for p, x in enumerate(xs):
        o_ref[:, p * pw:(p + 1) * pw] = (x * inv * g_ref[:, p * pw:(p + 1) * pw]).astype(o_ref.dtype)


def _rmsnorm(x, g, *, col_off=0, width=None, piece=None, tm=256, out_dtype=BF16):
    t = x.shape[0]
    width = width or x.shape[1]
    piece = piece or width
    n_pieces = width // piece
    off = col_off // piece
    assert col_off % piece == 0 and width % piece == 0
    in_specs = [pl.BlockSpec((tm, piece), functools.partial(lambda i, p: (i, off + p), p=p))
                for p in range(n_pieces)]
    in_specs.append(pl.BlockSpec((1, width), lambda i: (0, 0)))
    return pl.pallas_call(
        functools.partial(_rmsnorm_kernel, n_pieces=n_pieces, width=width),
        grid=(t // tm,),
        in_specs=in_specs,
        out_specs=pl.BlockSpec((tm, width), lambda i: (i, 0)),
        out_shape=jax.ShapeDtypeStruct((t, width), out_dtype),
        compiler_params=_params("parallel"),
        name="rmsnorm",
    )(*([x] * n_pieces), g.reshape(1, width))


def _stage_weight(w_ref, wbf_ref, w_is_nk):
    if not w_is_nk:
        wbf_ref[...] = w_ref[...].astype(BF16)
        return
    _, tn, k = w_ref.shape
    for c in range(k // tn):
        wbf_ref[c * tn:(c + 1) * tn, :] = w_ref[0, :, c * tn:(c + 1) * tn].T.astype(BF16)


def _swap_halves(x, half):
    if half == 64:
        return pltpu.roll(x, 64, axis=1)
    lane = lax.broadcasted_iota(jnp.int32, x.shape, 1)
    return jnp.where((lane & half) == 0, pltpu.roll(x, LANES - half, axis=1), pltpu.roll(x, half, axis=1))


def _matmul_kernel(*refs, stage_w, w_is_nk, has_res, rope_modes, rope_halves):
    x_ref, w_ref = refs[0], refs[1]
    n_in = 2 + has_res + 2 * len(rope_halves)
    r_ref = refs[2] if has_res else None
    table_refs = refs[2 + has_res:n_in]
    o_ref = refs[n_in]
    if stage_w:
        wbf_ref = refs[n_in + 1]

        @pl.when(pl.program_id(1) == 0)
        def _():
            _stage_weight(w_ref, wbf_ref, w_is_nk)

        w_src = wbf_ref
    else:
        w_src = w_ref

    def product():
        acc = jnp.dot(x_ref[...], w_src[...], preferred_element_type=F32)
        return acc + r_ref[...] if has_res else acc

    if rope_modes is None:
        o_ref[...] = product().astype(o_ref.dtype)
        return
    j = pl.program_id(0)
    for pattern in sorted(set(rope_modes), key=str):
        blocks = [b for b, p in enumerate(rope_modes) if p == pattern]

        @pl.when(functools.reduce(jnp.logical_or, [j == b for b in blocks]))
        def _(pattern=pattern):
            acc = product()
            for g, table in enumerate(pattern):
                xg = acc[:, g * LANES:(g + 1) * LANES]
                if table is not None:
                    c_ref, s_ref = table_refs[2 * table], table_refs[2 * table + 1]
                    xg = xg * c_ref[...] + _swap_halves(xg, rope_halves[table]) * s_ref[...]
                o_ref[:, g * LANES:(g + 1) * LANES] = xg.astype(o_ref.dtype)


def _matmul(x, w, layer, *, n_out, col_off=0, w_is_nk=False, tm=1024, tn=512, out_dtype=F32, residual=None,
            rope_tables=(), rope_group_table=None, name="matmul"):
    m, k = x.shape
    assert w.shape[2 if w_is_nk else 1] == k and m % tm == 0 and n_out % tn == 0
    stage_w = w_is_nk or w.dtype != BF16
    rope_modes = None
    if rope_group_table is not None:
        rope_modes = tuple(tuple(rope_group_table(b * tn + g * LANES) for g in range(tn // LANES))
                           for b in range(n_out // tn))
    if w_is_nk:
        assert k % tn == 0 and col_off % 8 == 0
        w_spec = pl.BlockSpec((pl.Element(1), pl.Element(tn), pl.Element(k)),
                              lambda j, i: (layer, pl.multiple_of(col_off + j * tn, 8), 0))
    else:
        assert col_off % tn == 0
        w_spec = pl.BlockSpec((None, k, tn), lambda j, i: (layer, 0, j + col_off // tn))
    in_specs = [pl.BlockSpec((tm, k), lambda j, i: (i, 0)), w_spec]
    args = [x, w]
    if residual is not None:
        in_specs.append(pl.BlockSpec((tm, tn), lambda j, i: (i, j)))
        args.append(residual)
    for c, s, _ in rope_tables:
        in_specs += [pl.BlockSpec((tm, LANES), lambda j, i: (i, 0))] * 2
        args += [c, s]
    return pl.pallas_call(
        functools.partial(_matmul_kernel, stage_w=stage_w, w_is_nk=w_is_nk, has_res=residual is not None,
                          rope_modes=rope_modes, rope_halves=tuple(half for _, _, half in rope_tables)),
        grid=(n_out // tn, m // tm),
        in_specs=in_specs,
        out_specs=pl.BlockSpec((tm, tn), lambda j, i: (i, j)),
        out_shape=jax.ShapeDtypeStruct((m, n_out), out_dtype),
        scratch_shapes=[pltpu.VMEM((k, tn), BF16)] if stage_w else [],
        compiler_params=_params("arbitrary", "arbitrary"),
        name=name,
    )(*args)


def _rope_tables_kernel(pos_ref, invf_ref, sign_ref, keep_ref, c_ref, s_ref):
    ang = pos_ref[...] * invf_ref[...]
    c_ref[...] = jnp.cos(ang) * keep_ref[...]
    s_ref[...] = jnp.sin(ang) * sign_ref[...]


def _rope_tables(pos, dim, *, keep_lanes=LANES):
    t = pos.shape[0]
    half = dim // 2
    lane = jnp.arange(LANES)
    invf = (ROPE_THETA ** (-(2.0 * (lane % half)).astype(F32) / dim)).reshape(1, LANES)
    keep = (lane < keep_lanes).astype(F32).reshape(1, LANES)
    sign = jnp.where((lane % dim) < half, -1.0, 1.0).astype(F32).reshape(1, LANES) * keep
    tm = 1024
    row = pl.BlockSpec((1, LANES), lambda i: (0, 0))
    out = pl.BlockSpec((tm, LANES), lambda i: (i, 0))
    return pl.pallas_call(
        _rope_tables_kernel,
        grid=(t // tm,),
        in_specs=[pl.BlockSpec((tm, 1), lambda i: (i, 0)), row, row, row],
        out_specs=[out, out],
        out_shape=[jax.ShapeDtypeStruct((t, LANES), F32)] * 2,
        compiler_params=_params("parallel"),
        name="rope_tables",
    )(pos.reshape(t, 1), invf, sign, keep)


def _online_block(q, k, v, s_mask, carry, c_exp):
    m, l, acc = carry
    s = s_mask(lax.dot_general(q, k, (((1,), (1,)), ((), ())), preferred_element_type=F32))
    m_new = jnp.maximum(m, jnp.max(s, axis=1, keepdims=True))
    alpha = jnp.exp2((m - m_new) * c_exp)
    p = jnp.exp2((s - m_new) * c_exp)
    l = alpha * l + jnp.sum(p, axis=1, keepdims=True)
    acc = alpha * acc + jnp.dot(p.astype(BF16), v, preferred_element_type=F32)
    return m_new, l, acc


def _causal_attend(qi, qs, load_kvs, tq, dv, scale, diag_mask, past_mask):
    heads = range(len(qs))
    c_exp = scale * LOG2E
    init = (jnp.full((tq, 1), NEG_INF, F32), jnp.zeros((tq, 1), F32), jnp.zeros((tq, dv), F32))
    kvs = load_kvs(qi)
    carries = tuple(_online_block(qs[h], *kvs[h], functools.partial(diag_mask, h), init, c_exp) for h in heads)

    def past(j, carries):
        j = jnp.asarray(j, jnp.int32)
        kvs = load_kvs(j)
        return tuple(_online_block(qs[h], *kvs[h], functools.partial(past_mask, h, j), carries[h], c_exp)
                     for h in heads)

    carries = lax.fori_loop(0, qi, past, carries)
    return [acc / l for _, l, acc in carries]


def _mla_kernel(q_ref, kv_ref, kpe_ref, o_ref, *, tq, heads, scale):
    n_q = q_ref.shape[0] // tq
    qw, dv = 2 * LANES, C_V_DIM
    row = lax.broadcasted_iota(jnp.int32, (tq, tq), 0)
    col = lax.broadcasted_iota(jnp.int32, (tq, tq), 1)
    causal = col <= row

    def load_kvs(j):
        rows = pl.ds(pl.multiple_of(j * tq, tq), tq)
        kpe = kpe_ref[rows, :].astype(BF16)
        return [(jnp.concatenate([kv_ref[rows, h * qw:h * qw + C_NOPE_DIM], kpe], axis=1),
                 kv_ref[rows, h * qw + C_NOPE_DIM:(h + 1) * qw]) for h in range(heads)]

    def q_tile(qi, _):
        qi = jnp.asarray(qi, jnp.int32)
        rows = pl.ds(pl.multiple_of(qi * tq, tq), tq)
        qs = [q_ref[rows, h * qw:(h + 1) * qw] for h in range(heads)]
        outs = _causal_attend(qi, qs, load_kvs, tq, dv, scale,
                              lambda h, s: jnp.where(causal, s, NEG_INF), lambda h, j, s: s)
        for h in range(heads):
            o_ref[rows, h * dv:(h + 1) * dv] = outs[h].astype(o_ref.dtype)
        return 0

    lax.fori_loop(0, n_q, q_tile, 0)


def _mla_attention(q, kv, z1, *, batch, seq, tq=1024, heads=2):
    t = q.shape[0]
    scale = (C_NOPE_DIM + C_ROPE_DIM) ** -0.5
    wide = pl.BlockSpec((seq, heads * 2 * LANES), lambda b, g: (b, g))
    return pl.pallas_call(
        functools.partial(_mla_kernel, tq=tq, heads=heads, scale=scale),
        grid=(batch, C_HEADS // heads),
        in_specs=[wide, wide, pl.BlockSpec((seq, LANES), lambda b, g: (b, OFF_KPE // LANES))],
        out_specs=pl.BlockSpec((seq, heads * C_V_DIM), lambda b, g: (b, g)),
        out_shape=jax.ShapeDtypeStruct((t, C_WIDTH), BF16),
        compiler_params=_params("parallel", "parallel"),
        name="mla_attention",
    )(q, kv, z1)


def _moba_kernel(q_ref, k_ref, v_ref, o_ref, kmean_ref, *, tq, heads, scale):
    bs, d = MOBA_BLOCK, A_HEAD_DIM
    n_blk = q_ref.shape[0] // bs
    per_tile = tq // bs
    for h in range(heads):
        for n in range(n_blk):
            kmean_ref[h, n:n + 1, :] = jnp.mean(k_ref[n * bs:(n + 1) * bs, h * d:(h + 1) * d], axis=0, keepdims=True)
    blk_i = lax.broadcasted_iota(jnp.int32, (tq, n_blk), 1)
    blk = blk_i.astype(F32)
    bit_of_blk = lax.shift_left(jnp.ones_like(blk_i), blk_i).astype(F32)
    row1 = lax.broadcasted_iota(jnp.int32, (tq, 1), 0)
    row_sub = sum(((row1 >= c * bs).astype(F32) for c in range(1, per_tile)), jnp.zeros((tq, 1), F32))
    col_local = lax.broadcasted_iota(jnp.int32, (tq, bs), 1)

    def load_kvs(j):
        rows = pl.ds(pl.multiple_of(j * tq, tq), tq)
        return [(k_ref[rows, h * d:(h + 1) * d].astype(BF16), v_ref[rows, h * d:(h + 1) * d].astype(BF16))
                for h in range(heads)]

    def q_tile(qi, _):
        qi = jnp.asarray(qi, jnp.int32)
        rows = pl.ds(pl.multiple_of(qi * tq, tq), tq)
        first = qi * per_tile
        own = first.astype(F32) + row_sub
        qs, sels = [], []
        for h in range(heads):
            qf = q_ref[rows, h * d:(h + 1) * d]
            gate = lax.dot_general(qf, kmean_ref[h], (((1,), (1,)), ((), ())),
                                   precision=lax.Precision.HIGHEST, preferred_element_type=F32)
            gate = jnp.where(blk < own, gate, NEG_INF)
            sel = jnp.zeros((tq, n_blk), F32)
            for _ in range(MOBA_TOPK):
                best = jnp.max(gate, axis=1, keepdims=True)
                idx = jnp.min(jnp.where(gate == best, blk, float(n_blk)), axis=1, keepdims=True)
                pick = blk == idx
                sel = jnp.where(pick & (blk < own), 1.0, sel)
                gate = jnp.where(pick, PICKED, gate)
            qs.append(qf.astype(BF16))
            sels.append(jnp.sum(sel * bit_of_blk, axis=1, keepdims=True).astype(jnp.int32))

        def chosen(h, b):
            return (lax.shift_right_logical(sels[h], jnp.broadcast_to(b, sels[h].shape)) & 1) == 1

        def by_block(s, mask_block):
            parts = [mask_block(c, s[:, c * bs:(c + 1) * bs]) for c in range(per_tile)]
            return parts[0] if per_tile == 1 else jnp.concatenate(parts, axis=1)

        def diag_mask(h, s):
            def mask_block(c, sc):
                limit = jnp.where(row_sub == c, row1 - c * bs + 1,
                                  jnp.where((row_sub > c) & chosen(h, first + c), bs, 0))
                return jnp.where(col_local < limit, sc, NEG_INF)
            return by_block(s, mask_block)

        def past_mask(h, j, s):
            jb = j * per_tile
            return by_block(s, lambda c, sc: jnp.where(chosen(h, jb + c), sc, NEG_INF))

        outs = _causal_attend(qi, qs, load_kvs, tq, d, scale, diag_mask, past_mask)
        for h in range(heads):
            o_ref[rows, h * d:(h + 1) * d] = outs[h].astype(o_ref.dtype)
        return 0

    lax.fori_loop(0, q_ref.shape[0] // tq, q_tile, 0)


def _moba_attention(z1, *, batch, seq, tq=1024, heads=2):
    t = z1.shape[0]
    assert seq % tq == 0 and tq % MOBA_BLOCK == 0
    n_groups = A_HEADS // heads
    wide = heads * A_HEAD_DIM
    assert OFF_QA == 0 and OFF_KA == A_WIDTH and OFF_VA == 2 * A_WIDTH
    return pl.pallas_call(
        functools.partial(_moba_kernel, tq=tq, heads=heads, scale=A_HEAD_DIM ** -0.5),
        grid=(batch, n_groups),
        in_specs=[pl.BlockSpec((seq, wide), lambda b, g: (b, g)),
                  pl.BlockSpec((seq, wide), lambda b, g: (b, n_groups + g)),
                  pl.BlockSpec((seq, wide), lambda b, g: (b, 2 * n_groups + g))],
        out_specs=pl.BlockSpec((seq, wide), lambda b, g: (b, g)),
        out_shape=jax.ShapeDtypeStruct((t, A_WIDTH), BF16),
        scratch_shapes=[pltpu.VMEM((heads, seq // MOBA_BLOCK, A_HEAD_DIM), F32)],
        compiler_params=_params("parallel", "parallel"),
        name="moba_attention",
    )(z1, z1, z1)


def _swa_kernel(sinks_ref, q_ref, k_ref, v_ref, o_ref, *, scale):
    w = SWA_WINDOW
    n_blk = q_ref.shape[0] // w
    pair = pl.program_id(1)
    kv_odd = ((pair * 2) // (B_HEADS // B_KV_HEADS)) % 2
    lane = lax.broadcasted_iota(jnp.int32, (w, LANES), 1)
    low = lane < B_HEAD_DIM
    keep_orig = jnp.where(low, 0, 1) == kv_odd
    row = lax.broadcasted_iota(jnp.int32, (2 * w, 2 * w), 0)
    col = lax.broadcasted_iota(jnp.int32, (2 * w, 2 * w), 1)
    rel = (row & (w - 1)) + w - col
    band = (rel >= 0) & (rel < w)
    c_exp = scale * LOG2E
    sink_raw = jnp.where(lax.broadcasted_iota(jnp.int32, (2 * w, 1), 0) < w,
                         sinks_ref[2 * pair], sinks_ref[2 * pair + 1]) / scale

    def dup(x):
        return jnp.where(keep_orig, x, pltpu.roll(x, B_HEAD_DIM, axis=1)).astype(BF16)

    def q_block(n, first):
        start = n * w
        cur = pl.ds(start if first else pl.multiple_of(start, w), w)
        prev = cur if first else pl.ds(pl.multiple_of(start - w, w), w)
        q = q_ref[cur, :]
        zero = jnp.zeros_like(q)
        q2 = jnp.concatenate([jnp.where(low, q, zero), jnp.where(low, zero, q)], axis=0).astype(BF16)
        k2 = jnp.concatenate([dup(k_ref[prev, :]), dup(k_ref[cur, :])], axis=0)
        v2 = jnp.concatenate([dup(v_ref[prev, :]), dup(v_ref[cur, :])], axis=0)
        s = lax.dot_general(q2, k2, (((1,), (1,)), ((), ())), preferred_element_type=F32)
        s = jnp.where(band & (col >= w) if first else band, s, NEG_INF)
        m = jnp.maximum(jnp.max(s, axis=1, keepdims=True), sink_raw)
        p = jnp.exp2((s - m) * c_exp)
        denom = jnp.sum(p, axis=1, keepdims=True) + jnp.exp2((sink_raw - m) * c_exp)
        o2 = jnp.dot(p.astype(BF16), v2, preferred_element_type=F32) / denom
        o_ref[cur, :] = jnp.where(low, o2[:w], o2[w:]).astype(o_ref.dtype)

    for u in range(SWA_UNROLL):
        q_block(u, u == 0)

    def group(g, _):
        g = jnp.asarray(g, jnp.int32)
        for u in range(SWA_UNROLL):
            q_block(g * SWA_UNROLL + u, False)
        return 0

    lax.fori_loop(1, n_blk // SWA_UNROLL, group, 0)


def _swa_attention(z1, sinks, *, batch, seq):
    t = z1.shape[0]
    pairs = B_HEADS // 2
    rep = B_HEADS // B_KV_HEADS
    assert (seq // SWA_WINDOW) % SWA_UNROLL == 0
    kv_group = lambda p: (2 * p) // rep // 2
    return pl.pallas_call(
        functools.partial(_swa_kernel, scale=B_HEAD_DIM ** -0.5),
        grid_spec=pltpu.PrefetchScalarGridSpec(
            num_scalar_prefetch=1,
            grid=(batch, pairs),
            in_specs=[pl.BlockSpec((seq, LANES), lambda b, p, sinks: (b, OFF_QB // LANES + p)),
                      pl.BlockSpec((seq, LANES), lambda b, p, sinks: (b, OFF_KB // LANES + kv_group(p))),
                      pl.BlockSpec((seq, LANES), lambda b, p, sinks: (b, OFF_VB // LANES + kv_group(p)))],
            out_specs=pl.BlockSpec((seq, LANES), lambda b, p, sinks: (b, p)),
        ),
        out_shape=jax.ShapeDtypeStruct((t, B_WIDTH), BF16),
        compiler_params=_params("parallel", "parallel"),
        name="swa_attention",
    )(sinks, z1, z1, z1)


def _gated_out_kernel(oa_ref, ob_ref, oc_ref, ga_ref, gb_ref, gc_ref, wa_ref, wb_ref, wc_ref, y_ref,
                      wa_bf, wb_bf, wc_bf):
    @pl.when(pl.program_id(1) == 0)
    def _():
        wa_bf[...] = wa_ref[...].astype(BF16)
        wb_bf[...] = wb_ref[...].astype(BF16)
        wc_bf[...] = wc_ref[...].astype(BF16)

    y = jax.nn.sigmoid(ga_ref[...]) * jnp.dot(oa_ref[...], wa_bf[...], preferred_element_type=F32)
    y += jax.nn.sigmoid(gb_ref[...]) * jnp.dot(ob_ref[...], wb_bf[...], preferred_element_type=F32)
    y += jax.nn.sigmoid(gc_ref[...]) * jnp.dot(oc_ref[...], wc_bf[...], preferred_element_type=F32)
    y_ref[...] = y.astype(y_ref.dtype)


def _gated_out(oa, ob, oc, gates, w_out_a, w_out_b, w_out_c, layer, *, tm=512, tn=512):
    t = oa.shape[0]
    d = w_out_a.shape[2]
    nb = d // tn
    o_spec = lambda width: pl.BlockSpec((tm, width), lambda j, i: (i, 0))
    g_spec = lambda which: pl.BlockSpec((tm, tn), lambda j, i: (i, which * nb + j))
    w_spec = lambda width: pl.BlockSpec((None, width, tn), lambda j, i: (layer, 0, j))
    return pl.pallas_call(
        _gated_out_kernel,
        grid=(nb, t // tm),
        in_specs=[o_spec(A_WIDTH), o_spec(B_WIDTH), o_spec(C_WIDTH), g_spec(0), g_spec(1), g_spec(2),
                  w_spec(A_WIDTH), w_spec(B_WIDTH), w_spec(C_WIDTH)],
        out_specs=pl.BlockSpec((tm, tn), lambda j, i: (i, j)),
        out_shape=jax.ShapeDtypeStruct((t, d), BF16),
        scratch_shapes=[pltpu.VMEM((A_WIDTH, tn), BF16), pltpu.VMEM((B_WIDTH, tn), BF16),
                        pltpu.VMEM((C_WIDTH, tn), BF16)],
        compiler_params=_params("arbitrary", "arbitrary"),
        name="gated_out",
    )(oa, ob, oc, gates, gates, gates, w_out_a, w_out_b, w_out_c)


HI16 = -65536


def _bf16_bits_hi(v):
    return lax.bitcast_convert_type(v.astype(BF16).astype(F32), jnp.int32)


def _router_kernel(h_ref, g_ref, w_ref, b_ref, ids_ref, wts_ref, packed_ref):
    x = h_ref[...]
    hn = x * lax.rsqrt(jnp.mean(x * x, axis=-1, keepdims=True) + NORM_EPS) * g_ref[...]
    half = hn.shape[1] // 2
    packed_ref[...] = lax.shift_right_logical(_bf16_bits_hi(hn[:, :half]), 16) | _bf16_bits_hi(hn[:, half:])
    logits = jnp.dot(hn, w_ref[...], precision=lax.Precision.HIGHEST, preferred_element_type=F32) + b_ref[...]
    lane = lax.broadcasted_iota(jnp.int32, logits.shape, 1)
    far = float(LANES)
    is_g = lane < N_GROUPS
    g_id = lane.astype(F32)
    gmax = jnp.max(jnp.where(is_g, logits, NEG_INF), axis=1, keepdims=True)
    gsel = jnp.min(jnp.where(is_g & (logits == gmax), g_id, far), axis=1, keepdims=True)
    p_g = 1.0 / jnp.sum(jnp.where(is_g, jnp.exp(logits - gmax), 0.0), axis=1, keepdims=True)
    e_lane = lane - N_GROUPS
    e_id = e_lane.astype(F32)
    e_group = jnp.right_shift(e_lane, 2).astype(F32)
    in_grp = (e_lane >= 0) & (e_lane < N_EXPERTS) & (e_group == gsel)
    emax = jnp.max(jnp.where(in_grp, logits, NEG_INF), axis=1, keepdims=True)
    ee = jnp.where(in_grp, jnp.exp(jnp.where(in_grp, logits, emax) - emax), 0.0)
    ep = ee / jnp.sum(ee, axis=1, keepdims=True)
    p1 = jnp.max(jnp.where(in_grp, ep, -1.0), axis=1, keepdims=True)
    i1 = jnp.min(jnp.where(in_grp & (ep == p1), e_id, far), axis=1, keepdims=True)
    rest = in_grp & (e_id != i1)
    p2 = jnp.max(jnp.where(rest, ep, -1.0), axis=1, keepdims=True)
    i2 = jnp.min(jnp.where(rest & (ep == p2), e_id, far), axis=1, keepdims=True)
    tot = p1 + p2
    ids_ref[...] = jnp.where(lane == 0, i1, jnp.where(lane == 1, i2, 0.0)).astype(jnp.int32)
    wts_ref[...] = jnp.where(lane == 0, p_g * p1 / tot, jnp.where(lane == 1, p_g * p2 / tot, 0.0))


def _router(h, g, w_r, b_r, layer, *, tm=256):
    t, d = h.shape
    out = pl.BlockSpec((tm, LANES), lambda i: (i, 0))
    return pl.pallas_call(
        _router_kernel,
        grid=(t // tm,),
        in_specs=[pl.BlockSpec((tm, d), lambda i: (i, 0)),
                  pl.BlockSpec((None, 1, d), lambda i: (layer, 0, 0)),
                  pl.BlockSpec((None, d, LANES), lambda i: (layer, 0, 0)),
                  pl.BlockSpec((None, 1, LANES), lambda i: (layer, 0, 0))],
        out_specs=[out, out, pl.BlockSpec((tm, d // 2), lambda i: (i, 0))],
        out_shape=[jax.ShapeDtypeStruct((t, LANES), jnp.int32), jax.ShapeDtypeStruct((t, LANES), F32),
                   jax.ShapeDtypeStruct((t, d // 2), jnp.int32)],
        compiler_params=_params("parallel"),
        name="moe_router",
    )(h, g, w_r, b_r)


def _row_copy(src_hbm, tok, dst_ref, r, sem):
    return pltpu.make_async_copy(src_hbm.at[pl.ds(tok, 1), :], dst_ref.at[pl.ds(r, 1), :], sem)


GATHER_UNROLL = 8
NORM_CHUNK = 16


def _dispatch_kernel(tok_ref, nvalid_ref, h_hbm, o_ref, rows_ref, sem):
    i = pl.program_id(0)
    n_valid = nvalid_ref[0]
    n_rows = rows_ref.shape[1]
    slot = lax.rem(i, 2)

    def fetch(blk, slot):
        base = blk * n_rows

        def issue(g, _):
            for u in range(GATHER_UNROLL):
                r = g * GATHER_UNROLL + u
                _row_copy(h_hbm, tok_ref[base + r], rows_ref.at[slot], r, sem.at[slot]).start(priority=u % 2)
            return 0

        lax.fori_loop(0, n_rows // GATHER_UNROLL, issue, 0)

    @pl.when(i == 0)
    def _():
        fetch(0, 0)

    @pl.when(i + 1 < n_valid)
    def _():
        fetch(i + 1, 1 - slot)

    @pl.when(i < n_valid)
    def _():
        def drain(r, _):
            _row_copy(h_hbm, 0, rows_ref.at[slot], r, sem.at[slot]).wait()
            return 0

        lax.fori_loop(0, n_rows, drain, 0, unroll=GATHER_UNROLL)

        half = rows_ref.shape[2]

        def unpack(c, _):
            rows = pl.ds(pl.multiple_of(jnp.asarray(c, jnp.int32) * NORM_CHUNK, NORM_CHUNK), NORM_CHUNK)
            word = rows_ref[slot, rows, :]
            o_ref[rows, :half] = lax.bitcast_convert_type(lax.shift_left(word, 16), F32).astype(o_ref.dtype)
            o_ref[rows, half:] = lax.bitcast_convert_type(word & HI16, F32).astype(o_ref.dtype)
            return 0

        lax.fori_loop(0, n_rows // NORM_CHUNK, unpack, 0, unroll=4)


def _dispatch(packed, buf_tok, nvalid, *, n_blocks):
    t, half = packed.shape
    blk = lambda i, tok, nv: (jnp.minimum(i, nv[0] - 1), 0)
    return pl.pallas_call(
        _dispatch_kernel,
        grid_spec=pltpu.PrefetchScalarGridSpec(
            num_scalar_prefetch=2,
            grid=(n_blocks,),
            in_specs=[pl.BlockSpec(memory_space=pl.ANY)],
            out_specs=pl.BlockSpec((MOE_ROWS, 2 * half), blk),
            scratch_shapes=[pltpu.VMEM((2, MOE_ROWS, half), jnp.int32), pltpu.SemaphoreType.DMA((2,))],
        ),
        out_shape=jax.ShapeDtypeStruct((n_blocks * MOE_ROWS, 2 * half), BF16),
        compiler_params=_params("arbitrary"),
        name="moe_dispatch",
    )(buf_tok, nvalid, packed)


def _run_weights(be_ref, first_ref, next_ref, i, pass_id, n_passes, n_blocks, copies, stage):
    @pl.when((pass_id == 0) & (i == 0))
    def _():
        for c in copies(be_ref[0], 0):
            c.start()

    @pl.when(first_ref[i] == 1)
    def _():
        for c in copies(be_ref[i], pass_id):
            c.wait()
        stage()
        nxt = next_ref[i]
        same_pass = nxt < n_blocks
        e_next = jnp.where(same_pass, be_ref[jnp.minimum(nxt, n_blocks - 1)], be_ref[0])
        pass_next = jnp.where(same_pass, pass_id, pass_id + 1)

        @pl.when(same_pass | (pass_id + 1 < n_passes))
        def _():
            for c in copies(e_next, pass_next):
                c.start()


def _expert_up_kernel(be_ref, nvalid_ref, first_ref, next_ref, x_ref, wg_hbm, wu_hbm, o_ref,
                      wg_land, wu_land, wg_bf, wu_bf, sem, *, layer):
    f, i = pl.program_id(0), pl.program_id(1)
    tf = wg_land.shape[1]

    def copies(e, ff_tile):
        cols = pl.ds(pl.multiple_of(ff_tile * tf, tf), tf)
        return (pltpu.make_async_copy(wg_hbm.at[layer, e, :, cols], wg_land, sem.at[0]),
                pltpu.make_async_copy(wu_hbm.at[layer, e, :, cols], wu_land, sem.at[1]))

    def stage():
        wg_bf[...] = wg_land[...].astype(BF16)
        wu_bf[...] = wu_land[...].astype(BF16)

    _run_weights(be_ref, first_ref, next_ref, i, f, pl.num_programs(0), pl.num_programs(1), copies, stage)

    @pl.when(i < nvalid_ref[0])
    def _():
        x = x_ref[...]
        hg = jnp.dot(x, wg_bf[...], preferred_element_type=F32)
        hu = jnp.dot(x, wu_bf[...], preferred_element_type=F32)
        o_ref[...] = (jax.nn.silu(hg) * hu).astype(o_ref.dtype)


def _expert_up(xb, w_gate, w_up, layer, runs, *, n_blocks):
    d = xb.shape[1]
    ff = w_gate.shape[3]
    tf = MOE_FF_TILE
    row_blk = lambda f, i, be, nv, first, nxt: (jnp.minimum(i, nv[0] - 1), 0)
    return pl.pallas_call(
        functools.partial(_expert_up_kernel, layer=layer),
        grid_spec=pltpu.PrefetchScalarGridSpec(
            num_scalar_prefetch=4,
            grid=(ff // tf, n_blocks),
            in_specs=[pl.BlockSpec((MOE_ROWS, d), row_blk), pl.BlockSpec(memory_space=pl.ANY),
                      pl.BlockSpec(memory_space=pl.ANY)],
            out_specs=pl.BlockSpec((MOE_ROWS, tf),
                                   lambda f, i, be, nv, first, nxt: (jnp.minimum(i, nv[0] - 1), f)),
            scratch_shapes=[pltpu.VMEM((d, tf), F32), pltpu.VMEM((d, tf), F32),
                            pltpu.VMEM((d, tf), BF16), pltpu.VMEM((d, tf), BF16),
                            pltpu.SemaphoreType.DMA((2,))],
        ),
        out_shape=jax.ShapeDtypeStruct((n_blocks * MOE_ROWS, ff), BF16),
        compiler_params=_params("arbitrary", "arbitrary"),
        name="moe_expert_up",
    )(*runs, xb, w_gate, w_up)


def _expert_down_kernel(be_ref, nvalid_ref, first_ref, next_ref, a_ref, wd_hbm, o_ref, wd_land, wd_bf, sem,
                        *, layer):
    i = pl.program_id(0)

    def copies(e, _):
        return (pltpu.make_async_copy(wd_hbm.at[layer, e], wd_land, sem.at[0]),)

    def stage():
        wd_bf[...] = wd_land[...].astype(BF16)

    _run_weights(be_ref, first_ref, next_ref, i, 0, 1, pl.num_programs(0), copies, stage)

    @pl.when(i < nvalid_ref[0])
    def _():
        o_ref[...] = jnp.dot(a_ref[...], wd_bf[...], preferred_element_type=F32)


def _expert_down(act, w_down, layer, runs, *, n_blocks):
    ff, d = w_down.shape[2], w_down.shape[3]
    row_blk = lambda i, be, nv, first, nxt: (jnp.minimum(i, nv[0] - 1), 0)
    return pl.pallas_call(
        functools.partial(_expert_down_kernel, layer=layer),
        grid_spec=pltpu.PrefetchScalarGridSpec(
            num_scalar_prefetch=4,
            grid=(n_blocks,),
            in_specs=[pl.BlockSpec((MOE_ROWS, ff), row_blk), pl.BlockSpec(memory_space=pl.ANY)],
            out_specs=pl.BlockSpec((MOE_ROWS, d), row_blk),
            scratch_shapes=[pltpu.VMEM((ff, d), F32), pltpu.VMEM((ff, d), BF16), pltpu.SemaphoreType.DMA((1,))],
        ),
        out_shape=jax.ShapeDtypeStruct((n_blocks * MOE_ROWS, d), F32),
        compiler_params=_params("arbitrary"),
        name="moe_expert_down",
    )(*runs, act, w_down)


def _combine_kernel(pos_ref, h_ref, wts_ref, g_ref, y_hbm, *refs, emit_h):
    o_ref = refs[0] if emit_h else None
    n_ref, y0_ref, y1_ref, sem = refs[int(emit_h):]
    i = pl.program_id(0)
    tm = h_ref.shape[0]
    slot = lax.rem(i, 2)

    def fetch(blk, slot):
        base = blk * tm * EXPERT_TOPK

        def issue(g, _):
            for u in range(GATHER_UNROLL):
                r = g * GATHER_UNROLL + u
                _row_copy(y_hbm, pos_ref[base + EXPERT_TOPK * r], y0_ref.at[slot], r, sem.at[slot]).start(priority=0)
                _row_copy(y_hbm, pos_ref[base + EXPERT_TOPK * r + 1], y1_ref.at[slot], r,
                          sem.at[slot]).start(priority=1)
            return 0

        lax.fori_loop(0, tm // GATHER_UNROLL, issue, 0)

    @pl.when(i == 0)
    def _():
        fetch(0, 0)

    @pl.when(i + 1 < pl.num_programs(0))
    def _():
        fetch(i + 1, 1 - slot)

    def drain(r, _):
        _row_copy(y_hbm, 0, y0_ref.at[slot], r, sem.at[slot]).wait()
        _row_copy(y_hbm, 0, y1_ref.at[slot], r, sem.at[slot]).wait()
        return 0

    lax.fori_loop(0, tm, drain, 0, unroll=GATHER_UNROLL)

    def rows_chunk(c, _):
        rows = pl.ds(pl.multiple_of(jnp.asarray(c, jnp.int32) * NORM_CHUNK, NORM_CHUNK), NORM_CHUNK)
        w = wts_ref[rows, :]
        hv = h_ref[rows, :] + w[:, 0:1] * y0_ref[slot, rows, :] + w[:, 1:2] * y1_ref[slot, rows, :]
        if emit_h:
            o_ref[rows, :] = hv
        hn = hv * lax.rsqrt(jnp.mean(hv * hv, axis=-1, keepdims=True) + NORM_EPS) * g_ref[...]
        n_ref[rows, :] = hn.astype(n_ref.dtype)
        return 0

    lax.fori_loop(0, tm // NORM_CHUNK, rows_chunk, 0, unroll=2)


def _combine(h, wts, y, pos, g, *, norm_dtype, emit_h, tm=128):
    t, d = h.shape
    row = pl.BlockSpec((tm, d), lambda i, pos: (i, 0))
    outs = pl.pallas_call(
        functools.partial(_combine_kernel, emit_h=emit_h),
        grid_spec=pltpu.PrefetchScalarGridSpec(
            num_scalar_prefetch=1,
            grid=(t // tm,),
            in_specs=[row, pl.BlockSpec((tm, LANES), lambda i, pos: (i, 0)),
                      pl.BlockSpec((1, d), lambda i, pos: (0, 0)), pl.BlockSpec(memory_space=pl.ANY)],
            out_specs=[row] * (1 + emit_h),
            scratch_shapes=[pltpu.VMEM((2, tm, d), F32), pltpu.VMEM((2, tm, d), F32),
                            pltpu.SemaphoreType.DMA((2,))],
        ),
        out_shape=([jax.ShapeDtypeStruct((t, d), F32)] if emit_h else [])
        + [jax.ShapeDtypeStruct((t, d), norm_dtype)],
        compiler_params=_params("arbitrary"),
        name="moe_combine",
    )(pos, h, wts, g.reshape(1, d), y)
    return (outs[0], outs[1]) if emit_h else (None, outs[0])


def _hier_moe(h, ffn_norm_g, w_r, b_r, w_gate, w_up, w_down, layer, next_norm_g, *, norm_dtype, emit_h):
    t, d = h.shape
    tk = t * EXPERT_TOPK
    n_blocks = tk // MOE_ROWS + N_EXPERTS
    ids, wts, hn_packed = _router(h, ffn_norm_g, w_r, b_r, layer)
    flat_e = ids[:, :EXPERT_TOPK].reshape(tk)
    onehot = (flat_e[:, None] == jnp.arange(N_EXPERTS, dtype=jnp.int32)[None, :]).astype(jnp.int32)
    csum = jnp.cumsum(onehot, axis=0)
    rank = jnp.sum(onehot * csum, axis=1) - 1
    counts = csum[-1]
    padded = (counts + MOE_ROWS - 1) // MOE_ROWS * MOE_ROWS
    pad_end = jnp.cumsum(padded)
    pad_start = pad_end - padded
    dest = (pad_start[flat_e] + rank).astype(jnp.int32)
    flat_tok = jnp.arange(tk, dtype=jnp.int32) // EXPERT_TOPK
    buf_tok = jnp.zeros((n_blocks * MOE_ROWS,), jnp.int32).at[dest].set(flat_tok)
    nvalid = (pad_end[-1] // MOE_ROWS).astype(jnp.int32).reshape(1)
    blk_start = jnp.minimum(jnp.arange(n_blocks, dtype=jnp.int32), nvalid[0] - 1) * MOE_ROWS
    block_expert = jnp.minimum(jnp.searchsorted(pad_end, blk_start, side='right'), N_EXPERTS - 1).astype(jnp.int32)

    blk = jnp.arange(n_blocks, dtype=jnp.int32)
    prev_expert = jnp.concatenate([jnp.full((1,), -1, jnp.int32), block_expert[:-1]])
    run_first = (blk < nvalid[0]) & (block_expert != prev_expert)
    first_at_or_after = lax.cummin(jnp.where(run_first, blk, n_blocks), reverse=True)
    run_next = jnp.concatenate([first_at_or_after[1:], jnp.full((1,), n_blocks, jnp.int32)])
    runs = (block_expert, nvalid, run_first.astype(jnp.int32), run_next)

    xb = _dispatch(hn_packed, buf_tok, nvalid, n_blocks=n_blocks)
    act = _expert_up(xb, w_gate, w_up, layer, runs, n_blocks=n_blocks)
    y = _expert_down(act, w_down, layer, runs, n_blocks=n_blocks)
    return _combine(h, wts, y, dest, next_norm_g, norm_dtype=norm_dtype, emit_h=emit_h)


ROPE_A, ROPE_B, ROPE_PE = 0, 1, 2


def _in_proj_rope_table(col):
    if OFF_QA <= col < OFF_VA:
        return ROPE_A
    if OFF_QB <= col < OFF_VB:
        return ROPE_B
    if col == OFF_KPE:
        return ROPE_PE
    return None


def _mla_q_rope_table(col):
    return 0 if (col // LANES) % 2 == 1 else None


def kernel(x, positions, attn_norm_g, w_in, q_norm_g, kv_norm_g, wq_b, wkv_b, sinks, w_out_a, w_out_b,
           w_out_c, w_o, ffn_norm_g, w_group, b_group, w_expert, b_expert, w_gate, w_up, w_down,
           final_norm_g):
    batch, seq, d = x.shape
    depth = w_in.shape[0]
    t = batch * seq
    h = x.reshape(t, d)
    pos = positions.reshape(t).astype(F32)

    w_in_t = jnp.swapaxes(w_in, 1, 2)
    q_head = C_NOPE_DIM + C_ROPE_DIM
    wq_pad = jnp.pad(wq_b.reshape(depth, C_Q_RANK, C_HEADS, q_head),
                     ((0, 0), (0, 0), (0, 0), (0, 2 * LANES - q_head))
                     ).reshape(depth, C_Q_RANK, C_HEADS * 2 * LANES).astype(BF16)
    w_r = jnp.concatenate([w_group, w_expert,
                           jnp.zeros((depth, d, LANES - N_GROUPS - N_EXPERTS), F32)], axis=2)
    b_r = jnp.concatenate([b_group, b_expert,
                           jnp.zeros((depth, LANES - N_GROUPS - N_EXPERTS), F32)], axis=1).reshape(depth, 1, LANES)
    attn_g = attn_norm_g.reshape(depth, 1, d)
    ffn_g = ffn_norm_g.reshape(depth, 1, d)

    c_a, s_a = _rope_tables(pos, A_HEAD_DIM)
    c_b, s_b = _rope_tables(pos, B_HEAD_DIM)
    c_pe, s_pe = _rope_tables(pos, C_ROPE_DIM, keep_lanes=C_ROPE_DIM)
    rope_tables = ((c_a, s_a, A_HEAD_DIM // 2), (c_b, s_b, B_HEAD_DIM // 2), (c_pe, s_pe, C_ROPE_DIM // 2))

    hn = _rmsnorm(h, attn_norm_g[0])
    for l in range(depth):
        z1 = _matmul(hn, w_in_t, l, n_out=Z1_WIDTH, w_is_nk=True, rope_tables=rope_tables,
                     rope_group_table=_in_proj_rope_table, name="in_proj")
        gates = _matmul(hn, w_in_t, l, n_out=3 * d, col_off=OFF_GATES, w_is_nk=True, name="gate_proj")
        o_a = _moba_attention(z1, batch=batch, seq=seq)
        o_b = _swa_attention(z1, sinks[l], batch=batch, seq=seq)
        cq_n = _rmsnorm(z1, q_norm_g[l], col_off=OFF_CQ, width=C_Q_RANK, piece=512)
        ckv_n = _rmsnorm(z1, kv_norm_g[l], col_off=OFF_CKV, width=C_KV_RANK, piece=512)
        q_c = _matmul(cq_n, wq_pad, l, n_out=C_HEADS * 2 * LANES, tn=1024, out_dtype=BF16,
                      rope_tables=rope_tables[ROPE_PE:], rope_group_table=_mla_q_rope_table, name="mla_q_proj")
        kv_c = _matmul(ckv_n, wkv_b, l, n_out=C_HEADS * (C_NOPE_DIM + C_V_DIM), tn=1024, out_dtype=BF16,
                       name="mla_kv_proj")
        o_c = _mla_attention(q_c, kv_c, z1, batch=batch, seq=seq)
        y = _gated_out(o_a, o_b, o_c, gates, w_out_a, w_out_b, w_out_c, l)
        h = _matmul(y, w_o, l, n_out=d, residual=h, name="out_proj")
        last = l == depth - 1
        h, hn = _hier_moe(h, ffn_g, w_r, b_r, w_gate, w_up, w_down, l,
                          final_norm_g if last else attn_norm_g[l + 1],
                          norm_dtype=F32 if last else BF16, emit_h=not last)
    return hn.reshape(batch, seq, d)
```

```python
import functools

import jax
import jax.numpy as jnp
from jax import lax
from jax.experimental import pallas as pl
from jax.experimental.pallas import tpu as pltpu

F32 = jnp.float32
BF16 = jnp.bfloat16

ROPE_THETA = 10000.0
NORM_EPS = 1e-6
NEG_INF = -1e30
PICKED = -3e38
LOG2E = 1.4426950408889634

A_HEADS, A_HEAD_DIM = 16, 128
MOBA_BLOCK, MOBA_TOPK = 256, 3
B_HEADS, B_KV_HEADS, B_HEAD_DIM, SWA_WINDOW = 32, 4, 64, 128
C_HEADS, C_Q_RANK, C_KV_RANK, C_NOPE_DIM, C_ROPE_DIM, C_V_DIM = 16, 1024, 512, 128, 64, 128
N_GROUPS, EXPERTS_PER_GROUP, EXPERT_TOPK, D_FF_EXPERT = 8, 4, 2, 768
N_EXPERTS = N_GROUPS * EXPERTS_PER_GROUP
assert EXPERTS_PER_GROUP == 4

LANES = 128
VMEM_LIMIT_BYTES = 58 * 1024 * 1024

A_WIDTH = A_HEADS * A_HEAD_DIM
B_WIDTH = B_HEADS * B_HEAD_DIM
B_KV_WIDTH = B_KV_HEADS * B_HEAD_DIM
C_WIDTH = C_HEADS * C_V_DIM
OFF_QA = 0
OFF_KA = OFF_QA + A_WIDTH
OFF_VA = OFF_KA + A_WIDTH
OFF_QB = OFF_VA + A_WIDTH
OFF_KB = OFF_QB + B_WIDTH
OFF_VB = OFF_KB + B_KV_WIDTH
OFF_CQ = OFF_VB + B_KV_WIDTH
OFF_CKV = OFF_CQ + C_Q_RANK
OFF_KPE = OFF_CKV + C_KV_RANK
OFF_GATES = OFF_KPE + C_ROPE_DIM
Z1_WIDTH = 10752

MOE_ROWS = 256
MOE_FF_TILE = 256
SWA_UNROLL = 8


def _params(*sem):
    return pltpu.CompilerParams(dimension_semantics=sem, vmem_limit_bytes=VMEM_LIMIT_BYTES)


def _rmsnorm_kernel(*refs, n_pieces, width):
    x_refs, g_ref, o_ref = refs[:n_pieces], refs[n_pieces], refs[n_pieces + 1]
    xs = [r[...].astype(F32) for r in x_refs]
    ss = sum(jnp.sum(x * x, axis=-1, keepdims=True) for x in xs)
    inv = lax.rsqrt(ss * (1.0 / width) + NORM_EPS)
    pw = xs[0].shape[1]
    for p, x in enumerate(xs):
        o_ref[:, p * pw:(p + 1) * pw] = (x * inv * g_ref[:, p * pw:(p + 1) * pw]).astype(o_ref.dtype)


def _rmsnorm(x, g, *, col_off=0, width=None, piece=None, tm=256, out_dtype=BF16):
    t = x.shape[0]
    width = width or x.shape[1]
    piece = piece or width
    n_pieces = width // piece
    off = col_off // piece
    assert col_off % piece == 0 and width % piece == 0
    in_specs = [pl.BlockSpec((tm, piece), functools.partial(lambda i, p: (i, off + p), p=p))
                for p in range(n_pieces)]
    in_specs.append(pl.BlockSpec((1, width), lambda i: (0, 0)))
    return pl.pallas_call(
        functools.partial(_rmsnorm_kernel, n_pieces=n_pieces, width=width),
        grid=(t // tm,),
        in_specs=in_specs,
        out_specs=pl.BlockSpec((tm, width), lambda i: (i, 0)),
        out_shape=jax.ShapeDtypeStruct((t, width), out_dtype),
        compiler_params=_params("parallel"),
        name="rmsnorm",
    )(*([x] * n_pieces), g.reshape(1, width))


def _stage_weight(w_ref, wbf_ref, w_is_nk):
    if not w_is_nk:
        wbf_ref[...] = w_ref[...].astype(BF16)
        return
    _, tn, k = w_ref.shape
    for c in range(k // tn):
        wbf_ref[c * tn:(c + 1) * tn, :] = w_ref[0, :, c * tn:(c + 1) * tn].T.astype(BF16)


def _swap_halves(x, half):
    if half == 64:
        return pltpu.roll(x, 64, axis=1)
    lane = lax.broadcasted_iota(jnp.int32, x.shape, 1)
    return jnp.where((lane & half) == 0, pltpu.roll(x, LANES - half, axis=1), pltpu.roll(x, half, axis=1))


def _matmul_kernel(*refs, stage_w, w_is_nk, has_res, rope_modes, rope_halves):
    x_ref, w_ref = refs[0], refs[1]
    n_in = 2 + has_res + 2 * len(rope_halves)
    r_ref = refs[2] if has_res else None
    table_refs = refs[2 + has_res:n_in]
    o_ref = refs[n_in]
    if stage_w:
        wbf_ref = refs[n_in + 1]

        @pl.when(pl.program_id(1) == 0)
        def _():
            _stage_weight(w_ref, wbf_ref, w_is_nk)

        w_src = wbf_ref
    else:
        w_src = w_ref

    def product():
        acc = jnp.dot(x_ref[...], w_src[...], preferred_element_type=F32)
        return acc + r_ref[...] if has_res else acc

    if rope_modes is None:
        o_ref[...] = product().astype(o_ref.dtype)
        return
    j = pl.program_id(0)
    for pattern in sorted(set(rope_modes), key=str):
        blocks = [b for b, p in enumerate(rope_modes) if p == pattern]

        @pl.when(functools.reduce(jnp.logical_or, [j == b for b in blocks]))
        def _(pattern=pattern):
            acc = product()
            for g, table in enumerate(pattern):
                xg = acc[:, g * LANES:(g + 1) * LANES]
                if table is not None:
                    c_ref, s_ref = table_refs[2 * table], table_refs[2 * table + 1]
                    xg = xg * c_ref[...] + _swap_halves(xg, rope_halves[table]) * s_ref[...]
                o_ref[:, g * LANES:(g + 1) * LANES] = xg.astype(o_ref.dtype)


def _matmul(x, w, layer, *, n_out, col_off=0, w_is_nk=False, tm=1024, tn=512, out_dtype=F32, residual=None,
            rope_tables=(), rope_group_table=None, name="matmul"):
    m, k = x.shape
    assert w.shape[2 if w_is_nk else 1] == k and m % tm == 0 and n_out % tn == 0
    stage_w = w_is_nk or w.dtype != BF16
    rope_modes = None
    if rope_group_table is not None:
        rope_modes = tuple(tuple(rope_group_table(b * tn + g * LANES) for g in range(tn // LANES))
                           for b in range(n_out // tn))
    if w_is_nk:
        assert k % tn == 0 and col_off % 8 == 0
        w_spec = pl.BlockSpec((pl.Element(1), pl.Element(tn), pl.Element(k)),
                              lambda j, i: (layer, pl.multiple_of(col_off + j * tn, 8), 0))
    else:
        assert col_off % tn == 0
        w_spec = pl.BlockSpec((None, k, tn), lambda j, i: (layer, 0, j + col_off // tn))
    in_specs = [pl.BlockSpec((tm, k), lambda j, i: (i, 0)), w_spec]
    args = [x, w]
    if residual is not None:
        in_specs.append(pl.BlockSpec((tm, tn), lambda j, i: (i, j)))
        args.append(residual)
    for c, s, _ in rope_tables:
        in_specs += [pl.BlockSpec((tm, LANES), lambda j, i: (i, 0))] * 2
        args += [c, s]
    return pl.pallas_call(
        functools.partial(_matmul_kernel, stage_w=stage_w, w_is_nk=w_is_nk, has_res=residual is not None,
                          rope_modes=rope_modes, rope_halves=tuple(half for _, _, half in rope_tables)),
        grid=(n_out // tn, m // tm),
        in_specs=in_specs,
        out_specs=pl.BlockSpec((tm, tn), lambda j, i: (i, j)),
        out_shape=jax.ShapeDtypeStruct((m, n_out), out_dtype),
        scratch_shapes=[pltpu.VMEM((k, tn), BF16)] if stage_w else [],
        compiler_params=_params("arbitrary", "arbitrary"),
        name=name,
    )(*args)


def _rope_tables_kernel(pos_ref, invf_ref, sign_ref, keep_ref, c_ref, s_ref):
    ang = pos_ref[...] * invf_ref[...]
    c_ref[...] = jnp.cos(ang) * keep_ref[...]
    s_ref[...] = jnp.sin(ang) * sign_ref[...]


def _rope_tables(pos, dim, *, keep_lanes=LANES):
    t = pos.shape[0]
    half = dim // 2
    lane = jnp.arange(LANES)
    invf = (ROPE_THETA ** (-(2.0 * (lane % half)).astype(F32) / dim)).reshape(1, LANES)
    keep = (lane < keep_lanes).astype(F32).reshape(1, LANES)
    sign = jnp.where((lane % dim) < half, -1.0, 1.0).astype(F32).reshape(1, LANES) * keep
    tm = 1024
    row = pl.BlockSpec((1, LANES), lambda i: (0, 0))
    out = pl.BlockSpec((tm, LANES), lambda i: (i, 0))
    return pl.pallas_call(
        _rope_tables_kernel,
        grid=(t // tm,),
        in_specs=[pl.BlockSpec((tm, 1), lambda i: (i, 0)), row, row, row],
        out_specs=[out, out],
        out_shape=[jax.ShapeDtypeStruct((t, LANES), F32)] * 2,
        compiler_params=_params("parallel"),
        name="rope_tables",
    )(pos.reshape(t, 1), invf, sign, keep)


def _online_block(q, k, v, s_mask, carry, c_exp):
    m, l, acc = carry
    s = s_mask(lax.dot_general(q, k, (((1,), (1,)), ((), ())), preferred_element_type=F32))
    m_new = jnp.maximum(m, jnp.max(s, axis=1, keepdims=True))
    alpha = jnp.exp2((m - m_new) * c_exp)
    p = jnp.exp2((s - m_new) * c_exp)
    l = alpha * l + jnp.sum(p, axis=1, keepdims=True)
    acc = alpha * acc + jnp.dot(p.astype(BF16), v, preferred_element_type=F32)
    return m_new, l, acc


def _causal_attend(qi, qs, load_kvs, tq, dv, scale, diag_mask, past_mask):
    heads = range(len(qs))
    c_exp = scale * LOG2E
    init = (jnp.full((tq, 1), NEG_INF, F32), jnp.zeros((tq, 1), F32), jnp.zeros((tq, dv), F32))
    kvs = load_kvs(qi)
    carries = tuple(_online_block(qs[h], *kvs[h], functools.partial(diag_mask, h), init, c_exp) for h in heads)

    def past(j, carries):
        j = jnp.asarray(j, jnp.int32)
        kvs = load_kvs(j)
        return tuple(_online_block(qs[h], *kvs[h], functools.partial(past_mask, h, j), carries[h], c_exp)
                     for h in heads)

    carries = lax.fori_loop(0, qi, past, carries)
    return [acc / l for _, l, acc in carries]


def _mla_kernel(q_ref, kv_ref, kpe_ref, o_ref, *, tq, heads, scale):
    n_q = q_ref.shape[0] // tq
    qw, dv = 2 * LANES, C_V_DIM
    row = lax.broadcasted_iota(jnp.int32, (tq, tq), 0)
    col = lax.broadcasted_iota(jnp.int32, (tq, tq), 1)
    causal = col <= row

    def load_kvs(j):
        rows = pl.ds(pl.multiple_of(j * tq, tq), tq)
        kpe = kpe_ref[rows, :].astype(BF16)
        return [(jnp.concatenate([kv_ref[rows, h * qw:h * qw + C_NOPE_DIM], kpe], axis=1),
                 kv_ref[rows, h * qw + C_NOPE_DIM:(h + 1) * qw]) for h in range(heads)]

    def q_tile(qi, _):
        qi = jnp.asarray(qi, jnp.int32)
        rows = pl.ds(pl.multiple_of(qi * tq, tq), tq)
        qs = [q_ref[rows, h * qw:(h + 1) * qw] for h in range(heads)]
        outs = _causal_attend(qi, qs, load_kvs, tq, dv, scale,
                              lambda h, s: jnp.where(causal, s, NEG_INF), lambda h, j, s: s)
        for h in range(heads):
            o_ref[rows, h * dv:(h + 1) * dv] = outs[h].astype(o_ref.dtype)
        return 0

    lax.fori_loop(0, n_q, q_tile, 0)


def _mla_attention(q, kv, z1, *, batch, seq, tq=1024, heads=2):
    t = q.shape[0]
    scale = (C_NOPE_DIM + C_ROPE_DIM) ** -0.5
    wide = pl.BlockSpec((seq, heads * 2 * LANES), lambda b, g: (b, g))
    return pl.pallas_call(
        functools.partial(_mla_kernel, tq=tq, heads=heads, scale=scale),
        grid=(batch, C_HEADS // heads),
        in_specs=[wide, wide, pl.BlockSpec((seq, LANES), lambda b, g: (b, OFF_KPE // LANES))],
        out_specs=pl.BlockSpec((seq, heads * C_V_DIM), lambda b, g: (b, g)),
        out_shape=jax.ShapeDtypeStruct((t, C_WIDTH), BF16),
        compiler_params=_params("parallel", "parallel"),
        name="mla_attention",
    )(q, kv, z1)


def _moba_kernel(q_ref, k_ref, v_ref, o_ref, kmean_ref, *, tq, heads, scale):
    bs, d = MOBA_BLOCK, A_HEAD_DIM
    n_blk = q_ref.shape[0] // bs
    per_tile = tq // bs
    for h in range(heads):
        for n in range(n_blk):
            kmean_ref[h, n:n + 1, :] = jnp.mean(k_ref[n * bs:(n + 1) * bs, h * d:(h + 1) * d], axis=0, keepdims=True)
    blk_i = lax.broadcasted_iota(jnp.int32, (n_blk, tq), 0)
    blk = blk_i.astype(F32)
    bit_of_blk = lax.shift_left(jnp.ones_like(blk_i), blk_i).astype(F32)
    lane_q = lax.broadcasted_iota(jnp.int32, (1, tq), 1)
    lane_sub = sum(((lane_q >= c * bs).astype(F32) for c in range(1, per_tile)), jnp.zeros((1, tq), F32))
    row1 = lax.broadcasted_iota(jnp.int32, (tq, 1), 0)
    row_sub = sum(((row1 >= c * bs).astype(F32) for c in range(1, per_tile)), jnp.zeros((tq, 1), F32))
    col_local = lax.broadcasted_iota(jnp.int32, (tq, bs), 1)

    def load_kvs(j):
        rows = pl.ds(pl.multiple_of(j * tq, tq), tq)
        return [(k_ref[rows, h * d:(h + 1) * d].astype(BF16), v_ref[rows, h * d:(h + 1) * d].astype(BF16))
                for h in range(heads)]

    def q_tile(qi, _):
        qi = jnp.asarray(qi, jnp.int32)
        rows = pl.ds(pl.multiple_of(qi * tq, tq), tq)
        first = qi * per_tile
        own = first.astype(F32) + lane_sub
        qs, sels = [], []
        for h in range(heads):
            qf = q_ref[rows, h * d:(h + 1) * d]
            gate = lax.dot_general(kmean_ref[h], qf, (((1,), (1,)), ((), ())),
                                   precision=lax.Precision.HIGHEST, preferred_element_type=F32)
            gate = jnp.where(blk < own, gate, NEG_INF)
            sel = jnp.zeros((n_blk, tq), F32)
            for _ in range(MOBA_TOPK):
                best = jnp.max(gate, axis=0, keepdims=True)
                idx = jnp.min(jnp.where(gate == best, blk, float(n_blk)), axis=0, keepdims=True)
                pick = blk == idx
                sel = jnp.where(pick & (blk < own), 1.0, sel)
                gate = jnp.where(pick, PICKED, gate)
            qs.append(qf.astype(BF16))
            code = jnp.sum(sel * bit_of_blk, axis=0, keepdims=True)
            code_col = jnp.max(jnp.broadcast_to(code, (LANES, tq)).T, axis=1, keepdims=True)
            sels.append(code_col.astype(jnp.int32))

        def chosen(h, b):
            return (lax.shift_right_logical(sels[h], jnp.broadcast_to(b, sels[h].shape)) & 1) == 1

        def by_block(s, mask_block):
            parts = [mask_block(c, s[:, c * bs:(c + 1) * bs]) for c in range(per_tile)]
            return parts[0] if per_tile == 1 else jnp.concatenate(parts, axis=1)

        def diag_mask(h, s):
            def mask_block(c, sc):
                limit = jnp.where(row_sub == c, row1 - c * bs + 1,
                                  jnp.where((row_sub > c) & chosen(h, first + c), bs, 0))
                return jnp.where(col_local < limit, sc, NEG_INF)
            return by_block(s, mask_block)

        def past_mask(h, j, s):
            jb = j * per_tile
            return by_block(s, lambda c, sc: jnp.where(chosen(h, jb + c), sc, NEG_INF))

        outs = _causal_attend(qi, qs, load_kvs, tq, d, scale, diag_mask, past_mask)
        for h in range(heads):
            o_ref[rows, h * d:(h + 1) * d] = outs[h].astype(o_ref.dtype)
        return 0

    lax.fori_loop(0, q_ref.shape[0] // tq, q_tile, 0)


def _moba_attention(z1, *, batch, seq, tq=1024, heads=2):
    t = z1.shape[0]
    assert seq % tq == 0 and tq % MOBA_BLOCK == 0
    n_groups = A_HEADS // heads
    wide = heads * A_HEAD_DIM
    assert OFF_QA == 0 and OFF_KA == A_WIDTH and OFF_VA == 2 * A_WIDTH
    return pl.pallas_call(
        functools.partial(_moba_kernel, tq=tq, heads=heads, scale=A_HEAD_DIM ** -0.5),
        grid=(batch, n_groups),
        in_specs=[pl.BlockSpec((seq, wide), lambda b, g: (b, g)),
                  pl.BlockSpec((seq, wide), lambda b, g: (b, n_groups + g)),
                  pl.BlockSpec((seq, wide), lambda b, g: (b, 2 * n_groups + g))],
        out_specs=pl.BlockSpec((seq, wide), lambda b, g: (b, g)),
        out_shape=jax.ShapeDtypeStruct((t, A_WIDTH), BF16),
        scratch_shapes=[pltpu.VMEM((heads, seq // MOBA_BLOCK, A_HEAD_DIM), F32)],
        compiler_params=_params("parallel", "parallel"),
        name="moba_attention",
    )(z1, z1, z1)


def _swa_kernel(sinks_ref, q_ref, k_ref, v_ref, o_ref, kdup_ref, vdup_ref, *, scale):
    w = SWA_WINDOW
    n_blk = q_ref.shape[0] // w
    pair = pl.program_id(1)
    kv_odd = ((pair * 2) // (B_HEADS // B_KV_HEADS)) % 2
    lane = lax.broadcasted_iota(jnp.int32, (w, LANES), 1)
    low = lane < B_HEAD_DIM
    keep_orig = jnp.where(low, 0, 1) == kv_odd
    row = lax.broadcasted_iota(jnp.int32, (2 * w, 2 * w), 0)
    col = lax.broadcasted_iota(jnp.int32, (2 * w, 2 * w), 1)
    rel = (row & (w - 1)) + w - col
    band = (rel >= 0) & (rel < w)
    c_exp = scale * LOG2E
    sink_raw = jnp.where(lax.broadcasted_iota(jnp.int32, (2 * w, 1), 0) < w,
                         sinks_ref[2 * pair], sinks_ref[2 * pair + 1]) / scale

    def dup(x):
        return jnp.where(keep_orig, x, pltpu.roll(x, B_HEAD_DIM, axis=1)).astype(BF16)

    @pl.when(lax.rem(pair, (B_HEADS // B_KV_HEADS) // 2) == 0)
    def _():
        def stage(n, _):
            rows = pl.ds(pl.multiple_of(jnp.asarray(n, jnp.int32) * w, w), w)
            kdup_ref[rows, :] = dup(k_ref[rows, :])
            vdup_ref[rows, :] = dup(v_ref[rows, :])
            return 0

        lax.fori_loop(0, n_blk, stage, 0, unroll=4)

    def q_block(n, first):
        start = n * w
        cur = pl.ds(start if first else pl.multiple_of(start, w), w)
        q = q_ref[cur, :]
        zero = jnp.zeros_like(q)
        q2 = jnp.concatenate([jnp.where(low, q, zero), jnp.where(low, zero, q)], axis=0).astype(BF16)
        if first:
            k2 = jnp.concatenate([kdup_ref[cur, :], kdup_ref[cur, :]], axis=0)
            v2 = jnp.concatenate([vdup_ref[cur, :], vdup_ref[cur, :]], axis=0)
        else:
            both = pl.ds(pl.multiple_of(start - w, w), 2 * w)
            k2, v2 = kdup_ref[both, :], vdup_ref[both, :]
        s = lax.dot_general(q2, k2, (((1,), (1,)), ((), ())), preferred_element_type=F32)
        s = jnp.where(band & (col >= w) if first else band, s, NEG_INF)
        m = jnp.maximum(jnp.max(s, axis=1, keepdims=True), sink_raw)
        p = jnp.exp2((s - m) * c_exp)
        denom = jnp.sum(p, axis=1, keepdims=True) + jnp.exp2((sink_raw - m) * c_exp)
        o2 = jnp.dot(p.astype(BF16), v2, preferred_element_type=F32) / denom
        o_ref[cur, :] = jnp.where(low, o2[:w], o2[w:]).astype(o_ref.dtype)

    for u in range(SWA_UNROLL):
        q_block(u, u == 0)

    def group(g, _):
        g = jnp.asarray(g, jnp.int32)
        for u in range(SWA_UNROLL):
            q_block(g * SWA_UNROLL + u, False)
        return 0

    lax.fori_loop(1, n_blk // SWA_UNROLL, group, 0)


def _swa_attention(z1, sinks, *, batch, seq):
    t = z1.shape[0]
    pairs = B_HEADS // 2
    rep = B_HEADS // B_KV_HEADS
    assert (seq // SWA_WINDOW) % SWA_UNROLL == 0
    kv_group = lambda p: (2 * p) // rep // 2
    return pl.pallas_call(
        functools.partial(_swa_kernel, scale=B_HEAD_DIM ** -0.5),
        grid_spec=pltpu.PrefetchScalarGridSpec(
            num_scalar_prefetch=1,
            grid=(batch, pairs),
            in_specs=[pl.BlockSpec((seq, LANES), lambda b, p, sinks: (b, OFF_QB // LANES + p)),
                      pl.BlockSpec((seq, LANES), lambda b, p, sinks: (b, OFF_KB // LANES + kv_group(p))),
                      pl.BlockSpec((seq, LANES), lambda b, p, sinks: (b, OFF_VB // LANES + kv_group(p)))],
            out_specs=pl.BlockSpec((seq, LANES), lambda b, p, sinks: (b, p)),
            scratch_shapes=[pltpu.VMEM((seq, LANES), BF16), pltpu.VMEM((seq, LANES), BF16)],
        ),
        out_shape=jax.ShapeDtypeStruct((t, B_WIDTH), BF16),
        compiler_params=_params("arbitrary", "arbitrary"),
        name="swa_attention",
    )(sinks, z1, z1, z1)


def _gated_out_kernel(oa_ref, ob_ref, oc_ref, ga_ref, gb_ref, gc_ref, wa_ref, wb_ref, wc_ref, y_ref,
                      wa_bf, wb_bf, wc_bf):
    @pl.when(pl.program_id(1) == 0)
    def _():
        wa_bf[...] = wa_ref[...].astype(BF16)
        wb_bf[...] = wb_ref[...].astype(BF16)
        wc_bf[...] = wc_ref[...].astype(BF16)

    y = jax.nn.sigmoid(ga_ref[...]) * jnp.dot(oa_ref[...], wa_bf[...], preferred_element_type=F32)
    y += jax.nn.sigmoid(gb_ref[...]) * jnp.dot(ob_ref[...], wb_bf[...], preferred_element_type=F32)
    y += jax.nn.sigmoid(gc_ref[...]) * jnp.dot(oc_ref[...], wc_bf[...], preferred_element_type=F32)
    y_ref[...] = y.astype(y_ref.dtype)


def _gated_out(oa, ob, oc, gates, w_out_a, w_out_b, w_out_c, layer, *, tm=512, tn=512):
    t = oa.shape[0]
    d = w_out_a.shape[2]
    nb = d // tn
    o_spec = lambda width: pl.BlockSpec((tm, width), lambda j, i: (i, 0))
    g_spec = lambda which: pl.BlockSpec((tm, tn), lambda j, i: (i, which * nb + j))
    w_spec = lambda width: pl.BlockSpec((None, width, tn), lambda j, i: (layer, 0, j))
    return pl.pallas_call(
        _gated_out_kernel,
        grid=(nb, t // tm),
        in_specs=[o_spec(A_WIDTH), o_spec(B_WIDTH), o_spec(C_WIDTH), g_spec(0), g_spec(1), g_spec(2),
                  w_spec(A_WIDTH), w_spec(B_WIDTH), w_spec(C_WIDTH)],
        out_specs=pl.BlockSpec((tm, tn), lambda j, i: (i, j)),
        out_shape=jax.ShapeDtypeStruct((t, d), BF16),
        scratch_shapes=[pltpu.VMEM((A_WIDTH, tn), BF16), pltpu.VMEM((B_WIDTH, tn), BF16),
                        pltpu.VMEM((C_WIDTH, tn), BF16)],
        compiler_params=_params("arbitrary", "arbitrary"),
        name="gated_out",
    )(oa, ob, oc, gates, gates, gates, w_out_a, w_out_b, w_out_c)


HI16 = -65536


def _bf16_bits_hi(v):
    return lax.bitcast_convert_type(v.astype(BF16).astype(F32), jnp.int32)


def _router_kernel(h_ref, g_ref, w_ref, b_ref, ids_ref, wts_ref, packed_ref):
    x = h_ref[...]
    hn = x * lax.rsqrt(jnp.mean(x * x, axis=-1, keepdims=True) + NORM_EPS) * g_ref[...]
    half = hn.shape[1] // 2
    packed_ref[...] = lax.shift_right_logical(_bf16_bits_hi(hn[:, :half]), 16) | _bf16_bits_hi(hn[:, half:])
    logits = jnp.dot(hn, w_ref[...], precision=lax.Precision.HIGHEST, preferred_element_type=F32) + b_ref[...]
    lane = lax.broadcasted_iota(jnp.int32, logits.shape, 1)
    far = float(LANES)
    is_g = lane < N_GROUPS
    g_id = lane.astype(F32)
    gmax = jnp.max(jnp.where(is_g, logits, NEG_INF), axis=1, keepdims=True)
    gsel = jnp.min(jnp.where(is_g & (logits == gmax), g_id, far), axis=1, keepdims=True)
    p_g = 1.0 / jnp.sum(jnp.where(is_g, jnp.exp(logits - gmax), 0.0), axis=1, keepdims=True)
    e_lane = lane - N_GROUPS
    e_id = e_lane.astype(F32)
    e_group = jnp.right_shift(e_lane, 2).astype(F32)
    in_grp = (e_lane >= 0) & (e_lane < N_EXPERTS) & (e_group == gsel)
    emax = jnp.max(jnp.where(in_grp, logits, NEG_INF), axis=1, keepdims=True)
    ee = jnp.where(in_grp, jnp.exp(jnp.where(in_grp, logits, emax) - emax), 0.0)
    ep = ee / jnp.sum(ee, axis=1, keepdims=True)
    p1 = jnp.max(jnp.where(in_grp, ep, -1.0), axis=1, keepdims=True)
    i1 = jnp.min(jnp.where(in_grp & (ep == p1), e_id, far), axis=1, keepdims=True)
    rest = in_grp & (e_id != i1)
    p2 = jnp.max(jnp.where(rest, ep, -1.0), axis=1, keepdims=True)
    i2 = jnp.min(jnp.where(rest & (ep == p2), e_id, far), axis=1, keepdims=True)
    tot = p1 + p2
    ids_ref[...] = jnp.where(lane == 0, i1, jnp.where(lane == 1, i2, 0.0)).astype(jnp.int32)
    wts_ref[...] = jnp.where(lane == 0, p_g * p1 / tot, jnp.where(lane == 1, p_g * p2 / tot, 0.0))


def _router(h, g, w_r, b_r, layer, *, tm=256):
    t, d = h.shape
    out = pl.BlockSpec((tm, LANES), lambda i: (i, 0))
    return pl.pallas_call(
        _router_kernel,
        grid=(t // tm,),
        in_specs=[pl.BlockSpec((tm, d), lambda i: (i, 0)),
                  pl.BlockSpec((None, 1, d), lambda i: (layer, 0, 0)),
                  pl.BlockSpec((None, d, LANES), lambda i: (layer, 0, 0)),
                  pl.BlockSpec((None, 1, LANES), lambda i: (layer, 0, 0))],
        out_specs=[out, out, pl.BlockSpec((tm, d // 2), lambda i: (i, 0))],
        out_shape=[jax.ShapeDtypeStruct((t, LANES), jnp.int32), jax.ShapeDtypeStruct((t, LANES), F32),
                   jax.ShapeDtypeStruct((t, d // 2), jnp.int32)],
        compiler_params=_params("parallel"),
        name="moe_router",
    )(h, g, w_r, b_r)


def _row_copy(src_hbm, tok, dst_ref, r, sem):
    return pltpu.make_async_copy(src_hbm.at[pl.ds(tok, 1), :], dst_ref.at[pl.ds(r, 1), :], sem)


GATHER_UNROLL = 8
NORM_CHUNK = 16


def _dispatch_kernel(tok_ref, nvalid_ref, h_hbm, o_ref, rows_ref, sem):
    i = pl.program_id(0)
    n_valid = nvalid_ref[0]
    n_rows = rows_ref.shape[1]
    slot = lax.rem(i, 2)

    def fetch(blk, slot):
        base = blk * n_rows

        def issue(g, _):
            for u in range(GATHER_UNROLL):
                r = g * GATHER_UNROLL + u
                _row_copy(h_hbm, tok_ref[base + r], rows_ref.at[slot], r, sem.at[slot]).start(priority=u % 2)
            return 0

        lax.fori_loop(0, n_rows // GATHER_UNROLL, issue, 0)

    @pl.when(i == 0)
    def _():
        fetch(0, 0)

    @pl.when(i + 1 < n_valid)
    def _():
        fetch(i + 1, 1 - slot)

    @pl.when(i < n_valid)
    def _():
        def drain(r, _):
            _row_copy(h_hbm, 0, rows_ref.at[slot], r, sem.at[slot]).wait()
            return 0

        lax.fori_loop(0, n_rows, drain, 0, unroll=GATHER_UNROLL)

        half = rows_ref.shape[2]

        def unpack(c, _):
            rows = pl.ds(pl.multiple_of(jnp.asarray(c, jnp.int32) * NORM_CHUNK, NORM_CHUNK), NORM_CHUNK)
            word = rows_ref[slot, rows, :]
            o_ref[rows, :half] = lax.bitcast_convert_type(lax.shift_left(word, 16), F32).astype(o_ref.dtype)
            o_ref[rows, half:] = lax.bitcast_convert_type(word & HI16, F32).astype(o_ref.dtype)
            return 0

        lax.fori_loop(0, n_rows // NORM_CHUNK, unpack, 0, unroll=4)


def _dispatch(packed, buf_tok, nvalid, *, n_blocks):
    t, half = packed.shape
    blk = lambda i, tok, nv: (jnp.minimum(i, nv[0] - 1), 0)
    return pl.pallas_call(
        _dispatch_kernel,
        grid_spec=pltpu.PrefetchScalarGridSpec(
            num_scalar_prefetch=2,
            grid=(n_blocks,),
            in_specs=[pl.BlockSpec(memory_space=pl.ANY)],
            out_specs=pl.BlockSpec((MOE_ROWS, 2 * half), blk),
            scratch_shapes=[pltpu.VMEM((2, MOE_ROWS, half), jnp.int32), pltpu.SemaphoreType.DMA((2,))],
        ),
        out_shape=jax.ShapeDtypeStruct((n_blocks * MOE_ROWS, 2 * half), BF16),
        compiler_params=_params("arbitrary"),
        name="moe_dispatch",
    )(buf_tok, nvalid, packed)


def _run_weights(be_ref, first_ref, next_ref, i, pass_id, n_passes, n_blocks, copies, stage):
    @pl.when((pass_id == 0) & (i == 0))
    def _():
        for c in copies(be_ref[0], 0):
            c.start()

    @pl.when(first_ref[i] == 1)
    def _():
        for c in copies(be_ref[i], pass_id):
            c.wait()
        stage()
        nxt = next_ref[i]
        same_pass = nxt < n_blocks
        e_next = jnp.where(same_pass, be_ref[jnp.minimum(nxt, n_blocks - 1)], be_ref[0])
        pass_next = jnp.where(same_pass, pass_id, pass_id + 1)

        @pl.when(same_pass | (pass_id + 1 < n_passes))
        def _():
            for c in copies(e_next, pass_next):
                c.start()


def _expert_up_kernel(be_ref, nvalid_ref, first_ref, next_ref, x_ref, wg_hbm, wu_hbm, o_ref,
                      wg_land, wu_land, wg_bf, wu_bf, sem, *, layer):
    f, i = pl.program_id(0), pl.program_id(1)
    tf = wg_land.shape[1]

    def copies(e, ff_tile):
        cols = pl.ds(pl.multiple_of(ff_tile * tf, tf), tf)
        return (pltpu.make_async_copy(wg_hbm.at[layer, e, :, cols], wg_land, sem.at[0]),
                pltpu.make_async_copy(wu_hbm.at[layer, e, :, cols], wu_land, sem.at[1]))

    def stage():
        wg_bf[...] = wg_land[...].astype(BF16)
        wu_bf[...] = wu_land[...].astype(BF16)

    _run_weights(be_ref, first_ref, next_ref, i, f, pl.num_programs(0), pl.num_programs(1), copies, stage)

    @pl.when(i < nvalid_ref[0])
    def _():
        x = x_ref[...]
        hg = jnp.dot(x, wg_bf[...], preferred_element_type=F32)
        hu = jnp.dot(x, wu_bf[...], preferred_element_type=F32)
        o_ref[...] = (jax.nn.silu(hg) * hu).astype(o_ref.dtype)


def _expert_up(xb, w_gate, w_up, layer, runs, *, n_blocks):
    d = xb.shape[1]
    ff = w_gate.shape[3]
    tf = MOE_FF_TILE
    row_blk = lambda f, i, be, nv, first, nxt: (jnp.minimum(i, nv[0] - 1), 0)
    return pl.pallas_call(
        functools.partial(_expert_up_kernel, layer=layer),
        grid_spec=pltpu.PrefetchScalarGridSpec(
            num_scalar_prefetch=4,
            grid=(ff // tf, n_blocks),
            in_specs=[pl.BlockSpec((MOE_ROWS, d), row_blk), pl.BlockSpec(memory_space=pl.ANY),
                      pl.BlockSpec(memory_space=pl.ANY)],
            out_specs=pl.BlockSpec((MOE_ROWS, tf),
                                   lambda f, i, be, nv, first, nxt: (jnp.minimum(i, nv[0] - 1), f)),
            scratch_shapes=[pltpu.VMEM((d, tf), F32), pltpu.VMEM((d, tf), F32),
                            pltpu.VMEM((d, tf), BF16), pltpu.VMEM((d, tf), BF16),
                            pltpu.SemaphoreType.DMA((2,))],
        ),
        out_shape=jax.ShapeDtypeStruct((n_blocks * MOE_ROWS, ff), BF16),
        compiler_params=_params("arbitrary", "arbitrary"),
        name="moe_expert_up",
    )(*runs, xb, w_gate, w_up)


def _expert_down_kernel(be_ref, nvalid_ref, first_ref, next_ref, a_ref, wd_hbm, o_ref, wd_land, wd_bf, sem,
                        *, layer):
    i = pl.program_id(0)

    def copies(e, _):
        return (pltpu.make_async_copy(wd_hbm.at[layer, e], wd_land, sem.at[0]),)

    def stage():
        wd_bf[...] = wd_land[...].astype(BF16)

    _run_weights(be_ref, first_ref, next_ref, i, 0, 1, pl.num_programs(0), copies, stage)

    @pl.when(i < nvalid_ref[0])
    def _():
        o_ref[...] = jnp.dot(a_ref[...], wd_bf[...], preferred_element_type=F32)


def _expert_down(act, w_down, layer, runs, *, n_blocks):
    ff, d = w_down.shape[2], w_down.shape[3]
    row_blk = lambda i, be, nv, first, nxt: (jnp.minimum(i, nv[0] - 1), 0)
    return pl.pallas_call(
        functools.partial(_expert_down_kernel, layer=layer),
        grid_spec=pltpu.PrefetchScalarGridSpec(
            num_scalar_prefetch=4,
            grid=(n_blocks,),
            in_specs=[pl.BlockSpec((MOE_ROWS, ff), row_blk), pl.BlockSpec(memory_space=pl.ANY)],
            out_specs=pl.BlockSpec((MOE_ROWS, d), row_blk),
            scratch_shapes=[pltpu.VMEM((ff, d), F32), pltpu.VMEM((ff, d), BF16), pltpu.SemaphoreType.DMA((1,))],
        ),
        out_shape=jax.ShapeDtypeStruct((n_blocks * MOE_ROWS, d), F32),
        compiler_params=_params("arbitrary"),
        name="moe_expert_down",
    )(*runs, act, w_down)


def _combine_kernel(pos_ref, h_ref, wts_ref, g_ref, y_hbm, *refs, emit_h):
    o_ref = refs[0] if emit_h else None
    n_ref, y0_ref, y1_ref, sem = refs[int(emit_h):]
    i = pl.program_id(0)
    tm = h_ref.shape[0]
    slot = lax.rem(i, 2)

    def fetch(blk, slot):
        base = blk * tm * EXPERT_TOPK

        def issue(g, _):
            for u in range(GATHER_UNROLL):
                r = g * GATHER_UNROLL + u
                _row_copy(y_hbm, pos_ref[base + EXPERT_TOPK * r], y0_ref.at[slot], r, sem.at[slot]).start(priority=0)
                _row_copy(y_hbm, pos_ref[base + EXPERT_TOPK * r + 1], y1_ref.at[slot], r,
                          sem.at[slot]).start(priority=1)
            return 0

        lax.fori_loop(0, tm // GATHER_UNROLL, issue, 0)

    @pl.when(i == 0)
    def _():
        fetch(0, 0)

    @pl.when(i + 1 < pl.num_programs(0))
    def _():
        fetch(i + 1, 1 - slot)

    def drain(r, _):
        _row_copy(y_hbm, 0, y0_ref.at[slot], r, sem.at[slot]).wait()
        _row_copy(y_hbm, 0, y1_ref.at[slot], r, sem.at[slot]).wait()
        return 0

    lax.fori_loop(0, tm, drain, 0, unroll=GATHER_UNROLL)

    def rows_chunk(c, _):
        rows = pl.ds(pl.multiple_of(jnp.asarray(c, jnp.int32) * NORM_CHUNK, NORM_CHUNK), NORM_CHUNK)
        w = wts_ref[rows, :]
        hv = h_ref[rows, :] + w[:, 0:1] * y0_ref[slot, rows, :] + w[:, 1:2] * y1_ref[slot, rows, :]
        if emit_h:
            o_ref[rows, :] = hv
        hn = hv * lax.rsqrt(jnp.mean(hv * hv, axis=-1, keepdims=True) + NORM_EPS) * g_ref[...]
        n_ref[rows, :] = hn.astype(n_ref.dtype)
        return 0

    lax.fori_loop(0, tm // NORM_CHUNK, rows_chunk, 0, unroll=4)


def _combine(h, wts, y, pos, g, *, norm_dtype, emit_h, tm=128):
    t, d = h.shape
    row = pl.BlockSpec((tm, d), lambda i, pos: (i, 0))
    outs = pl.pallas_call(
        functools.partial(_combine_kernel, emit_h=emit_h),
        grid_spec=pltpu.PrefetchScalarGridSpec(
            num_scalar_prefetch=1,
            grid=(t // tm,),
            in_specs=[row, pl.BlockSpec((tm, LANES), lambda i, pos: (i, 0)),
                      pl.BlockSpec((1, d), lambda i, pos: (0, 0)), pl.BlockSpec(memory_space=pl.ANY)],
            out_specs=[row] * (1 + emit_h),
            scratch_shapes=[pltpu.VMEM((2, tm, d), F32), pltpu.VMEM((2, tm, d), F32),
                            pltpu.SemaphoreType.DMA((2,))],
        ),
        out_shape=([jax.ShapeDtypeStruct((t, d), F32)] if emit_h else [])
        + [jax.ShapeDtypeStruct((t, d), norm_dtype)],
        compiler_params=_params("arbitrary"),
        name="moe_combine",
    )(pos, h, wts, g.reshape(1, d), y)
    return (outs[0], outs[1]) if emit_h else (None, outs[0])


def _hier_moe(h, ffn_norm_g, w_r, b_r, w_gate, w_up, w_down, layer, next_norm_g, *, norm_dtype, emit_h):
    t, d = h.shape
    tk = t * EXPERT_TOPK
    n_blocks = tk // MOE_ROWS + N_EXPERTS
    ids, wts, hn_packed = _router(h, ffn_norm_g, w_r, b_r, layer)
    flat_e = ids[:, :EXPERT_TOPK].reshape(tk)
    onehot = (flat_e[:, None] == jnp.arange(N_EXPERTS, dtype=jnp.int32)[None, :]).astype(jnp.int32)
    csum = jnp.cumsum(onehot, axis=0)
    rank = jnp.sum(onehot * csum, axis=1) - 1
    counts = csum[-1]
    padded = (counts + MOE_ROWS - 1) // MOE_ROWS * MOE_ROWS
    pad_end = jnp.cumsum(padded)
    pad_start = pad_end - padded
    dest = (pad_start[flat_e] + rank).astype(jnp.int32)
    flat_tok = jnp.arange(tk, dtype=jnp.int32) // EXPERT_TOPK
    buf_tok = jnp.zeros((n_blocks * MOE_ROWS,), jnp.int32).at[dest].set(flat_tok)
    nvalid = (pad_end[-1] // MOE_ROWS).astype(jnp.int32).reshape(1)
    blk_start = jnp.minimum(jnp.arange(n_blocks, dtype=jnp.int32), nvalid[0] - 1) * MOE_ROWS
    block_expert = jnp.minimum(jnp.searchsorted(pad_end, blk_start, side='right'), N_EXPERTS - 1).astype(jnp.int32)

    blk = jnp.arange(n_blocks, dtype=jnp.int32)
    prev_expert = jnp.concatenate([jnp.full((1,), -1, jnp.int32), block_expert[:-1]])
    run_first = (blk < nvalid[0]) & (block_expert != prev_expert)
    first_at_or_after = lax.cummin(jnp.where(run_first, blk, n_blocks), reverse=True)
    run_next = jnp.concatenate([first_at_or_after[1:], jnp.full((1,), n_blocks, jnp.int32)])
    runs = (block_expert, nvalid, run_first.astype(jnp.int32), run_next)

    xb = _dispatch(hn_packed, buf_tok, nvalid, n_blocks=n_blocks)
    act = _expert_up(xb, w_gate, w_up, layer, runs, n_blocks=n_blocks)
    y = _expert_down(act, w_down, layer, runs, n_blocks=n_blocks)
    return _combine(h, wts, y, dest, next_norm_g, norm_dtype=norm_dtype, emit_h=emit_h)


ROPE_A, ROPE_B, ROPE_PE = 0, 1, 2


def _in_proj_rope_table(col):
    if OFF_QA <= col < OFF_VA:
        return ROPE_A
    if OFF_QB <= col < OFF_VB:
        return ROPE_B
    if col == OFF_KPE:
        return ROPE_PE
    return None


def _mla_q_rope_table(col):
    return 0 if (col // LANES) % 2 == 1 else None


def kernel(x, positions, attn_norm_g, w_in, q_norm_g, kv_norm_g, wq_b, wkv_b, sinks, w_out_a, w_out_b,
           w_out_c, w_o, ffn_norm_g, w_group, b_group, w_expert, b_expert, w_gate, w_up, w_down,
           final_norm_g):
    batch, seq, d = x.shape
    depth = w_in.shape[0]
    t = batch * seq
    h = x.reshape(t, d)
    pos = positions.reshape(t).astype(F32)

    w_in_t = jnp.swapaxes(w_in, 1, 2)
    q_head = C_NOPE_DIM + C_ROPE_DIM
    wq_pad = jnp.pad(wq_b.reshape(depth, C_Q_RANK, C_HEADS, q_head),
                     ((0, 0), (0, 0), (0, 0), (0, 2 * LANES - q_head))
                     ).reshape(depth, C_Q_RANK, C_HEADS * 2 * LANES).astype(BF16)
    w_r = jnp.concatenate([w_group, w_expert,
                           jnp.zeros((depth, d, LANES - N_GROUPS - N_EXPERTS), F32)], axis=2)
    b_r = jnp.concatenate([b_group, b_expert,
                           jnp.zeros((depth, LANES - N_GROUPS - N_EXPERTS), F32)], axis=1).reshape(depth, 1, LANES)
    attn_g = attn_norm_g.reshape(depth, 1, d)
    ffn_g = ffn_norm_g.reshape(depth, 1, d)

    c_a, s_a = _rope_tables(pos, A_HEAD_DIM)
    c_b, s_b = _rope_tables(pos, B_HEAD_DIM)
    c_pe, s_pe = _rope_tables(pos, C_ROPE_DIM, keep_lanes=C_ROPE_DIM)
    rope_tables = ((c_a, s_a, A_HEAD_DIM // 2), (c_b, s_b, B_HEAD_DIM // 2), (c_pe, s_pe, C_ROPE_DIM // 2))

    hn = _rmsnorm(h, attn_norm_g[0])
    for l in range(depth):
        z1 = _matmul(hn, w_in_t, l, n_out=Z1_WIDTH, w_is_nk=True, rope_tables=rope_tables,
                     rope_group_table=_in_proj_rope_table, name="in_proj")
        gates = _matmul(hn, w_in_t, l, n_out=3 * d, col_off=OFF_GATES, w_is_nk=True, name="gate_proj")
        o_a = _moba_attention(z1, batch=batch, seq=seq)
        o_b = _swa_attention(z1, sinks[l], batch=batch, seq=seq)
        cq_n = _rmsnorm(z1, q_norm_g[l], col_off=OFF_CQ, width=C_Q_RANK, piece=512)
        ckv_n = _rmsnorm(z1, kv_norm_g[l], col_off=OFF_CKV, width=C_KV_RANK, piece=512)
        q_c = _matmul(cq_n, wq_pad, l, n_out=C_HEADS * 2 * LANES, tn=1024, out_dtype=BF16,
                      rope_tables=rope_tables[ROPE_PE:], rope_group_table=_mla_q_rope_table, name="mla_q_proj")
        kv_c = _matmul(ckv_n, wkv_b, l, n_out=C_HEADS * (C_NOPE_DIM + C_V_DIM), tn=1024, out_dtype=BF16,
                       name="mla_kv_proj")
        o_c = _mla_attention(q_c, kv_c, z1, batch=batch, seq=seq)
        y = _gated_out(o_a, o_b, o_c, gates, w_out_a, w_out_b, w_out_c, l)
        h = _matmul(y, w_o, l, n_out=d, residual=h, name="out_proj")
        last = l == depth - 1
        h, hn = _hier_moe(h, ffn_g, w_r, b_r, w_gate, w_up, w_down, l,
                          final_norm_g if last else attn_norm_g[l + 1],
                          norm_dtype=F32 if last else BF16, emit_h=not last)
    return hn.reshape(batch, seq, d)
```

```python
import functools

import jax
import jax.numpy as jnp
from jax import lax
from jax.experimental import pallas as pl
from jax.experimental.pallas import tpu as pltpu

F32 = jnp.float32
BF16 = jnp.bfloat16

ROPE_THETA = 10000.0
NORM_EPS = 1e-6
NEG_INF = -1e30
PICKED = -3e38
LOG2E = 1.4426950408889634

A_HEADS, A_HEAD_DIM = 16, 128
MOBA_BLOCK, MOBA_TOPK = 256, 3
B_HEADS, B_KV_HEADS, B_HEAD_DIM, SWA_WINDOW = 32, 4, 64, 128
C_HEADS, C_Q_RANK, C_KV_RANK, C_NOPE_DIM, C_ROPE_DIM, C_V_DIM = 16, 1024, 512, 128, 64, 128
N_GROUPS, EXPERTS_PER_GROUP, EXPERT_TOPK, D_FF_EXPERT = 8, 4, 2, 768
N_EXPERTS = N_GROUPS * EXPERTS_PER_GROUP
assert EXPERTS_PER_GROUP == 4

LANES = 128
VMEM_LIMIT_BYTES = 58 * 1024 * 1024

A_WIDTH = A_HEADS * A_HEAD_DIM
B_WIDTH = B_HEADS * B_HEAD_DIM
B_KV_WIDTH = B_KV_HEADS * B_HEAD_DIM
C_WIDTH = C_HEADS * C_V_DIM
OFF_QA = 0
OFF_KA = OFF_QA + A_WIDTH
OFF_VA = OFF_KA + A_WIDTH
OFF_QB = OFF_VA + A_WIDTH
OFF_KB = OFF_QB + B_WIDTH
OFF_VB = OFF_KB + B_KV_WIDTH
OFF_CQ = OFF_VB + B_KV_WIDTH
OFF_CKV = OFF_CQ + C_Q_RANK
OFF_KPE = OFF_CKV + C_KV_RANK
OFF_GATES = OFF_KPE + C_ROPE_DIM
Z1_WIDTH = 10752

MOE_ROWS = 256
MOE_FF_TILE = 256
SWA_UNROLL = 8


def _params(*sem):
    return pltpu.CompilerParams(dimension_semantics=sem, vmem_limit_bytes=VMEM_LIMIT_BYTES)


def _rmsnorm_kernel(*refs, n_pieces, width):
    x_refs, g_ref, o_ref = refs[:n_pieces], refs[n_pieces], refs[n_pieces + 1]
    xs = [r[...].astype(F32) for r in x_refs]
    ss = sum(jnp.sum(x * x, axis=-1, keepdims=True) for x in xs)
    inv = lax.rsqrt(ss * (1.0 / width) + NORM_EPS)
    pw = xs[0].shape[1]
    for p, x in enumerate(xs):
        o_ref[:, p * pw:(p + 1) * pw] = (x * inv * g_ref[:, p * pw:(p + 1) * pw]).astype(o_ref.dtype)


def _rmsnorm(x, g, *, col_off=0, width=None, piece=None, tm=256, out_dtype=BF16):
    t = x.shape[0]
    width = width or x.shape[1]
    piece = piece or width
    n_pieces = width // piece
    off = col_off // piece
    assert col_off % piece == 0 and width % piece == 0
    in_specs = [pl.BlockSpec((tm, piece), functools.partial(lambda i, p: (i, off + p), p=p))
                for p in range(n_pieces)]
    in_specs.append(pl.BlockSpec((1, width), lambda i: (0, 0)))
    return pl.pallas_call(
        functools.partial(_rmsnorm_kernel, n_pieces=n_pieces, width=width),
        grid=(t // tm,),
        in_specs=in_specs,
        out_specs=pl.BlockSpec((tm, width), lambda i: (i, 0)),
        out_shape=jax.ShapeDtypeStruct((t, width), out_dtype),
        compiler_params=_params("parallel"),
        name="rmsnorm",
    )(*([x] * n_pieces), g.reshape(1, width))


def _stage_weight(w_ref, wbf_ref, w_is_nk):
    if not w_is_nk:
        wbf_ref[...] = w_ref[...].astype(BF16)
        return
    _, tn, k = w_ref.shape
    for c in range(k // tn):
        wbf_ref[c * tn:(c + 1) * tn, :] = w_ref[0, :, c * tn:(c + 1) * tn].T.astype(BF16)


def _swap_halves(x, half):
    if half == 64:
        return pltpu.roll(x, 64, axis=1)
    lane = lax.broadcasted_iota(jnp.int32, x.shape, 1)
    return jnp.where((lane & half) == 0, pltpu.roll(x, LANES - half, axis=1), pltpu.roll(x, half, axis=1))


def _matmul_kernel(*refs, stage_w, w_is_nk, has_res, rope_modes, rope_halves):
    x_ref, w_ref = refs[0], refs[1]
    n_in = 2 + has_res + 2 * len(rope_halves)
    r_ref = refs[2] if has_res else None
    table_refs = refs[2 + has_res:n_in]
    o_ref = refs[n_in]
    if stage_w:
        wbf_ref = refs[n_in + 1]

        @pl.when(pl.program_id(1) == 0)
        def _():
            _stage_weight(w_ref, wbf_ref, w_is_nk)

        w_src = wbf_ref
    else:
        w_src = w_ref

    def product():
        acc = jnp.dot(x_ref[...], w_src[...], preferred_element_type=F32)
        return acc + r_ref[...] if has_res else acc

    if rope_modes is None:
        o_ref[...] = product().astype(o_ref.dtype)
        return
    j = pl.program_id(0)
    for pattern in sorted(set(rope_modes), key=str):
        blocks = [b for b, p in enumerate(rope_modes) if p == pattern]

        @pl.when(functools.reduce(jnp.logical_or, [j == b for b in blocks]))
        def _(pattern=pattern):
            acc = product()
            for g, table in enumerate(pattern):
                xg = acc[:, g * LANES:(g + 1) * LANES]
                if table is not None:
                    c_ref, s_ref = table_refs[2 * table], table_refs[2 * table + 1]
                    xg = xg * c_ref[...] + _swap_halves(xg, rope_halves[table]) * s_ref[...]
                o_ref[:, g * LANES:(g + 1) * LANES] = xg.astype(o_ref.dtype)


def _matmul(x, w, layer, *, n_out, col_off=0, w_is_nk=False, tm=1024, tn=512, out_dtype=F32, residual=None,
            rope_tables=(), rope_group_table=None, name="matmul"):
    m, k = x.shape
    assert w.shape[2 if w_is_nk else 1] == k and m % tm == 0 and n_out % tn == 0
    stage_w = w_is_nk or w.dtype != BF16
    rope_modes = None
    if rope_group_table is not None:
        rope_modes = tuple(tuple(rope_group_table(b * tn + g * LANES) for g in range(tn // LANES))
                           for b in range(n_out // tn))
    if w_is_nk:
        assert k % tn == 0 and col_off % 8 == 0
        w_spec = pl.BlockSpec((pl.Element(1), pl.Element(tn), pl.Element(k)),
                              lambda j, i: (layer, pl.multiple_of(col_off + j * tn, 8), 0))
    else:
        assert col_off % tn == 0
        w_spec = pl.BlockSpec((None, k, tn), lambda j, i: (layer, 0, j + col_off // tn))
    in_specs = [pl.BlockSpec((tm, k), lambda j, i: (i, 0)), w_spec]
    args = [x, w]
    if residual is not None:
        in_specs.append(pl.BlockSpec((tm, tn), lambda j, i: (i, j)))
        args.append(residual)
    for c, s, _ in rope_tables:
        in_specs += [pl.BlockSpec((tm, LANES), lambda j, i: (i, 0))] * 2
        args += [c, s]
    return pl.pallas_call(
        functools.partial(_matmul_kernel, stage_w=stage_w, w_is_nk=w_is_nk, has_res=residual is not None,
                          rope_modes=rope_modes, rope_halves=tuple(half for _, _, half in rope_tables)),
        grid=(n_out // tn, m // tm),
        in_specs=in_specs,
        out_specs=pl.BlockSpec((tm, tn), lambda j, i: (i, j)),
        out_shape=jax.ShapeDtypeStruct((m, n_out), out_dtype),
        scratch_shapes=[pltpu.VMEM((k, tn), BF16)] if stage_w else [],
        compiler_params=_params("arbitrary", "arbitrary"),
        name=name,
    )(*args)


def _rope_tables_kernel(pos_ref, invf_ref, sign_ref, keep_ref, c_ref, s_ref):
    ang = pos_ref[...] * invf_ref[...]
    c_ref[...] = jnp.cos(ang) * keep_ref[...]
    s_ref[...] = jnp.sin(ang) * sign_ref[...]


def _rope_tables(pos, dim, *, keep_lanes=LANES):
    t = pos.shape[0]
    half = dim // 2
    lane = jnp.arange(LANES)
    invf = (ROPE_THETA ** (-(2.0 * (lane % half)).astype(F32) / dim)).reshape(1, LANES)
    keep = (lane < keep_lanes).astype(F32).reshape(1, LANES)
    sign = jnp.where((lane % dim) < half, -1.0, 1.0).astype(F32).reshape(1, LANES) * keep
    tm = 1024
    row = pl.BlockSpec((1, LANES), lambda i: (0, 0))
    out = pl.BlockSpec((tm, LANES), lambda i: (i, 0))
    return pl.pallas_call(
        _rope_tables_kernel,
        grid=(t // tm,),
        in_specs=[pl.BlockSpec((tm, 1), lambda i: (i, 0)), row, row, row],
        out_specs=[out, out],
        out_shape=[jax.ShapeDtypeStruct((t, LANES), F32)] * 2,
        compiler_params=_params("parallel"),
        name="rope_tables",
    )(pos.reshape(t, 1), invf, sign, keep)


def _online_block(q, k, v, s_mask, carry, c_exp):
    m, l, acc = carry
    s = s_mask(lax.dot_general(q, k, (((1,), (1,)), ((), ())), preferred_element_type=F32))
    m_new = jnp.maximum(m, jnp.max(s, axis=1, keepdims=True))
    exp2_scaled = jnp.exp2 if c_exp is None else (lambda x: jnp.exp2(x * c_exp))
    alpha = exp2_scaled(m - m_new)
    p = exp2_scaled(s - m_new)
    l = alpha * l + jnp.sum(p, axis=1, keepdims=True)
    acc = alpha * acc + jnp.dot(p.astype(BF16), v, preferred_element_type=F32)
    return m_new, l, acc


def _causal_attend(qi, qs, load_kvs, tq, dv, scale, diag_mask, past_mask):
    heads = range(len(qs))
    c_exp = None if scale is None else scale * LOG2E
    init = (jnp.full((tq, 1), NEG_INF, F32), jnp.zeros((tq, 1), F32), jnp.zeros((tq, dv), F32))
    kvs = load_kvs(qi)
    carries = tuple(_online_block(qs[h], *kvs[h], functools.partial(diag_mask, h), init, c_exp) for h in heads)

    def past(j, carries):
        j = jnp.asarray(j, jnp.int32)
        kvs = load_kvs(j)
        return tuple(_online_block(qs[h], *kvs[h], functools.partial(past_mask, h, j), carries[h], c_exp)
                     for h in heads)

    carries = lax.fori_loop(0, qi, past, carries)
    return [acc / l for _, l, acc in carries]


def _mla_kernel(q_ref, kv_ref, kpe_ref, o_ref, *, tq, heads, scale):
    n_q = q_ref.shape[0] // tq
    qw, dv = 2 * LANES, C_V_DIM
    row = lax.broadcasted_iota(jnp.int32, (tq, tq), 0)
    col = lax.broadcasted_iota(jnp.int32, (tq, tq), 1)
    causal = col <= row

    def load_kvs(j):
        rows = pl.ds(pl.multiple_of(j * tq, tq), tq)
        kpe = kpe_ref[rows, :].astype(BF16)
        return [(jnp.concatenate([kv_ref[rows, h * qw:h * qw + C_NOPE_DIM], kpe], axis=1),
                 kv_ref[rows, h * qw + C_NOPE_DIM:(h + 1) * qw]) for h in range(heads)]

    def q_tile(qi, _):
        qi = jnp.asarray(qi, jnp.int32)
        rows = pl.ds(pl.multiple_of(qi * tq, tq), tq)
        qs = [q_ref[rows, h * qw:(h + 1) * qw] for h in range(heads)]
        outs = _causal_attend(qi, qs, load_kvs, tq, dv, scale,
                              lambda h, s: jnp.where(causal, s, NEG_INF), lambda h, j, s: s)
        for h in range(heads):
            o_ref[rows, h * dv:(h + 1) * dv] = outs[h].astype(o_ref.dtype)
        return 0

    lax.fori_loop(0, n_q, q_tile, 0)


MLA_Q_FACTOR = (C_NOPE_DIM + C_ROPE_DIM) ** -0.5 * LOG2E


def _mla_attention(q, kv, z1, *, batch, seq, tq=1024, heads=2):
    t = q.shape[0]
    wide = pl.BlockSpec((seq, heads * 2 * LANES), lambda b, g: (b, g))
    return pl.pallas_call(
        functools.partial(_mla_kernel, tq=tq, heads=heads, scale=None),
        grid=(batch, C_HEADS // heads),
        in_specs=[wide, wide, pl.BlockSpec((seq, LANES), lambda b, g: (b, OFF_KPE // LANES))],
        out_specs=pl.BlockSpec((seq, heads * C_V_DIM), lambda b, g: (b, g)),
        out_shape=jax.ShapeDtypeStruct((t, C_WIDTH), BF16),
        compiler_params=_params("parallel", "parallel"),
        name="mla_attention",
    )(q, kv, z1)


def _moba_kernel(q_ref, k_ref, v_ref, o_ref, kmean_ref, *, tq, heads, scale):
    bs, d = MOBA_BLOCK, A_HEAD_DIM
    n_blk = q_ref.shape[0] // bs
    per_tile = tq // bs
    for h in range(heads):
        for n in range(n_blk):
            kmean_ref[h, n:n + 1, :] = jnp.mean(k_ref[n * bs:(n + 1) * bs, h * d:(h + 1) * d], axis=0, keepdims=True)
    blk_i = lax.broadcasted_iota(jnp.int32, (n_blk, tq), 0)
    blk = blk_i.astype(F32)
    bit_of_blk = lax.shift_left(jnp.ones_like(blk_i), blk_i).astype(F32)
    lane_q = lax.broadcasted_iota(jnp.int32, (1, tq), 1)
    lane_sub = sum(((lane_q >= c * bs).astype(F32) for c in range(1, per_tile)), jnp.zeros((1, tq), F32))
    row1 = lax.broadcasted_iota(jnp.int32, (tq, 1), 0)
    row_sub = sum(((row1 >= c * bs).astype(F32) for c in range(1, per_tile)), jnp.zeros((tq, 1), F32))
    col_local = lax.broadcasted_iota(jnp.int32, (tq, bs), 1)

    def load_kvs(j):
        rows = pl.ds(pl.multiple_of(j * tq, tq), tq)
        return [(k_ref[rows, h * d:(h + 1) * d].astype(BF16), v_ref[rows, h * d:(h + 1) * d].astype(BF16))
                for h in range(heads)]

    def q_tile(qi, _):
        qi = jnp.asarray(qi, jnp.int32)
        rows = pl.ds(pl.multiple_of(qi * tq, tq), tq)
        first = qi * per_tile
        own = first.astype(F32) + lane_sub
        qs, sels = [], []
        for h in range(heads):
            qf = q_ref[rows, h * d:(h + 1) * d]
            gate = lax.dot_general(kmean_ref[h], qf, (((1,), (1,)), ((), ())),
                                   precision=lax.Precision.HIGHEST, preferred_element_type=F32)
            gate = jnp.where(blk < own, gate, NEG_INF)
            sel = jnp.zeros((n_blk, tq), F32)
            for _ in range(MOBA_TOPK):
                best = jnp.max(gate, axis=0, keepdims=True)
                idx = jnp.min(jnp.where(gate == best, blk, float(n_blk)), axis=0, keepdims=True)
                pick = blk == idx
                sel = jnp.where(pick & (blk < own), 1.0, sel)
                gate = jnp.where(pick, PICKED, gate)
            qs.append((qf * (scale * LOG2E)).astype(BF16))
            code = jnp.sum(sel * bit_of_blk, axis=0, keepdims=True)
            code_col = jnp.max(jnp.broadcast_to(code, (LANES, tq)).T, axis=1, keepdims=True)
            sels.append(code_col.astype(jnp.int32))

        def chosen(h, b):
            return (lax.shift_right_logical(sels[h], jnp.broadcast_to(b, sels[h].shape)) & 1) == 1

        def by_block(s, mask_block):
            parts = [mask_block(c, s[:, c * bs:(c + 1) * bs]) for c in range(per_tile)]
            return parts[0] if per_tile == 1 else jnp.concatenate(parts, axis=1)

        def diag_mask(h, s):
            def mask_block(c, sc):
                limit = jnp.where(row_sub == c, row1 - c * bs + 1,
                                  jnp.where((row_sub > c) & chosen(h, first + c), bs, 0))
                return jnp.where(col_local < limit, sc, NEG_INF)
            return by_block(s, mask_block)

        def past_mask(h, j, s):
            jb = j * per_tile
            return by_block(s, lambda c, sc: jnp.where(chosen(h, jb + c), sc, NEG_INF))

        outs = _causal_attend(qi, qs, load_kvs, tq, d, None, diag_mask, past_mask)
        for h in range(heads):
            o_ref[rows, h * d:(h + 1) * d] = outs[h].astype(o_ref.dtype)
        return 0

    lax.fori_loop(0, q_ref.shape[0] // tq, q_tile, 0)


def _moba_attention(z1, *, batch, seq, tq=1024, heads=2):
    t = z1.shape[0]
    assert seq % tq == 0 and tq % MOBA_BLOCK == 0
    n_groups = A_HEADS // heads
    wide = heads * A_HEAD_DIM
    assert OFF_QA == 0 and OFF_KA == A_WIDTH and OFF_VA == 2 * A_WIDTH
    return pl.pallas_call(
        functools.partial(_moba_kernel, tq=tq, heads=heads, scale=A_HEAD_DIM ** -0.5),
        grid=(batch, n_groups),
        in_specs=[pl.BlockSpec((seq, wide), lambda b, g: (b, g)),
                  pl.BlockSpec((seq, wide), lambda b, g: (b, n_groups + g)),
                  pl.BlockSpec((seq, wide), lambda b, g: (b, 2 * n_groups + g))],
        out_specs=pl.BlockSpec((seq, wide), lambda b, g: (b, g)),
        out_shape=jax.ShapeDtypeStruct((t, A_WIDTH), BF16),
        scratch_shapes=[pltpu.VMEM((heads, seq // MOBA_BLOCK, A_HEAD_DIM), F32)],
        compiler_params=_params("parallel", "parallel"),
        name="moba_attention",
    )(z1, z1, z1)


def _swa_kernel(sinks_ref, q_ref, k_ref, v_ref, o_ref, kdup_ref, vdup_ref, *, scale):
    w = SWA_WINDOW
    n_blk = q_ref.shape[0] // w
    pair = pl.program_id(1)
    kv_odd = ((pair * 2) // (B_HEADS // B_KV_HEADS)) % 2
    lane = lax.broadcasted_iota(jnp.int32, (w, LANES), 1)
    low = lane < B_HEAD_DIM
    keep_orig = jnp.where(low, 0, 1) == kv_odd
    row = lax.broadcasted_iota(jnp.int32, (2 * w, 2 * w), 0)
    col = lax.broadcasted_iota(jnp.int32, (2 * w, 2 * w), 1)
    rel = (row & (w - 1)) + w - col
    band = (rel >= 0) & (rel < w)
    c_exp = scale * LOG2E
    sink_raw = jnp.where(lax.broadcasted_iota(jnp.int32, (2 * w, 1), 0) < w,
                         sinks_ref[2 * pair], sinks_ref[2 * pair + 1]) / scale

    def dup(x):
        return jnp.where(keep_orig, x, pltpu.roll(x, B_HEAD_DIM, axis=1)).astype(BF16)

    @pl.when(lax.rem(pair, (B_HEADS // B_KV_HEADS) // 2) == 0)
    def _():
        def stage(n, _):
            rows = pl.ds(pl.multiple_of(jnp.asarray(n, jnp.int32) * w, w), w)
            kdup_ref[rows, :] = dup(k_ref[rows, :])
            vdup_ref[rows, :] = dup(v_ref[rows, :])
            return 0

        lax.fori_loop(0, n_blk, stage, 0, unroll=4)

    def q_block(n, first):
        start = n * w
        cur = pl.ds(start if first else pl.multiple_of(start, w), w)
        q = q_ref[cur, :]
        zero = jnp.zeros_like(q)
        q2 = jnp.concatenate([jnp.where(low, q, zero), jnp.where(low, zero, q)], axis=0).astype(BF16)
        if first:
            k2 = jnp.concatenate([kdup_ref[cur, :], kdup_ref[cur, :]], axis=0)
            v2 = jnp.concatenate([vdup_ref[cur, :], vdup_ref[cur, :]], axis=0)
        else:
            both = pl.ds(pl.multiple_of(start - w, w), 2 * w)
            k2, v2 = kdup_ref[both, :], vdup_ref[both, :]
        s = lax.dot_general(q2, k2, (((1,), (1,)), ((), ())), preferred_element_type=F32)
        s = jnp.where(band & (col >= w) if first else band, s, NEG_INF)
        m = jnp.maximum(jnp.max(s, axis=1, keepdims=True), sink_raw)
        p = jnp.exp2((s - m) * c_exp)
        denom = jnp.sum(p, axis=1, keepdims=True) + jnp.exp2((sink_raw - m) * c_exp)
        o2 = jnp.dot(p.astype(BF16), v2, preferred_element_type=F32) / denom
        o_ref[cur, :] = jnp.where(low, o2[:w], o2[w:]).astype(o_ref.dtype)

    for u in range(SWA_UNROLL):
        q_block(u, u == 0)

    def group(g, _):
        g = jnp.asarray(g, jnp.int32)
        for u in range(SWA_UNROLL):
            q_block(g * SWA_UNROLL + u, False)
        return 0

    lax.fori_loop(1, n_blk // SWA_UNROLL, group, 0)


def _swa_attention(z1, sinks, *, batch, seq):
    t = z1.shape[0]
    pairs = B_HEADS // 2
    rep = B_HEADS // B_KV_HEADS
    assert (seq // SWA_WINDOW) % SWA_UNROLL == 0
    kv_group = lambda p: (2 * p) // rep // 2
    return pl.pallas_call(
        functools.partial(_swa_kernel, scale=B_HEAD_DIM ** -0.5),
        grid_spec=pltpu.PrefetchScalarGridSpec(
            num_scalar_prefetch=1,
            grid=(batch, pairs),
            in_specs=[pl.BlockSpec((seq, LANES), lambda b, p, sinks: (b, OFF_QB // LANES + p)),
                      pl.BlockSpec((seq, LANES), lambda b, p, sinks: (b, OFF_KB // LANES + kv_group(p))),
                      pl.BlockSpec((seq, LANES), lambda b, p, sinks: (b, OFF_VB // LANES + kv_group(p)))],
            out_specs=pl.BlockSpec((seq, LANES), lambda b, p, sinks: (b, p)),
            scratch_shapes=[pltpu.VMEM((seq, LANES), BF16), pltpu.VMEM((seq, LANES), BF16)],
        ),
        out_shape=jax.ShapeDtypeStruct((t, B_WIDTH), BF16),
        compiler_params=_params("arbitrary", "arbitrary"),
        name="swa_attention",
    )(sinks, z1, z1, z1)


def _gated_out_kernel(oa_ref, ob_ref, oc_ref, ga_ref, gb_ref, gc_ref, wa_ref, wb_ref, wc_ref, y_ref,
                      wa_bf, wb_bf, wc_bf):
    @pl.when(pl.program_id(1) == 0)
    def _():
        wa_bf[...] = wa_ref[...].astype(BF16)
        wb_bf[...] = wb_ref[...].astype(BF16)
        wc_bf[...] = wc_ref[...].astype(BF16)

    y = jax.nn.sigmoid(ga_ref[...]) * jnp.dot(oa_ref[...], wa_bf[...], preferred_element_type=F32)
    y += jax.nn.sigmoid(gb_ref[...]) * jnp.dot(ob_ref[...], wb_bf[...], preferred_element_type=F32)
    y += jax.nn.sigmoid(gc_ref[...]) * jnp.dot(oc_ref[...], wc_bf[...], preferred_element_type=F32)
    y_ref[...] = y.astype(y_ref.dtype)


def _gated_out(oa, ob, oc, gates, w_out_a, w_out_b, w_out_c, layer, *, tm=512, tn=512):
    t = oa.shape[0]
    d = w_out_a.shape[2]
    nb = d // tn
    o_spec = lambda width: pl.BlockSpec((tm, width), lambda j, i: (i, 0))
    g_spec = lambda which: pl.BlockSpec((tm, tn), lambda j, i: (i, which * nb + j))
    w_spec = lambda width: pl.BlockSpec((None, width, tn), lambda j, i: (layer, 0, j))
    return pl.pallas_call(
        _gated_out_kernel,
        grid=(nb, t // tm),
        in_specs=[o_spec(A_WIDTH), o_spec(B_WIDTH), o_spec(C_WIDTH), g_spec(0), g_spec(1), g_spec(2),
                  w_spec(A_WIDTH), w_spec(B_WIDTH), w_spec(C_WIDTH)],
        out_specs=pl.BlockSpec((tm, tn), lambda j, i: (i, j)),
        out_shape=jax.ShapeDtypeStruct((t, d), BF16),
        scratch_shapes=[pltpu.VMEM((A_WIDTH, tn), BF16), pltpu.VMEM((B_WIDTH, tn), BF16),
                        pltpu.VMEM((C_WIDTH, tn), BF16)],
        compiler_params=_params("arbitrary", "arbitrary"),
        name="gated_out",
    )(oa, ob, oc, gates, gates, gates, w_out_a, w_out_b, w_out_c)


HI16 = -65536


def _bf16_bits_hi(v):
    return lax.bitcast_convert_type(v.astype(BF16).astype(F32), jnp.int32)


def _router_kernel(h_ref, g_ref, w_ref, wlo_ref, b_ref, ids_ref, wts_ref, packed_ref):
    x = h_ref[...]
    hn = x * lax.rsqrt(jnp.mean(x * x, axis=-1, keepdims=True) + NORM_EPS) * g_ref[...]
    half = hn.shape[1] // 2
    packed_ref[...] = lax.shift_right_logical(_bf16_bits_hi(hn[:, :half]), 16) | _bf16_bits_hi(hn[:, half:])
    x_hi = hn.astype(BF16)
    x_lo = (hn - x_hi.astype(F32)).astype(BF16)
    logits = (jnp.dot(x_hi, w_ref[...], preferred_element_type=F32)
              + jnp.dot(x_hi, wlo_ref[...], preferred_element_type=F32)
              + jnp.dot(x_lo, w_ref[...], preferred_element_type=F32)) + b_ref[...]
    lane = lax.broadcasted_iota(jnp.int32, logits.shape, 1)
    far = float(LANES)
    is_g = lane < N_GROUPS
    g_id = lane.astype(F32)
    gmax = jnp.max(jnp.where(is_g, logits, NEG_INF), axis=1, keepdims=True)
    gsel = jnp.min(jnp.where(is_g & (logits == gmax), g_id, far), axis=1, keepdims=True)
    p_g = 1.0 / jnp.sum(jnp.where(is_g, jnp.exp(logits - gmax), 0.0), axis=1, keepdims=True)
    e_lane = lane - N_GROUPS
    e_id = e_lane.astype(F32)
    e_group = jnp.right_shift(e_lane, 2).astype(F32)
    in_grp = (e_lane >= 0) & (e_lane < N_EXPERTS) & (e_group == gsel)
    emax = jnp.max(jnp.where(in_grp, logits, NEG_INF), axis=1, keepdims=True)
    ee = jnp.where(in_grp, jnp.exp(jnp.where(in_grp, logits, emax) - emax), 0.0)
    ep = ee / jnp.sum(ee, axis=1, keepdims=True)
    p1 = jnp.max(jnp.where(in_grp, ep, -1.0), axis=1, keepdims=True)
    i1 = jnp.min(jnp.where(in_grp & (ep == p1), e_id, far), axis=1, keepdims=True)
    rest = in_grp & (e_id != i1)
    p2 = jnp.max(jnp.where(rest, ep, -1.0), axis=1, keepdims=True)
    i2 = jnp.min(jnp.where(rest & (ep == p2), e_id, far), axis=1, keepdims=True)
    tot = p1 + p2
    ids_ref[...] = jnp.where(lane == 0, i1, jnp.where(lane == 1, i2, 0.0)).astype(jnp.int32)
    wts_ref[...] = jnp.where(lane == 0, p_g * p1 / tot, jnp.where(lane == 1, p_g * p2 / tot, 0.0))


def _router(h, g, w_r, b_r, layer, *, tm=256):
    t, d = h.shape
    out = pl.BlockSpec((tm, LANES), lambda i: (i, 0))
    return pl.pallas_call(
        _router_kernel,
        grid=(t // tm,),
        in_specs=[pl.BlockSpec((tm, d), lambda i: (i, 0)),
                  pl.BlockSpec((None, 1, d), lambda i: (layer, 0, 0)),
                  pl.BlockSpec((None, d, LANES), lambda i: (layer, 0, 0)),
                  pl.BlockSpec((None, d, LANES), lambda i: (layer, 0, 0)),
                  pl.BlockSpec((None, 1, LANES), lambda i: (layer, 0, 0))],
        out_specs=[out, out, pl.BlockSpec((tm, d // 2), lambda i: (i, 0))],
        out_shape=[jax.ShapeDtypeStruct((t, LANES), jnp.int32), jax.ShapeDtypeStruct((t, LANES), F32),
                   jax.ShapeDtypeStruct((t, d // 2), jnp.int32)],
        compiler_params=_params("parallel"),
        name="moe_router",
    )(h, g, *w_r, b_r)


def _row_copy(src_hbm, tok, dst_ref, r, sem):
    return pltpu.make_async_copy(src_hbm.at[pl.ds(tok, 1), :], dst_ref.at[pl.ds(r, 1), :], sem)


GATHER_UNROLL = 8
NORM_CHUNK = 16


def _dispatch_kernel(tok_ref, nvalid_ref, h_hbm, o_ref, rows_ref, sem):
    i = pl.program_id(0)
    n_valid = nvalid_ref[0]
    n_rows = rows_ref.shape[1]
    slot = lax.rem(i, 2)

    def fetch(blk, slot):
        base = blk * n_rows

        def issue(g, _):
            for u in range(GATHER_UNROLL):
                r = g * GATHER_UNROLL + u
                _row_copy(h_hbm, tok_ref[base + r], rows_ref.at[slot], r, sem.at[slot]).start(priority=u % 2)
            return 0

        lax.fori_loop(0, n_rows // GATHER_UNROLL, issue, 0)

    @pl.when(i == 0)
    def _():
        fetch(0, 0)

    @pl.when(i + 1 < n_valid)
    def _():
        fetch(i + 1, 1 - slot)

    @pl.when(i < n_valid)
    def _():
        def drain(r, _):
            _row_copy(h_hbm, 0, rows_ref.at[slot], r, sem.at[slot]).wait()
            return 0

        lax.fori_loop(0, n_rows, drain, 0, unroll=GATHER_UNROLL)

        half = rows_ref.shape[2]

        def unpack(c, _):
            rows = pl.ds(pl.multiple_of(jnp.asarray(c, jnp.int32) * NORM_CHUNK, NORM_CHUNK), NORM_CHUNK)
            word = rows_ref[slot, rows, :]
            o_ref[rows, :half] = lax.bitcast_convert_type(lax.shift_left(word, 16), F32).astype(o_ref.dtype)
            o_ref[rows, half:] = lax.bitcast_convert_type(word & HI16, F32).astype(o_ref.dtype)
            return 0

        lax.fori_loop(0, n_rows // NORM_CHUNK, unpack, 0, unroll=4)


def _dispatch(packed, buf_tok, nvalid, *, n_blocks):
    t, half = packed.shape
    blk = lambda i, tok, nv: (jnp.minimum(i, nv[0] - 1), 0)
    return pl.pallas_call(
        _dispatch_kernel,
        grid_spec=pltpu.PrefetchScalarGridSpec(
            num_scalar_prefetch=2,
            grid=(n_blocks,),
            in_specs=[pl.BlockSpec(memory_space=pl.ANY)],
            out_specs=pl.BlockSpec((MOE_ROWS, 2 * half), blk),
            scratch_shapes=[pltpu.VMEM((2, MOE_ROWS, half), jnp.int32), pltpu.SemaphoreType.DMA((2,))],
        ),
        out_shape=jax.ShapeDtypeStruct((n_blocks * MOE_ROWS, 2 * half), BF16),
        compiler_params=_params("arbitrary"),
        name="moe_dispatch",
    )(buf_tok, nvalid, packed)


def _run_weights(be_ref, first_ref, next_ref, i, pass_id, n_passes, n_blocks, copies, stage):
    @pl.when((pass_id == 0) & (i == 0))
    def _():
        for c in copies(be_ref[0], 0):
            c.start()

    @pl.when(first_ref[i] == 1)
    def _():
        for c in copies(be_ref[i], pass_id):
            c.wait()
        stage()
        nxt = next_ref[i]
        same_pass = nxt < n_blocks
        e_next = jnp.where(same_pass, be_ref[jnp.minimum(nxt, n_blocks - 1)], be_ref[0])
        pass_next = jnp.where(same_pass, pass_id, pass_id + 1)

        @pl.when(same_pass | (pass_id + 1 < n_passes))
        def _():
            for c in copies(e_next, pass_next):
                c.start()


def _expert_up_kernel(be_ref, nvalid_ref, first_ref, next_ref, x_ref, wg_hbm, wu_hbm, o_ref,
                      wg_land, wu_land, wg_bf, wu_bf, sem, *, layer):
    f, i = pl.program_id(0), pl.program_id(1)
    tf = wg_land.shape[1]

    def copies(e, ff_tile):
        cols = pl.ds(pl.multiple_of(ff_tile * tf, tf), tf)
        return (pltpu.make_async_copy(wg_hbm.at[layer, e, :, cols], wg_land, sem.at[0]),
                pltpu.make_async_copy(wu_hbm.at[layer, e, :, cols], wu_land, sem.at[1]))

    def stage():
        wg_bf[...] = wg_land[...].astype(BF16)
        wu_bf[...] = wu_land[...].astype(BF16)

    _run_weights(be_ref, first_ref, next_ref, i, f, pl.num_programs(0), pl.num_programs(1), copies, stage)

    @pl.when(i < nvalid_ref[0])
    def _():
        x = x_ref[...]
        hg = jnp.dot(x, wg_bf[...], preferred_element_type=F32)
        hu = jnp.dot(x, wu_bf[...], preferred_element_type=F32)
        o_ref[...] = (jax.nn.silu(hg) * hu).astype(o_ref.dtype)


def _expert_up(xb, w_gate, w_up, layer, runs, *, n_blocks):
    d = xb.shape[1]
    ff = w_gate.shape[3]
    tf = MOE_FF_TILE
    row_blk = lambda f, i, be, nv, first, nxt: (jnp.minimum(i, nv[0] - 1), 0)
    return pl.pallas_call(
        functools.partial(_expert_up_kernel, layer=layer),
        grid_spec=pltpu.PrefetchScalarGridSpec(
            num_scalar_prefetch=4,
            grid=(ff // tf, n_blocks),
            in_specs=[pl.BlockSpec((MOE_ROWS, d), row_blk), pl.BlockSpec(memory_space=pl.ANY),
                      pl.BlockSpec(memory_space=pl.ANY)],
            out_specs=pl.BlockSpec((MOE_ROWS, tf),
                                   lambda f, i, be, nv, first, nxt: (jnp.minimum(i, nv[0] - 1), f)),
            scratch_shapes=[pltpu.VMEM((d, tf), F32), pltpu.VMEM((d, tf), F32),
                            pltpu.VMEM((d, tf), BF16), pltpu.VMEM((d, tf), BF16),
                            pltpu.SemaphoreType.DMA((2,))],
        ),
        out_shape=jax.ShapeDtypeStruct((n_blocks * MOE_ROWS, ff), BF16),
        compiler_params=_params("arbitrary", "arbitrary"),
        name="moe_expert_up",
    )(*runs, xb, w_gate, w_up)


def _expert_down_kernel(be_ref, nvalid_ref, first_ref, next_ref, a_ref, wd_hbm, o_ref, wd_land, wd_bf, sem,
                        *, layer):
    i = pl.program_id(0)

    def copies(e, _):
        return (pltpu.make_async_copy(wd_hbm.at[layer, e], wd_land, sem.at[0]),)

    def stage():
        wd_bf[...] = wd_land[...].astype(BF16)

    _run_weights(be_ref, first_ref, next_ref, i, 0, 1, pl.num_programs(0), copies, stage)

    @pl.when(i < nvalid_ref[0])
    def _():
        o_ref[...] = jnp.dot(a_ref[...], wd_bf[...], preferred_element_type=F32)


def _expert_down(act, w_down, layer, runs, *, n_blocks):
    ff, d = w_down.shape[2], w_down.shape[3]
    row_blk = lambda i, be, nv, first, nxt: (jnp.minimum(i, nv[0] - 1), 0)
    return pl.pallas_call(
        functools.partial(_expert_down_kernel, layer=layer),
        grid_spec=pltpu.PrefetchScalarGridSpec(
            num_scalar_prefetch=4,
            grid=(n_blocks,),
            in_specs=[pl.BlockSpec((MOE_ROWS, ff), row_blk), pl.BlockSpec(memory_space=pl.ANY)],
            out_specs=pl.BlockSpec((MOE_ROWS, d), row_blk),
            scratch_shapes=[pltpu.VMEM((ff, d), F32), pltpu.VMEM((ff, d), BF16), pltpu.SemaphoreType.DMA((1,))],
        ),
        out_shape=jax.ShapeDtypeStruct((n_blocks * MOE_ROWS, d), F32),
        compiler_params=_params("arbitrary"),
        name="moe_expert_down",
    )(*runs, act, w_down)


def _combine_kernel(pos_ref, h_ref, wts_ref, g_ref, y_hbm, *refs, emit_h):
    o_ref = refs[0] if emit_h else None
    n_ref, y0_ref, y1_ref, sem = refs[int(emit_h):]
    i = pl.program_id(0)
    tm = h_ref.shape[0]
    slot = lax.rem(i, 2)

    def fetch(blk, slot):
        base = blk * tm * EXPERT_TOPK

        def issue(g, _):
            for u in range(GATHER_UNROLL):
                r = g * GATHER_UNROLL + u
                _row_copy(y_hbm, pos_ref[base + EXPERT_TOPK * r], y0_ref.at[slot], r, sem.at[slot]).start(priority=0)
                _row_copy(y_hbm, pos_ref[base + EXPERT_TOPK * r + 1], y1_ref.at[slot], r,
                          sem.at[slot]).start(priority=1)
            return 0

        lax.fori_loop(0, tm // GATHER_UNROLL, issue, 0)

    @pl.when(i == 0)
    def _():
        fetch(0, 0)

    @pl.when(i + 1 < pl.num_programs(0))
    def _():
        fetch(i + 1, 1 - slot)

    def drain(r, _):
        _row_copy(y_hbm, 0, y0_ref.at[slot], r, sem.at[slot]).wait()
        _row_copy(y_hbm, 0, y1_ref.at[slot], r, sem.at[slot]).wait()
        return 0

    lax.fori_loop(0, tm, drain, 0, unroll=GATHER_UNROLL)

    def rows_chunk(c, _):
        rows = pl.ds(pl.multiple_of(jnp.asarray(c, jnp.int32) * NORM_CHUNK, NORM_CHUNK), NORM_CHUNK)
        w = wts_ref[rows, :]
        hv = h_ref[rows, :] + w[:, 0:1] * y0_ref[slot, rows, :] + w[:, 1:2] * y1_ref[slot, rows, :]
        if emit_h:
            o_ref[rows, :] = hv
        hn = hv * lax.rsqrt(jnp.mean(hv * hv, axis=-1, keepdims=True) + NORM_EPS) * g_ref[...]
        n_ref[rows, :] = hn.astype(n_ref.dtype)
        return 0

    lax.fori_loop(0, tm // NORM_CHUNK, rows_chunk, 0, unroll=4)


def _combine(h, wts, y, pos, g, *, norm_dtype, emit_h, tm=128):
    t, d = h.shape
    row = pl.BlockSpec((tm, d), lambda i, pos: (i, 0))
    outs = pl.pallas_call(
        functools.partial(_combine_kernel, emit_h=emit_h),
        grid_spec=pltpu.PrefetchScalarGridSpec(
            num_scalar_prefetch=1,
            grid=(t // tm,),
            in_specs=[row, pl.BlockSpec((tm, LANES), lambda i, pos: (i, 0)),
                      pl.BlockSpec((1, d), lambda i, pos: (0, 0)), pl.BlockSpec(memory_space=pl.ANY)],
            out_specs=[row] * (1 + emit_h),
            scratch_shapes=[pltpu.VMEM((2, tm, d), F32), pltpu.VMEM((2, tm, d), F32),
                            pltpu.SemaphoreType.DMA((2,))],
        ),
        out_shape=([jax.ShapeDtypeStruct((t, d), F32)] if emit_h else [])
        + [jax.ShapeDtypeStruct((t, d), norm_dtype)],
        compiler_params=_params("arbitrary"),
        name="moe_combine",
    )(pos, h, wts, g.reshape(1, d), y)
    return (outs[0], outs[1]) if emit_h else (None, outs[0])


def _hier_moe(h, ffn_norm_g, w_r, b_r, w_gate, w_up, w_down, layer, next_norm_g, *, norm_dtype, emit_h):
    t, d = h.shape
    tk = t * EXPERT_TOPK
    n_blocks = tk // MOE_ROWS + N_EXPERTS
    ids, wts, hn_packed = _router(h, ffn_norm_g, w_r, b_r, layer)
    flat_e = ids[:, :EXPERT_TOPK].reshape(tk)
    onehot = (flat_e[:, None] == jnp.arange(N_EXPERTS, dtype=jnp.int32)[None, :]).astype(jnp.int32)
    csum = jnp.cumsum(onehot, axis=0)
    rank = jnp.sum(onehot * csum, axis=1) - 1
    counts = csum[-1]
    padded = (counts + MOE_ROWS - 1) // MOE_ROWS * MOE_ROWS
    pad_end = jnp.cumsum(padded)
    pad_start = pad_end - padded
    dest = (pad_start[flat_e] + rank).astype(jnp.int32)
    flat_tok = jnp.arange(tk, dtype=jnp.int32) // EXPERT_TOPK
    buf_tok = jnp.zeros((n_blocks * MOE_ROWS,), jnp.int32).at[dest].set(flat_tok)
    nvalid = (pad_end[-1] // MOE_ROWS).astype(jnp.int32).reshape(1)
    blk_start = jnp.minimum(jnp.arange(n_blocks, dtype=jnp.int32), nvalid[0] - 1) * MOE_ROWS
    block_expert = jnp.minimum(jnp.searchsorted(pad_end, blk_start, side='right'), N_EXPERTS - 1).astype(jnp.int32)

    blk = jnp.arange(n_blocks, dtype=jnp.int32)
    prev_expert = jnp.concatenate([jnp.full((1,), -1, jnp.int32), block_expert[:-1]])
    run_first = (blk < nvalid[0]) & (block_expert != prev_expert)
    first_at_or_after = lax.cummin(jnp.where(run_first, blk, n_blocks), reverse=True)
    run_next = jnp.concatenate([first_at_or_after[1:], jnp.full((1,), n_blocks, jnp.int32)])
    runs = (block_expert, nvalid, run_first.astype(jnp.int32), run_next)

    xb = _dispatch(hn_packed, buf_tok, nvalid, n_blocks=n_blocks)
    act = _expert_up(xb, w_gate, w_up, layer, runs, n_blocks=n_blocks)
    y = _expert_down(act, w_down, layer, runs, n_blocks=n_blocks)
    return _combine(h, wts, y, dest, next_norm_g, norm_dtype=norm_dtype, emit_h=emit_h)


ROPE_A, ROPE_B, ROPE_PE = 0, 1, 2


def _in_proj_rope_table(col):
    if OFF_QA <= col < OFF_VA:
        return ROPE_A
    if OFF_QB <= col < OFF_VB:
        return ROPE_B
    if col == OFF_KPE:
        return ROPE_PE
    return None


def _mla_q_rope_table(col):
    return 0 if (col // LANES) % 2 == 1 else None


def kernel(x, positions, attn_norm_g, w_in, q_norm_g, kv_norm_g, wq_b, wkv_b, sinks, w_out_a, w_out_b,
           w_out_c, w_o, ffn_norm_g, w_group, b_group, w_expert, b_expert, w_gate, w_up, w_down,
           final_norm_g):
    batch, seq, d = x.shape
    depth = w_in.shape[0]
    t = batch * seq
    h = x.reshape(t, d)
    pos = positions.reshape(t).astype(F32)

    w_in_t = jnp.swapaxes(w_in, 1, 2)
    q_head = C_NOPE_DIM + C_ROPE_DIM
    wq_pad = jnp.pad(wq_b.reshape(depth, C_Q_RANK, C_HEADS, q_head),
                     ((0, 0), (0, 0), (0, 0), (0, 2 * LANES - q_head))
                     ).reshape(depth, C_Q_RANK, C_HEADS * 2 * LANES)
    wq_pad = (wq_pad * MLA_Q_FACTOR).astype(BF16)
    w_r = jnp.concatenate([w_group, w_expert,
                           jnp.zeros((depth, d, LANES - N_GROUPS - N_EXPERTS), F32)], axis=2)
    w_r_hi = w_r.astype(BF16)
    w_r = (w_r_hi, (w_r - w_r_hi.astype(F32)).astype(BF16))
    b_r = jnp.concatenate([b_group, b_expert,
                           jnp.zeros((depth, LANES - N_GROUPS - N_EXPERTS), F32)], axis=1).reshape(depth, 1, LANES)
    attn_g = attn_norm_g.reshape(depth, 1, d)
    ffn_g = ffn_norm_g.reshape(depth, 1, d)

    c_a, s_a = _rope_tables(pos, A_HEAD_DIM)
    c_b, s_b = _rope_tables(pos, B_HEAD_DIM)
    c_pe, s_pe = _rope_tables(pos, C_ROPE_DIM, keep_lanes=C_ROPE_DIM)
    rope_tables = ((c_a, s_a, A_HEAD_DIM // 2), (c_b, s_b, B_HEAD_DIM // 2), (c_pe, s_pe, C_ROPE_DIM // 2))

    hn = _rmsnorm(h, attn_norm_g[0])
    for l in range(depth):
        z1 = _matmul(hn, w_in_t, l, n_out=Z1_WIDTH, w_is_nk=True, rope_tables=rope_tables,
                     rope_group_table=_in_proj_rope_table, name="in_proj")
        gates = _matmul(hn, w_in_t, l, n_out=3 * d, col_off=OFF_GATES, w_is_nk=True, name="gate_proj")
        o_a = _moba_attention(z1, batch=batch, seq=seq)
        o_b = _swa_attention(z1, sinks[l], batch=batch, seq=seq)
        cq_n = _rmsnorm(z1, q_norm_g[l], col_off=OFF_CQ, width=C_Q_RANK, piece=512)
        ckv_n = _rmsnorm(z1, kv_norm_g[l], col_off=OFF_CKV, width=C_KV_RANK, piece=512)
        q_c = _matmul(cq_n, wq_pad, l, n_out=C_HEADS * 2 * LANES, tn=1024, out_dtype=BF16,
                      rope_tables=rope_tables[ROPE_PE:], rope_group_table=_mla_q_rope_table, name="mla_q_proj")
        kv_c = _matmul(ckv_n, wkv_b, l, n_out=C_HEADS * (C_NOPE_DIM + C_V_DIM), tn=1024, out_dtype=BF16,
                       name="mla_kv_proj")
        o_c = _mla_attention(q_c, kv_c, z1, batch=batch, seq=seq)
        y = _gated_out(o_a, o_b, o_c, gates, w_out_a, w_out_b, w_out_c, l)
        h = _matmul(y, w_o, l, n_out=d, residual=h, name="out_proj")
        last = l == depth - 1
        h, hn = _hier_moe(h, ffn_g, w_r, b_r, w_gate, w_up, w_down, l,
                          final_norm_g if last else attn_norm_g[l + 1],
                          norm_dtype=F32 if last else BF16, emit_h=not last)
    return hn.reshape(batch, seq, d)
```

```python
import functools

import jax
import jax.numpy as jnp
from jax import lax
from jax.experimental import pallas as pl
from jax.experimental.pallas import tpu as pltpu

F32 = jnp.float32
BF16 = jnp.bfloat16

ROPE_THETA = 10000.0
NORM_EPS = 1e-6
NEG_INF = -1e30
PICKED = -3e38
LOG2E = 1.4426950408889634

A_HEADS, A_HEAD_DIM = 16, 128
MOBA_BLOCK, MOBA_TOPK = 256, 3
B_HEADS, B_KV_HEADS, B_HEAD_DIM, SWA_WINDOW = 32, 4, 64, 128
C_HEADS, C_Q_RANK, C_KV_RANK, C_NOPE_DIM, C_ROPE_DIM, C_V_DIM = 16, 1024, 512, 128, 64, 128
N_GROUPS, EXPERTS_PER_GROUP, EXPERT_TOPK, D_FF_EXPERT = 8, 4, 2, 768
N_EXPERTS = N_GROUPS * EXPERTS_PER_GROUP
assert EXPERTS_PER_GROUP == 4

LANES = 128
VMEM_LIMIT_BYTES = 58 * 1024 * 1024

A_WIDTH = A_HEADS * A_HEAD_DIM
B_WIDTH = B_HEADS * B_HEAD_DIM
B_KV_WIDTH = B_KV_HEADS * B_HEAD_DIM
C_WIDTH = C_HEADS * C_V_DIM
OFF_QA = 0
OFF_KA = OFF_QA + A_WIDTH
OFF_VA = OFF_KA + A_WIDTH
OFF_QB = OFF_VA + A_WIDTH
OFF_KB = OFF_QB + B_WIDTH
OFF_VB = OFF_KB + B_KV_WIDTH
OFF_CQ = OFF_VB + B_KV_WIDTH
OFF_CKV = OFF_CQ + C_Q_RANK
OFF_KPE = OFF_CKV + C_KV_RANK
OFF_GATES = OFF_KPE + C_ROPE_DIM
Z1_WIDTH = 10752

MOE_ROWS = 256
MOE_FF_TILE = 256
SWA_UNROLL = 8


def _params(*sem):
    return pltpu.CompilerParams(dimension_semantics=sem, vmem_limit_bytes=VMEM_LIMIT_BYTES)


def _rmsnorm_kernel(*refs, n_pieces, width):
    x_refs, g_ref, o_ref = refs[:n_pieces], refs[n_pieces], refs[n_pieces + 1]
    xs = [r[...].astype(F32) for r in x_refs]
    ss = sum(jnp.sum(x * x, axis=-1, keepdims=True) for x in xs)
    inv = lax.rsqrt(ss * (1.0 / width) + NORM_EPS)
    pw = xs[0].shape[1]
    for p, x in enumerate(xs):
        o_ref[:, p * pw:(p + 1) * pw] = (x * inv * g_ref[:, p * pw:(p + 1) * pw]).astype(o_ref.dtype)


def _rmsnorm(x, g, *, col_off=0, width=None, piece=None, tm=256, out_dtype=BF16):
    t = x.shape[0]
    width = width or x.shape[1]
    piece = piece or width
    n_pieces = width // piece
    off = col_off // piece
    assert col_off % piece == 0 and width % piece == 0
    in_specs = [pl.BlockSpec((tm, piece), functools.partial(lambda i, p: (i, off + p), p=p))
                for p in range(n_pieces)]
    in_specs.append(pl.BlockSpec((1, width), lambda i: (0, 0)))
    return pl.pallas_call(
        functools.partial(_rmsnorm_kernel, n_pieces=n_pieces, width=width),
        grid=(t // tm,),
        in_specs=in_specs,
        out_specs=pl.BlockSpec((tm, width), lambda i: (i, 0)),
        out_shape=jax.ShapeDtypeStruct((t, width), out_dtype),
        compiler_params=_params("parallel"),
        name="rmsnorm",
    )(*([x] * n_pieces), g.reshape(1, width))


def _stage_weight(w_ref, wbf_ref, w_is_nk):
    if not w_is_nk:
        wbf_ref[...] = w_ref[...].astype(BF16)
        return
    _, tn, k = w_ref.shape
    for c in range(k // tn):
        wbf_ref[c * tn:(c + 1) * tn, :] = w_ref[0, :, c * tn:(c + 1) * tn].T.astype(BF16)


def _swap_halves(x, half):
    if half == 64:
        return pltpu.roll(x, 64, axis=1)
    lane = lax.broadcasted_iota(jnp.int32, x.shape, 1)
    return jnp.where((lane & half) == 0, pltpu.roll(x, LANES - half, axis=1), pltpu.roll(x, half, axis=1))


def _matmul_kernel(*refs, stage_w, w_is_nk, has_res, rope_modes, rope_halves):
    x_ref, w_ref = refs[0], refs[1]
    n_in = 2 + has_res + 2 * len(rope_halves)
    r_ref = refs[2] if has_res else None
    table_refs = refs[2 + has_res:n_in]
    o_ref = refs[n_in]
    if stage_w:
        wbf_ref = refs[n_in + 1]

        @pl.when(pl.program_id(1) == 0)
        def _():
            _stage_weight(w_ref, wbf_ref, w_is_nk)

        w_src = wbf_ref
    else:
        w_src = w_ref

    def product():
        acc = jnp.dot(x_ref[...], w_src[...], preferred_element_type=F32)
        return acc + r_ref[...] if has_res else acc

    if rope_modes is None:
        o_ref[...] = product().astype(o_ref.dtype)
        return
    j = pl.program_id(0)
    for pattern in sorted(set(rope_modes), key=str):
        blocks = [b for b, p in enumerate(rope_modes) if p == pattern]

        @pl.when(functools.reduce(jnp.logical_or, [j == b for b in blocks]))
        def _(pattern=pattern):
            acc = product()
            for g, table in enumerate(pattern):
                xg = acc[:, g * LANES:(g + 1) * LANES]
                if table is not None:
                    c_ref, s_ref = table_refs[2 * table], table_refs[2 * table + 1]
                    xg = xg * c_ref[...] + _swap_halves(xg, rope_halves[table]) * s_ref[...]
                o_ref[:, g * LANES:(g + 1) * LANES] = xg.astype(o_ref.dtype)


def _matmul(x, w, layer, *, n_out, col_off=0, w_is_nk=False, tm=1024, tn=512, out_dtype=F32, residual=None,
            rope_tables=(), rope_group_table=None, name="matmul"):
    m, k = x.shape
    assert w.shape[2 if w_is_nk else 1] == k and m % tm == 0 and n_out % tn == 0
    stage_w = w_is_nk or w.dtype != BF16
    rope_modes = None
    if rope_group_table is not None:
        rope_modes = tuple(tuple(rope_group_table(b * tn + g * LANES) for g in range(tn // LANES))
                           for b in range(n_out // tn))
    if w_is_nk:
        assert k % tn == 0 and col_off % 8 == 0
        w_spec = pl.BlockSpec((pl.Element(1), pl.Element(tn), pl.Element(k)),
                              lambda j, i: (layer, pl.multiple_of(col_off + j * tn, 8), 0))
    else:
        assert col_off % tn == 0
        w_spec = pl.BlockSpec((None, k, tn), lambda j, i: (layer, 0, j + col_off // tn))
    in_specs = [pl.BlockSpec((tm, k), lambda j, i: (i, 0)), w_spec]
    args = [x, w]
    if residual is not None:
        in_specs.append(pl.BlockSpec((tm, tn), lambda j, i: (i, j)))
        args.append(residual)
    for c, s, _ in rope_tables:
        in_specs += [pl.BlockSpec((tm, LANES), lambda j, i: (i, 0))] * 2
        args += [c, s]
    return pl.pallas_call(
        functools.partial(_matmul_kernel, stage_w=stage_w, w_is_nk=w_is_nk, has_res=residual is not None,
                          rope_modes=rope_modes, rope_halves=tuple(half for _, _, half in rope_tables)),
        grid=(n_out // tn, m // tm),
        in_specs=in_specs,
        out_specs=pl.BlockSpec((tm, tn), lambda j, i: (i, j)),
        out_shape=jax.ShapeDtypeStruct((m, n_out), out_dtype),
        scratch_shapes=[pltpu.VMEM((k, tn), BF16)] if stage_w else [],
        compiler_params=_params("arbitrary", "arbitrary"),
        name=name,
    )(*args)


def _rope_tables_kernel(pos_ref, invf_ref, sign_ref, keep_ref, c_ref, s_ref):
    ang = pos_ref[...] * invf_ref[...]
    c_ref[...] = jnp.cos(ang) * keep_ref[...]
    s_ref[...] = jnp.sin(ang) * sign_ref[...]


def _rope_tables(pos, dim, *, keep_lanes=LANES):
    t = pos.shape[0]
    half = dim // 2
    lane = jnp.arange(LANES)
    invf = (ROPE_THETA ** (-(2.0 * (lane % half)).astype(F32) / dim)).reshape(1, LANES)
    keep = (lane < keep_lanes).astype(F32).reshape(1, LANES)
    sign = jnp.where((lane % dim) < half, -1.0, 1.0).astype(F32).reshape(1, LANES) * keep
    tm = 1024
    row = pl.BlockSpec((1, LANES), lambda i: (0, 0))
    out = pl.BlockSpec((tm, LANES), lambda i: (i, 0))
    return pl.pallas_call(
        _rope_tables_kernel,
        grid=(t // tm,),
        in_specs=[pl.BlockSpec((tm, 1), lambda i: (i, 0)), row, row, row],
        out_specs=[out, out],
        out_shape=[jax.ShapeDtypeStruct((t, LANES), F32)] * 2,
        compiler_params=_params("parallel"),
        name="rope_tables",
    )(pos.reshape(t, 1), invf, sign, keep)


def _online_block(q, k, v, s_mask, carry):
    m, l, acc = carry
    s = s_mask(lax.dot_general(q, k, (((1,), (1,)), ((), ())), preferred_element_type=F32))
    m_new = jnp.maximum(m, jnp.max(s, axis=1, keepdims=True))
    alpha = jnp.exp2(m - m_new)
    p = jnp.exp2(s - m_new)
    l = alpha * l + jnp.sum(p, axis=1, keepdims=True)
    acc = alpha * acc + jnp.dot(p.astype(BF16), v, preferred_element_type=F32)
    return m_new, l, acc


def _causal_attend(qi, qs, load_kvs, tq, dv, diag_mask, past_mask):
    heads = range(len(qs))
    init = (jnp.full((tq, 1), NEG_INF, F32), jnp.zeros((tq, 1), F32), jnp.zeros((tq, dv), F32))
    kvs = load_kvs(qi)
    carries = tuple(_online_block(qs[h], *kvs[h], functools.partial(diag_mask, h), init) for h in heads)

    def past(j, carries):
        j = jnp.asarray(j, jnp.int32)
        kvs = load_kvs(j)
        return tuple(_online_block(qs[h], *kvs[h], functools.partial(past_mask, h, j), carries[h])
                     for h in heads)

    carries = lax.fori_loop(0, qi, past, carries)
    return [acc / l for _, l, acc in carries]


def _mla_kernel(q_ref, kv_ref, kpe_ref, o_ref, *, tq, heads):
    n_q = q_ref.shape[0] // tq
    qw, dv = 2 * LANES, C_V_DIM
    row = lax.broadcasted_iota(jnp.int32, (tq, tq), 0)
    col = lax.broadcasted_iota(jnp.int32, (tq, tq), 1)
    causal = col <= row

    def load_kvs(j):
        rows = pl.ds(pl.multiple_of(j * tq, tq), tq)
        kpe = kpe_ref[rows, :].astype(BF16)
        return [(jnp.concatenate([kv_ref[rows, h * qw:h * qw + C_NOPE_DIM], kpe], axis=1),
                 kv_ref[rows, h * qw + C_NOPE_DIM:(h + 1) * qw]) for h in range(heads)]

    def q_tile(qi, _):
        qi = jnp.asarray(qi, jnp.int32)
        rows = pl.ds(pl.multiple_of(qi * tq, tq), tq)
        qs = [q_ref[rows, h * qw:(h + 1) * qw] for h in range(heads)]
        outs = _causal_attend(qi, qs, load_kvs, tq, dv,
                              lambda h, s: jnp.where(causal, s, NEG_INF), lambda h, j, s: s)
        for h in range(heads):
            o_ref[rows, h * dv:(h + 1) * dv] = outs[h].astype(o_ref.dtype)
        return 0

    lax.fori_loop(0, n_q, q_tile, 0)


MLA_Q_FACTOR = (C_NOPE_DIM + C_ROPE_DIM) ** -0.5 * LOG2E


def _mla_attention(q, kv, z1, *, batch, seq, tq=1024, heads=2):
    t = q.shape[0]
    wide = pl.BlockSpec((seq, heads * 2 * LANES), lambda b, g: (b, g))
    return pl.pallas_call(
        functools.partial(_mla_kernel, tq=tq, heads=heads),
        grid=(batch, C_HEADS // heads),
        in_specs=[wide, wide, pl.BlockSpec((seq, LANES), lambda b, g: (b, OFF_KPE // LANES))],
        out_specs=pl.BlockSpec((seq, heads * C_V_DIM), lambda b, g: (b, g)),
        out_shape=jax.ShapeDtypeStruct((t, C_WIDTH), BF16),
        compiler_params=_params("parallel", "parallel"),
        name="mla_attention",
    )(q, kv, z1)


def _moba_kernel(q_ref, k_ref, v_ref, o_ref, kmean_ref, *, tq, heads, scale):
    bs, d = MOBA_BLOCK, A_HEAD_DIM
    n_blk = q_ref.shape[0] // bs
    per_tile = tq // bs
    for h in range(heads):
        for n in range(n_blk):
            kmean_ref[h, n:n + 1, :] = jnp.mean(k_ref[n * bs:(n + 1) * bs, h * d:(h + 1) * d], axis=0, keepdims=True)
    blk_i = lax.broadcasted_iota(jnp.int32, (n_blk, tq), 0)
    blk = blk_i.astype(F32)
    bit_of_blk = lax.shift_left(jnp.ones_like(blk_i), blk_i).astype(F32)
    lane_q = lax.broadcasted_iota(jnp.int32, (1, tq), 1)
    lane_sub = sum(((lane_q >= c * bs).astype(F32) for c in range(1, per_tile)), jnp.zeros((1, tq), F32))
    row1 = lax.broadcasted_iota(jnp.int32, (tq, 1), 0)
    row_sub = sum(((row1 >= c * bs).astype(F32) for c in range(1, per_tile)), jnp.zeros((tq, 1), F32))
    col_local = lax.broadcasted_iota(jnp.int32, (tq, bs), 1)

    def load_kvs(j):
        rows = pl.ds(pl.multiple_of(j * tq, tq), tq)
        return [(k_ref[rows, h * d:(h + 1) * d].astype(BF16), v_ref[rows, h * d:(h + 1) * d].astype(BF16))
                for h in range(heads)]

    def q_tile(qi, _):
        qi = jnp.asarray(qi, jnp.int32)
        rows = pl.ds(pl.multiple_of(qi * tq, tq), tq)
        first = qi * per_tile
        own = first.astype(F32) + lane_sub
        qs, sels = [], []
        for h in range(heads):
            qf = q_ref[rows, h * d:(h + 1) * d]
            gate = lax.dot_general(kmean_ref[h], qf, (((1,), (1,)), ((), ())),
                                   precision=lax.Precision.HIGHEST, preferred_element_type=F32)
            gate = jnp.where(blk < own, gate, NEG_INF)
            sel = jnp.zeros((n_blk, tq), F32)
            for _ in range(MOBA_TOPK):
                best = jnp.max(gate, axis=0, keepdims=True)
                idx = jnp.min(jnp.where(gate == best, blk, float(n_blk)), axis=0, keepdims=True)
                pick = blk == idx
                sel = jnp.where(pick & (blk < own), 1.0, sel)
                gate = jnp.where(pick, PICKED, gate)
            qs.append((qf * (scale * LOG2E)).astype(BF16))
            code = jnp.sum(sel * bit_of_blk, axis=0, keepdims=True)
            code_col = jnp.max(jnp.broadcast_to(code, (LANES, tq)).T, axis=1, keepdims=True)
            sels.append(code_col.astype(jnp.int32))

        def chosen(h, b):
            return (lax.shift_right_logical(sels[h], jnp.broadcast_to(b, sels[h].shape)) & 1) == 1

        def by_block(s, mask_block):
            parts = [mask_block(c, s[:, c * bs:(c + 1) * bs]) for c in range(per_tile)]
            return parts[0] if per_tile == 1 else jnp.concatenate(parts, axis=1)

        def diag_mask(h, s):
            def mask_block(c, sc):
                limit = jnp.where(row_sub == c, row1 - c * bs + 1,
                                  jnp.where((row_sub > c) & chosen(h, first + c), bs, 0))
                return jnp.where(col_local < limit, sc, NEG_INF)
            return by_block(s, mask_block)

        def past_mask(h, j, s):
            jb = j * per_tile
            return by_block(s, lambda c, sc: jnp.where(chosen(h, jb + c), sc, NEG_INF))

        outs = _causal_attend(qi, qs, load_kvs, tq, d, diag_mask, past_mask)
        for h in range(heads):
            o_ref[rows, h * d:(h + 1) * d] = outs[h].astype(o_ref.dtype)
        return 0

    lax.fori_loop(0, q_ref.shape[0] // tq, q_tile, 0)


def _moba_attention(z1, *, batch, seq, tq=1024, heads=2):
    t = z1.shape[0]
    assert seq % tq == 0 and tq % MOBA_BLOCK == 0
    n_groups = A_HEADS // heads
    wide = heads * A_HEAD_DIM
    assert OFF_QA == 0 and OFF_KA == A_WIDTH and OFF_VA == 2 * A_WIDTH
    return pl.pallas_call(
        functools.partial(_moba_kernel, tq=tq, heads=heads, scale=A_HEAD_DIM ** -0.5),
        grid=(batch, n_groups),
        in_specs=[pl.BlockSpec((seq, wide), lambda b, g: (b, g)),
                  pl.BlockSpec((seq, wide), lambda b, g: (b, n_groups + g)),
                  pl.BlockSpec((seq, wide), lambda b, g: (b, 2 * n_groups + g))],
        out_specs=pl.BlockSpec((seq, wide), lambda b, g: (b, g)),
        out_shape=jax.ShapeDtypeStruct((t, A_WIDTH), BF16),
        scratch_shapes=[pltpu.VMEM((heads, seq // MOBA_BLOCK, A_HEAD_DIM), F32)],
        compiler_params=_params("parallel", "parallel"),
        name="moba_attention",
    )(z1, z1, z1)


def _swa_kernel(sinks_ref, q_ref, k_ref, v_ref, o_ref, kdup_ref, vdup_ref, *, scale):
    w = SWA_WINDOW
    n_blk = q_ref.shape[0] // w
    pair = pl.program_id(1)
    kv_odd = ((pair * 2) // (B_HEADS // B_KV_HEADS)) % 2
    lane = lax.broadcasted_iota(jnp.int32, (w, LANES), 1)
    low = lane < B_HEAD_DIM
    keep_orig = jnp.where(low, 0, 1) == kv_odd
    row = lax.broadcasted_iota(jnp.int32, (2 * w, 2 * w), 0)
    col = lax.broadcasted_iota(jnp.int32, (2 * w, 2 * w), 1)
    rel = (row & (w - 1)) + w - col
    band = (rel >= 0) & (rel < w)
    sink2 = jnp.where(lax.broadcasted_iota(jnp.int32, (2 * w, 1), 0) < w,
                      sinks_ref[2 * pair], sinks_ref[2 * pair + 1]) * LOG2E

    def dup(x):
        return jnp.where(keep_orig, x, pltpu.roll(x, B_HEAD_DIM, axis=1)).astype(BF16)

    @pl.when(lax.rem(pair, (B_HEADS // B_KV_HEADS) // 2) == 0)
    def _():
        def stage(n, _):
            rows = pl.ds(pl.multiple_of(jnp.asarray(n, jnp.int32) * w, w), w)
            kdup_ref[rows, :] = dup(k_ref[rows, :])
            vdup_ref[rows, :] = dup(v_ref[rows, :])
            return 0

        lax.fori_loop(0, n_blk, stage, 0, unroll=4)

    def q_block(n, first):
        start = n * w
        cur = pl.ds(start if first else pl.multiple_of(start, w), w)
        q = q_ref[cur, :] * (scale * LOG2E)
        zero = jnp.zeros_like(q)
        q2 = jnp.concatenate([jnp.where(low, q, zero), jnp.where(low, zero, q)], axis=0).astype(BF16)
        if first:
            k2 = jnp.concatenate([kdup_ref[cur, :], kdup_ref[cur, :]], axis=0)
            v2 = jnp.concatenate([vdup_ref[cur, :], vdup_ref[cur, :]], axis=0)
        else:
            both = pl.ds(pl.multiple_of(start - w, w), 2 * w)
            k2, v2 = kdup_ref[both, :], vdup_ref[both, :]
        s = lax.dot_general(q2, k2, (((1,), (1,)), ((), ())), preferred_element_type=F32)
        s = jnp.where(band & (col >= w) if first else band, s, NEG_INF)
        m = jnp.maximum(jnp.max(s, axis=1, keepdims=True), sink2)
        p = jnp.exp2(s - m)
        denom = jnp.sum(p, axis=1, keepdims=True) + jnp.exp2(sink2 - m)
        o2 = jnp.dot(p.astype(BF16), v2, preferred_element_type=F32) / denom
        o_ref[cur, :] = jnp.where(low, o2[:w], o2[w:]).astype(o_ref.dtype)

    for u in range(SWA_UNROLL):
        q_block(u, u == 0)

    def group(g, _):
        g = jnp.asarray(g, jnp.int32)
        for u in range(SWA_UNROLL):
            q_block(g * SWA_UNROLL + u, False)
        return 0

    lax.fori_loop(1, n_blk // SWA_UNROLL, group, 0)


def _swa_attention(z1, sinks, *, batch, seq):
    t = z1.shape[0]
    pairs = B_HEADS // 2
    rep = B_HEADS // B_KV_HEADS
    assert (seq // SWA_WINDOW) % SWA_UNROLL == 0
    kv_group = lambda p: (2 * p) // rep // 2
    return pl.pallas_call(
        functools.partial(_swa_kernel, scale=B_HEAD_DIM ** -0.5),
        grid_spec=pltpu.PrefetchScalarGridSpec(
            num_scalar_prefetch=1,
            grid=(batch, pairs),
            in_specs=[pl.BlockSpec((seq, LANES), lambda b, p, sinks: (b, OFF_QB // LANES + p)),
                      pl.BlockSpec((seq, LANES), lambda b, p, sinks: (b, OFF_KB // LANES + kv_group(p))),
                      pl.BlockSpec((seq, LANES), lambda b, p, sinks: (b, OFF_VB // LANES + kv_group(p)))],
            out_specs=pl.BlockSpec((seq, LANES), lambda b, p, sinks: (b, p)),
            scratch_shapes=[pltpu.VMEM((seq, LANES), BF16), pltpu.VMEM((seq, LANES), BF16)],
        ),
        out_shape=jax.ShapeDtypeStruct((t, B_WIDTH), BF16),
        compiler_params=_params("arbitrary", "arbitrary"),
        name="swa_attention",
    )(sinks, z1, z1, z1)


def _gated_out_kernel(oa_ref, ob_ref, oc_ref, ga_ref, gb_ref, gc_ref, wa_ref, wb_ref, wc_ref, y_ref,
                      wa_bf, wb_bf, wc_bf):
    @pl.when(pl.program_id(1) == 0)
    def _():
        wa_bf[...] = wa_ref[...].astype(BF16)
        wb_bf[...] = wb_ref[...].astype(BF16)
        wc_bf[...] = wc_ref[...].astype(BF16)

    y = jax.nn.sigmoid(ga_ref[...]) * jnp.dot(oa_ref[...], wa_bf[...], preferred_element_type=F32)
    y += jax.nn.sigmoid(gb_ref[...]) * jnp.dot(ob_ref[...], wb_bf[...], preferred_element_type=F32)
    y += jax.nn.sigmoid(gc_ref[...]) * jnp.dot(oc_ref[...], wc_bf[...], preferred_element_type=F32)
    y_ref[...] = y.astype(y_ref.dtype)


def _gated_out(oa, ob, oc, gates, w_out_a, w_out_b, w_out_c, layer, *, tm=512, tn=512):
    t = oa.shape[0]
    d = w_out_a.shape[2]
    nb = d // tn
    o_spec = lambda width: pl.BlockSpec((tm, width), lambda j, i: (i, 0))
    g_spec = lambda which: pl.BlockSpec((tm, tn), lambda j, i: (i, which * nb + j))
    w_spec = lambda width: pl.BlockSpec((None, width, tn), lambda j, i: (layer, 0, j))
    return pl.pallas_call(
        _gated_out_kernel,
        grid=(nb, t // tm),
        in_specs=[o_spec(A_WIDTH), o_spec(B_WIDTH), o_spec(C_WIDTH), g_spec(0), g_spec(1), g_spec(2),
                  w_spec(A_WIDTH), w_spec(B_WIDTH), w_spec(C_WIDTH)],
        out_specs=pl.BlockSpec((tm, tn), lambda j, i: (i, j)),
        out_shape=jax.ShapeDtypeStruct((t, d), BF16),
        scratch_shapes=[pltpu.VMEM((A_WIDTH, tn), BF16), pltpu.VMEM((B_WIDTH, tn), BF16),
                        pltpu.VMEM((C_WIDTH, tn), BF16)],
        compiler_params=_params("arbitrary", "arbitrary"),
        name="gated_out",
    )(oa, ob, oc, gates, gates, gates, w_out_a, w_out_b, w_out_c)


HI16 = -65536


def _bf16_bits_hi(v):
    return lax.bitcast_convert_type(v.astype(BF16).astype(F32), jnp.int32)


def _router_kernel(h_ref, g_ref, w_ref, wlo_ref, b_ref, ids_ref, wts_ref, packed_ref):
    x = h_ref[...]
    hn = x * lax.rsqrt(jnp.mean(x * x, axis=-1, keepdims=True) + NORM_EPS) * g_ref[...]
    half = hn.shape[1] // 2
    packed_ref[...] = lax.shift_right_logical(_bf16_bits_hi(hn[:, :half]), 16) | _bf16_bits_hi(hn[:, half:])
    x_hi = hn.astype(BF16)
    x_lo = (hn - x_hi.astype(F32)).astype(BF16)
    logits = (jnp.dot(x_hi, w_ref[...], preferred_element_type=F32)
              + jnp.dot(x_hi, wlo_ref[...], preferred_element_type=F32)
              + jnp.dot(x_lo, w_ref[...], preferred_element_type=F32)) + b_ref[...]
    lane = lax.broadcasted_iota(jnp.int32, logits.shape, 1)
    far = float(LANES)
    is_g = lane < N_GROUPS
    g_id = lane.astype(F32)
    gmax = jnp.max(jnp.where(is_g, logits, NEG_INF), axis=1, keepdims=True)
    gsel = jnp.min(jnp.where(is_g & (logits == gmax), g_id, far), axis=1, keepdims=True)
    p_g = 1.0 / jnp.sum(jnp.where(is_g, jnp.exp(logits - gmax), 0.0), axis=1, keepdims=True)
    e_lane = lane - N_GROUPS
    e_id = e_lane.astype(F32)
    e_group = jnp.right_shift(e_lane, 2).astype(F32)
    in_grp = (e_lane >= 0) & (e_lane < N_EXPERTS) & (e_group == gsel)
    emax = jnp.max(jnp.where(in_grp, logits, NEG_INF), axis=1, keepdims=True)
    ee = jnp.where(in_grp, jnp.exp(jnp.where(in_grp, logits, emax) - emax), 0.0)
    ep = ee / jnp.sum(ee, axis=1, keepdims=True)
    p1 = jnp.max(jnp.where(in_grp, ep, -1.0), axis=1, keepdims=True)
    i1 = jnp.min(jnp.where(in_grp & (ep == p1), e_id, far), axis=1, keepdims=True)
    rest = in_grp & (e_id != i1)
    p2 = jnp.max(jnp.where(rest, ep, -1.0), axis=1, keepdims=True)
    i2 = jnp.min(jnp.where(rest & (ep == p2), e_id, far), axis=1, keepdims=True)
    tot = p1 + p2
    ids_ref[...] = jnp.where(lane == 0, i1, jnp.where(lane == 1, i2, 0.0)).astype(jnp.int32)
    wts_ref[...] = jnp.where(lane == 0, p_g * p1 / tot, jnp.where(lane == 1, p_g * p2 / tot, 0.0))


def _router(h, g, w_r, b_r, layer, *, tm=256):
    t, d = h.shape
    out = pl.BlockSpec((tm, LANES), lambda i: (i, 0))
    return pl.pallas_call(
        _router_kernel,
        grid=(t // tm,),
        in_specs=[pl.BlockSpec((tm, d), lambda i: (i, 0)),
                  pl.BlockSpec((None, 1, d), lambda i: (layer, 0, 0)),
                  pl.BlockSpec((None, d, LANES), lambda i: (layer, 0, 0)),
                  pl.BlockSpec((None, d, LANES), lambda i: (layer, 0, 0)),
                  pl.BlockSpec((None, 1, LANES), lambda i: (layer, 0, 0))],
        out_specs=[out, out, pl.BlockSpec((tm, d // 2), lambda i: (i, 0))],
        out_shape=[jax.ShapeDtypeStruct((t, LANES), jnp.int32), jax.ShapeDtypeStruct((t, LANES), F32),
                   jax.ShapeDtypeStruct((t, d // 2), jnp.int32)],
        compiler_params=_params("parallel"),
        name="moe_router",
    )(h, g, *w_r, b_r)


def _row_copy(src_hbm, tok, dst_ref, r, sem):
    return pltpu.make_async_copy(src_hbm.at[pl.ds(tok, 1), :], dst_ref.at[pl.ds(r, 1), :], sem)


GATHER_UNROLL = 8
NORM_CHUNK = 16


def _dispatch_kernel(tok_ref, nvalid_ref, h_hbm, o_ref, rows_ref, sem):
    i = pl.program_id(0)
    n_valid = nvalid_ref[0]
    n_rows = rows_ref.shape[1]
    slot = lax.rem(i, 2)

    def fetch(blk, slot):
        base = blk * n_rows

        def issue(g, _):
            for u in range(GATHER_UNROLL):
                r = g * GATHER_UNROLL + u
                _row_copy(h_hbm, tok_ref[base + r], rows_ref.at[slot], r, sem.at[slot]).start(priority=u % 2)
            return 0

        lax.fori_loop(0, n_rows // GATHER_UNROLL, issue, 0)

    @pl.when(i == 0)
    def _():
        fetch(0, 0)

    @pl.when(i + 1 < n_valid)
    def _():
        fetch(i + 1, 1 - slot)

    @pl.when(i < n_valid)
    def _():
        def drain(r, _):
            _row_copy(h_hbm, 0, rows_ref.at[slot], r, sem.at[slot]).wait()
            return 0

        lax.fori_loop(0, n_rows, drain, 0, unroll=GATHER_UNROLL)

        half = rows_ref.shape[2]

        def unpack(c, _):
            rows = pl.ds(pl.multiple_of(jnp.asarray(c, jnp.int32) * NORM_CHUNK, NORM_CHUNK), NORM_CHUNK)
            word = rows_ref[slot, rows, :]
            o_ref[rows, :half] = lax.bitcast_convert_type(lax.shift_left(word, 16), F32).astype(o_ref.dtype)
            o_ref[rows, half:] = lax.bitcast_convert_type(word & HI16, F32).astype(o_ref.dtype)
            return 0

        lax.fori_loop(0, n_rows // NORM_CHUNK, unpack, 0, unroll=4)


def _dispatch(packed, buf_tok, nvalid, *, n_blocks):
    t, half = packed.shape
    blk = lambda i, tok, nv: (jnp.minimum(i, nv[0] - 1), 0)
    return pl.pallas_call(
        _dispatch_kernel,
        grid_spec=pltpu.PrefetchScalarGridSpec(
            num_scalar_prefetch=2,
            grid=(n_blocks,),
            in_specs=[pl.BlockSpec(memory_space=pl.ANY)],
            out_specs=pl.BlockSpec((MOE_ROWS, 2 * half), blk),
            scratch_shapes=[pltpu.VMEM((2, MOE_ROWS, half), jnp.int32), pltpu.SemaphoreType.DMA((2,))],
        ),
        out_shape=jax.ShapeDtypeStruct((n_blocks * MOE_ROWS, 2 * half), BF16),
        compiler_params=_params("arbitrary"),
        name="moe_dispatch",
    )(buf_tok, nvalid, packed)


def _run_weights(be_ref, first_ref, next_ref, i, pass_id, n_passes, n_blocks, copies, stage):
    @pl.when((pass_id == 0) & (i == 0))
    def _():
        for c in copies(be_ref[0], 0):
            c.start()

    @pl.when(first_ref[i] == 1)
    def _():
        for c in copies(be_ref[i], pass_id):
            c.wait()
        stage()
        nxt = next_ref[i]
        same_pass = nxt < n_blocks
        e_next = jnp.where(same_pass, be_ref[jnp.minimum(nxt, n_blocks - 1)], be_ref[0])
        pass_next = jnp.where(same_pass, pass_id, pass_id + 1)

        @pl.when(same_pass | (pass_id + 1 < n_passes))
        def _():
            for c in copies(e_next, pass_next):
                c.start()


def _expert_up_kernel(be_ref, nvalid_ref, first_ref, next_ref, x_ref, wg_hbm, wu_hbm, o_ref,
                      wg_land, wu_land, wg_bf, wu_bf, sem, *, layer):
    f, i = pl.program_id(0), pl.program_id(1)
    tf = wg_land.shape[1]

    def copies(e, ff_tile):
        cols = pl.ds(pl.multiple_of(ff_tile * tf, tf), tf)
        return (pltpu.make_async_copy(wg_hbm.at[layer, e, :, cols], wg_land, sem.at[0]),
                pltpu.make_async_copy(wu_hbm.at[layer, e, :, cols], wu_land, sem.at[1]))

    def stage():
        wg_bf[...] = wg_land[...].astype(BF16)
        wu_bf[...] = wu_land[...].astype(BF16)

    _run_weights(be_ref, first_ref, next_ref, i, f, pl.num_programs(0), pl.num_programs(1), copies, stage)

    @pl.when(i < nvalid_ref[0])
    def _():
        x = x_ref[...]
        hg = jnp.dot(x, wg_bf[...], preferred_element_type=F32)
        hu = jnp.dot(x, wu_bf[...], preferred_element_type=F32)
        o_ref[...] = (jax.nn.silu(hg) * hu).astype(o_ref.dtype)


def _expert_up(xb, w_gate, w_up, layer, runs, *, n_blocks):
    d = xb.shape[1]
    ff = w_gate.shape[3]
    tf = MOE_FF_TILE
    row_blk = lambda f, i, be, nv, first, nxt: (jnp.minimum(i, nv[0] - 1), 0)
    return pl.pallas_call(
        functools.partial(_expert_up_kernel, layer=layer),
        grid_spec=pltpu.PrefetchScalarGridSpec(
            num_scalar_prefetch=4,
            grid=(ff // tf, n_blocks),
            in_specs=[pl.BlockSpec((MOE_ROWS, d), row_blk), pl.BlockSpec(memory_space=pl.ANY),
                      pl.BlockSpec(memory_space=pl.ANY)],
            out_specs=pl.BlockSpec((MOE_ROWS, tf),
                                   lambda f, i, be, nv, first, nxt: (jnp.minimum(i, nv[0] - 1), f)),
            scratch_shapes=[pltpu.VMEM((d, tf), F32), pltpu.VMEM((d, tf), F32),
                            pltpu.VMEM((d, tf), BF16), pltpu.VMEM((d, tf), BF16),
                            pltpu.SemaphoreType.DMA((2,))],
        ),
        out_shape=jax.ShapeDtypeStruct((n_blocks * MOE_ROWS, ff), BF16),
        compiler_params=_params("arbitrary", "arbitrary"),
        name="moe_expert_up",
    )(*runs, xb, w_gate, w_up)


def _expert_down_kernel(be_ref, nvalid_ref, first_ref, next_ref, a_ref, wd_hbm, o_ref, wd_land, wd_bf, sem,
                        *, layer):
    i = pl.program_id(0)

    def copies(e, _):
        return (pltpu.make_async_copy(wd_hbm.at[layer, e], wd_land, sem.at[0]),)

    def stage():
        wd_bf[...] = wd_land[...].astype(BF16)

    _run_weights(be_ref, first_ref, next_ref, i, 0, 1, pl.num_programs(0), copies, stage)

    @pl.when(i < nvalid_ref[0])
    def _():
        o_ref[...] = jnp.dot(a_ref[...], wd_bf[...], preferred_element_type=F32)


def _expert_down(act, w_down, layer, runs, *, n_blocks):
    ff, d = w_down.shape[2], w_down.shape[3]
    row_blk = lambda i, be, nv, first, nxt: (jnp.minimum(i, nv[0] - 1), 0)
    return pl.pallas_call(
        functools.partial(_expert_down_kernel, layer=layer),
        grid_spec=pltpu.PrefetchScalarGridSpec(
            num_scalar_prefetch=4,
            grid=(n_blocks,),
            in_specs=[pl.BlockSpec((MOE_ROWS, ff), row_blk), pl.BlockSpec(memory_space=pl.ANY)],
            out_specs=pl.BlockSpec((MOE_ROWS, d), row_blk),
            scratch_shapes=[pltpu.VMEM((ff, d), F32), pltpu.VMEM((ff, d), BF16), pltpu.SemaphoreType.DMA((1,))],
        ),
        out_shape=jax.ShapeDtypeStruct((n_blocks * MOE_ROWS, d), F32),
        compiler_params=_params("arbitrary"),
        name="moe_expert_down",
    )(*runs, act, w_down)


def _combine_kernel(pos_ref, h_ref, wts_ref, g_ref, y_hbm, *refs, emit_h):
    o_ref = refs[0] if emit_h else None
    n_ref, y0_ref, y1_ref, sem = refs[int(emit_h):]
    i = pl.program_id(0)
    tm = h_ref.shape[0]
    slot = lax.rem(i, 2)

    def fetch(blk, slot):
        base = blk * tm * EXPERT_TOPK

        def issue(g, _):
            for u in range(GATHER_UNROLL):
                r = g * GATHER_UNROLL + u
                _row_copy(y_hbm, pos_ref[base + EXPERT_TOPK * r], y0_ref.at[slot], r, sem.at[slot]).start(priority=0)
                _row_copy(y_hbm, pos_ref[base + EXPERT_TOPK * r + 1], y1_ref.at[slot], r,
                          sem.at[slot]).start(priority=1)
            return 0

        lax.fori_loop(0, tm // GATHER_UNROLL, issue, 0)

    @pl.when(i == 0)
    def _():
        fetch(0, 0)

    @pl.when(i + 1 < pl.num_programs(0))
    def _():
        fetch(i + 1, 1 - slot)

    def drain(r, _):
        _row_copy(y_hbm, 0, y0_ref.at[slot], r, sem.at[slot]).wait()
        _row_copy(y_hbm, 0, y1_ref.at[slot], r, sem.at[slot]).wait()
        return 0

    lax.fori_loop(0, tm, drain, 0, unroll=GATHER_UNROLL)

    def rows_chunk(c, _):
        rows = pl.ds(pl.multiple_of(jnp.asarray(c, jnp.int32) * NORM_CHUNK, NORM_CHUNK), NORM_CHUNK)
        w = wts_ref[rows, :]
        hv = h_ref[rows, :] + w[:, 0:1] * y0_ref[slot, rows, :] + w[:, 1:2] * y1_ref[slot, rows, :]
        if emit_h:
            o_ref[rows, :] = hv
        hn = hv * lax.rsqrt(jnp.mean(hv * hv, axis=-1, keepdims=True) + NORM_EPS) * g_ref[...]
        n_ref[rows, :] = hn.astype(n_ref.dtype)
        return 0

    lax.fori_loop(0, tm // NORM_CHUNK, rows_chunk, 0, unroll=4)


def _combine(h, wts, y, pos, g, *, norm_dtype, emit_h, tm=128):
    t, d = h.shape
    row = pl.BlockSpec((tm, d), lambda i, pos: (i, 0))
    outs = pl.pallas_call(
        functools.partial(_combine_kernel, emit_h=emit_h),
        grid_spec=pltpu.PrefetchScalarGridSpec(
            num_scalar_prefetch=1,
            grid=(t // tm,),
            in_specs=[row, pl.BlockSpec((tm, LANES), lambda i, pos: (i, 0)),
                      pl.BlockSpec((1, d), lambda i, pos: (0, 0)), pl.BlockSpec(memory_space=pl.ANY)],
            out_specs=[row] * (1 + emit_h),
            scratch_shapes=[pltpu.VMEM((2, tm, d), F32), pltpu.VMEM((2, tm, d), F32),
                            pltpu.SemaphoreType.DMA((2,))],
        ),
        out_shape=([jax.ShapeDtypeStruct((t, d), F32)] if emit_h else [])
        + [jax.ShapeDtypeStruct((t, d), norm_dtype)],
        compiler_params=_params("arbitrary"),
        name="moe_combine",
    )(pos, h, wts, g.reshape(1, d), y)
    return (outs[0], outs[1]) if emit_h else (None, outs[0])


def _hier_moe(h, ffn_norm_g, w_r, b_r, w_gate, w_up, w_down, layer, next_norm_g, *, norm_dtype, emit_h):
    t, d = h.shape
    tk = t * EXPERT_TOPK
    n_blocks = tk // MOE_ROWS + N_EXPERTS
    ids, wts, hn_packed = _router(h, ffn_norm_g, w_r, b_r, layer)
    flat_e = ids[:, :EXPERT_TOPK].reshape(tk)
    onehot = (flat_e[:, None] == jnp.arange(N_EXPERTS, dtype=jnp.int32)[None, :]).astype(jnp.int32)
    csum = jnp.cumsum(onehot, axis=0)
    rank = jnp.sum(onehot * csum, axis=1) - 1
    counts = csum[-1]
    padded = (counts + MOE_ROWS - 1) // MOE_ROWS * MOE_ROWS
    pad_end = jnp.cumsum(padded)
    pad_start = pad_end - padded
    dest = (pad_start[flat_e] + rank).astype(jnp.int32)
    flat_tok = jnp.arange(tk, dtype=jnp.int32) // EXPERT_TOPK
    buf_tok = jnp.zeros((n_blocks * MOE_ROWS,), jnp.int32).at[dest].set(flat_tok)
    nvalid = (pad_end[-1] // MOE_ROWS).astype(jnp.int32).reshape(1)
    blk_start = jnp.minimum(jnp.arange(n_blocks, dtype=jnp.int32), nvalid[0] - 1) * MOE_ROWS
    block_expert = jnp.minimum(jnp.searchsorted(pad_end, blk_start, side='right'), N_EXPERTS - 1).astype(jnp.int32)

    blk = jnp.arange(n_blocks, dtype=jnp.int32)
    prev_expert = jnp.concatenate([jnp.full((1,), -1, jnp.int32), block_expert[:-1]])
    run_first = (blk < nvalid[0]) & (block_expert != prev_expert)
    first_at_or_after = lax.cummin(jnp.where(run_first, blk, n_blocks), reverse=True)
    run_next = jnp.concatenate([first_at_or_after[1:], jnp.full((1,), n_blocks, jnp.int32)])
    runs = (block_expert, nvalid, run_first.astype(jnp.int32), run_next)

    xb = _dispatch(hn_packed, buf_tok, nvalid, n_blocks=n_blocks)
    act = _expert_up(xb, w_gate, w_up, layer, runs, n_blocks=n_blocks)
    y = _expert_down(act, w_down, layer, runs, n_blocks=n_blocks)
    return _combine(h, wts, y, dest, next_norm_g, norm_dtype=norm_dtype, emit_h=emit_h)


ROPE_A, ROPE_B, ROPE_PE = 0, 1, 2


def _in_proj_rope_table(col):
    if OFF_QA <= col < OFF_VA:
        return ROPE_A
    if OFF_QB <= col < OFF_VB:
        return ROPE_B
    if col == OFF_KPE:
        return ROPE_PE
    return None


def _mla_q_rope_table(col):
    return 0 if (col // LANES) % 2 == 1 else None


def kernel(x, positions, attn_norm_g, w_in, q_norm_g, kv_norm_g, wq_b, wkv_b, sinks, w_out_a, w_out_b,
           w_out_c, w_o, ffn_norm_g, w_group, b_group, w_expert, b_expert, w_gate, w_up, w_down,
           final_norm_g):
    batch, seq, d = x.shape
    depth = w_in.shape[0]
    t = batch * seq
    h = x.reshape(t, d)
    pos = positions.reshape(t).astype(F32)

    w_in_t = jnp.swapaxes(w_in, 1, 2)
    q_head = C_NOPE_DIM + C_ROPE_DIM
    wq_pad = jnp.pad(wq_b.reshape(depth, C_Q_RANK, C_HEADS, q_head),
                     ((0, 0), (0, 0), (0, 0), (0, 2 * LANES - q_head))
                     ).reshape(depth, C_Q_RANK, C_HEADS * 2 * LANES)
    wq_pad = (wq_pad * MLA_Q_FACTOR).astype(BF16)
    w_r = jnp.concatenate([w_group, w_expert,
                           jnp.zeros((depth, d, LANES - N_GROUPS - N_EXPERTS), F32)], axis=2)
    w_r_hi = w_r.astype(BF16)
    w_r = (w_r_hi, (w_r - w_r_hi.astype(F32)).astype(BF16))
    b_r = jnp.concatenate([b_group, b_expert,
                           jnp.zeros((depth, LANES - N_GROUPS - N_EXPERTS), F32)], axis=1).reshape(depth, 1, LANES)
    ffn_g = ffn_norm_g.reshape(depth, 1, d)

    c_a, s_a = _rope_tables(pos, A_HEAD_DIM)
    c_b, s_b = _rope_tables(pos, B_HEAD_DIM)
    c_pe, s_pe = _rope_tables(pos, C_ROPE_DIM, keep_lanes=C_ROPE_DIM)
    rope_tables = ((c_a, s_a, A_HEAD_DIM // 2), (c_b, s_b, B_HEAD_DIM // 2), (c_pe, s_pe, C_ROPE_DIM // 2))

    hn = _rmsnorm(h, attn_norm_g[0])
    for l in range(depth):
        z1 = _matmul(hn, w_in_t, l, n_out=Z1_WIDTH, w_is_nk=True, rope_tables=rope_tables,
                     rope_group_table=_in_proj_rope_table, name="in_proj")
        gates = _matmul(hn, w_in_t, l, n_out=3 * d, col_off=OFF_GATES, w_is_nk=True, name="gate_proj")
        o_a = _moba_attention(z1, batch=batch, seq=seq)
        o_b = _swa_attention(z1, sinks[l], batch=batch, seq=seq)
        cq_n = _rmsnorm(z1, q_norm_g[l], col_off=OFF_CQ, width=C_Q_RANK, piece=512)
        ckv_n = _rmsnorm(z1, kv_norm_g[l], col_off=OFF_CKV, width=C_KV_RANK, piece=512)
        q_c = _matmul(cq_n, wq_pad, l, n_out=C_HEADS * 2 * LANES, tn=1024, out_dtype=BF16,
                      rope_tables=rope_tables[ROPE_PE:], rope_group_table=_mla_q_rope_table, name="mla_q_proj")
        kv_c = _matmul(ckv_n, wkv_b, l, n_out=C_HEADS * (C_NOPE_DIM + C_V_DIM), tn=1024, out_dtype=BF16,
                       name="mla_kv_proj")
        o_c = _mla_attention(q_c, kv_c, z1, batch=batch, seq=seq)
        y = _gated_out(o_a, o_b, o_c, gates, w_out_a, w_out_b, w_out_c, l)
        h = _matmul(y, w_o, l, n_out=d, residual=h, name="out_proj")
        last = l == depth - 1
        h, hn = _hier_moe(h, ffn_g, w_r, b_r, w_gate, w_up, w_down, l,
                          final_norm_g if last else attn_norm_g[l + 1],
                          norm_dtype=F32 if last else BF16, emit_h=not last)
    return hn.reshape(batch, seq, d)
```

```python
import functools

import jax
import jax.numpy as jnp
from jax import lax
from jax.experimental import pallas as pl
from jax.experimental.pallas import tpu as pltpu

F32 = jnp.float32
BF16 = jnp.bfloat16

ROPE_THETA = 10000.0
NORM_EPS = 1e-6
NEG_INF = -1e30
PICKED = -3e38
LOG2E = 1.4426950408889634

A_HEADS, A_HEAD_DIM = 16, 128
MOBA_BLOCK, MOBA_TOPK = 256, 3
B_HEADS, B_KV_HEADS, B_HEAD_DIM, SWA_WINDOW = 32, 4, 64, 128
C_HEADS, C_Q_RANK, C_KV_RANK, C_NOPE_DIM, C_ROPE_DIM, C_V_DIM = 16, 1024, 512, 128, 64, 128
N_GROUPS, EXPERTS_PER_GROUP, EXPERT_TOPK, D_FF_EXPERT = 8, 4, 2, 768
N_EXPERTS = N_GROUPS * EXPERTS_PER_GROUP
assert EXPERTS_PER_GROUP == 4

LANES = 128
VMEM_LIMIT_BYTES = 58 * 1024 * 1024

A_WIDTH = A_HEADS * A_HEAD_DIM
B_WIDTH = B_HEADS * B_HEAD_DIM
B_KV_WIDTH = B_KV_HEADS * B_HEAD_DIM
C_WIDTH = C_HEADS * C_V_DIM
OFF_QA = 0
OFF_KA = OFF_QA + A_WIDTH
OFF_VA = OFF_KA + A_WIDTH
OFF_QB = OFF_VA + A_WIDTH
OFF_KB = OFF_QB + B_WIDTH
OFF_VB = OFF_KB + B_KV_WIDTH
OFF_CQ = OFF_VB + B_KV_WIDTH
OFF_CKV = OFF_CQ + C_Q_RANK
OFF_KPE = OFF_CKV + C_KV_RANK
OFF_GATES = OFF_KPE + C_ROPE_DIM
Z1_WIDTH = 10752

MOE_ROWS = 256
MOE_FF_TILE = 768
SWA_UNROLL = 8


def _params(*sem):
    return pltpu.CompilerParams(dimension_semantics=sem, vmem_limit_bytes=VMEM_LIMIT_BYTES)


def _rmsnorm_kernel(*refs, n_pieces, width):
    x_refs, g_ref, o_ref = refs[:n_pieces], refs[n_pieces], refs[n_pieces + 1]
    xs = [r[...].astype(F32) for r in x_refs]
    ss = sum(jnp.sum(x * x, axis=-1, keepdims=True) for x in xs)
    inv = lax.rsqrt(ss * (1.0 / width) + NORM_EPS)
    pw = xs[0].shape[1]
    for p, x in enumerate(xs):
        o_ref[:, p * pw:(p + 1) * pw] = (x * inv * g_ref[:, p * pw:(p + 1) * pw]).astype(o_ref.dtype)


def _rmsnorm(x, g, *, col_off=0, width=None, piece=None, tm=256, out_dtype=BF16):
    t = x.shape[0]
    width = width or x.shape[1]
    piece = piece or width
    n_pieces = width // piece
    off = col_off // piece
    assert col_off % piece == 0 and width % piece == 0
    in_specs = [pl.BlockSpec((tm, piece), functools.partial(lambda i, p: (i, off + p), p=p))
                for p in range(n_pieces)]
    in_specs.append(pl.BlockSpec((1, width), lambda i: (0, 0)))
    return pl.pallas_call(
        functools.partial(_rmsnorm_kernel, n_pieces=n_pieces, width=width),
        grid=(t // tm,),
        in_specs=in_specs,
        out_specs=pl.BlockSpec((tm, width), lambda i: (i, 0)),
        out_shape=jax.ShapeDtypeStruct((t, width), out_dtype),
        compiler_params=_params("parallel"),
        name="rmsnorm",
    )(*([x] * n_pieces), g.reshape(1, width))


def _stage_weight(w_ref, wbf_ref, w_is_nk):
    if not w_is_nk:
        wbf_ref[...] = w_ref[...].astype(BF16)
        return
    _, tn, k = w_ref.shape
    for c in range(k // tn):
        wbf_ref[c * tn:(c + 1) * tn, :] = w_ref[0, :, c * tn:(c + 1) * tn].T.astype(BF16)


def _swap_halves(x, half):
    if half == 64:
        return pltpu.roll(x, 64, axis=1)
    lane = lax.broadcasted_iota(jnp.int32, x.shape, 1)
    return jnp.where((lane & half) == 0, pltpu.roll(x, LANES - half, axis=1), pltpu.roll(x, half, axis=1))


def _matmul_kernel(*refs, stage_w, w_is_nk, has_res, rope_modes, rope_halves):
    x_ref, w_ref = refs[0], refs[1]
    n_in = 2 + has_res + 2 * len(rope_halves)
    r_ref = refs[2] if has_res else None
    table_refs = refs[2 + has_res:n_in]
    o_ref = refs[n_in]
    if stage_w:
        wbf_ref = refs[n_in + 1]

        @pl.when(pl.program_id(1) == 0)
        def _():
            _stage_weight(w_ref, wbf_ref, w_is_nk)

        w_src = wbf_ref
    else:
        w_src = w_ref

    def product():
        acc = jnp.dot(x_ref[...], w_src[...], preferred_element_type=F32)
        return acc + r_ref[...] if has_res else acc

    if rope_modes is None:
        o_ref[...] = product().astype(o_ref.dtype)
        return
    j = pl.program_id(0)
    for pattern in sorted(set(rope_modes), key=str):
        blocks = [b for b, p in enumerate(rope_modes) if p == pattern]

        @pl.when(functools.reduce(jnp.logical_or, [j == b for b in blocks]))
        def _(pattern=pattern):
            acc = product()
            for g, table in enumerate(pattern):
                xg = acc[:, g * LANES:(g + 1) * LANES]
                if table is not None:
                    c_ref, s_ref = table_refs[2 * table], table_refs[2 * table + 1]
                    xg = xg * c_ref[...] + _swap_halves(xg, rope_halves[table]) * s_ref[...]
                o_ref[:, g * LANES:(g + 1) * LANES] = xg.astype(o_ref.dtype)


def _matmul(x, w, layer, *, n_out, col_off=0, w_is_nk=False, tm=1024, tn=512, out_dtype=F32, residual=None,
            rope_tables=(), rope_group_table=None, name="matmul"):
    m, k = x.shape
    assert w.shape[2 if w_is_nk else 1] == k and m % tm == 0 and n_out % tn == 0
    stage_w = w_is_nk or w.dtype != BF16
    rope_modes = None
    if rope_group_table is not None:
        rope_modes = tuple(tuple(rope_group_table(b * tn + g * LANES) for g in range(tn // LANES))
                           for b in range(n_out // tn))
    if w_is_nk:
        assert k % tn == 0 and col_off % 8 == 0
        w_spec = pl.BlockSpec((pl.Element(1), pl.Element(tn), pl.Element(k)),
                              lambda j, i: (layer, pl.multiple_of(col_off + j * tn, 8), 0))
    else:
        assert col_off % tn == 0
        w_spec = pl.BlockSpec((None, k, tn), lambda j, i: (layer, 0, j + col_off // tn))
    in_specs = [pl.BlockSpec((tm, k), lambda j, i: (i, 0)), w_spec]
    args = [x, w]
    if residual is not None:
        in_specs.append(pl.BlockSpec((tm, tn), lambda j, i: (i, j)))
        args.append(residual)
    for c, s, _ in rope_tables:
        in_specs += [pl.BlockSpec((tm, LANES), lambda j, i: (i, 0))] * 2
        args += [c, s]
    return pl.pallas_call(
        functools.partial(_matmul_kernel, stage_w=stage_w, w_is_nk=w_is_nk, has_res=residual is not None,
                          rope_modes=rope_modes, rope_halves=tuple(half for _, _, half in rope_tables)),
        grid=(n_out // tn, m // tm),
        in_specs=in_specs,
        out_specs=pl.BlockSpec((tm, tn), lambda j, i: (i, j)),
        out_shape=jax.ShapeDtypeStruct((m, n_out), out_dtype),
        scratch_shapes=[pltpu.VMEM((k, tn), BF16)] if stage_w else [],
        compiler_params=_params("arbitrary", "arbitrary"),
        name=name,
    )(*args)


def _rope_tables_kernel(pos_ref, invf_ref, sign_ref, keep_ref, c_ref, s_ref):
    ang = pos_ref[...] * invf_ref[...]
    c_ref[...] = jnp.cos(ang) * keep_ref[...]
    s_ref[...] = jnp.sin(ang) * sign_ref[...]


def _rope_tables(pos, dim, *, keep_lanes=LANES):
    t = pos.shape[0]
    half = dim // 2
    lane = jnp.arange(LANES)
    invf = (ROPE_THETA ** (-(2.0 * (lane % half)).astype(F32) / dim)).reshape(1, LANES)
    keep = (lane < keep_lanes).astype(F32).reshape(1, LANES)
    sign = jnp.where((lane % dim) < half, -1.0, 1.0).astype(F32).reshape(1, LANES) * keep
    tm = 1024
    row = pl.BlockSpec((1, LANES), lambda i: (0, 0))
    out = pl.BlockSpec((tm, LANES), lambda i: (i, 0))
    return pl.pallas_call(
        _rope_tables_kernel,
        grid=(t // tm,),
        in_specs=[pl.BlockSpec((tm, 1), lambda i: (i, 0)), row, row, row],
        out_specs=[out, out],
        out_shape=[jax.ShapeDtypeStruct((t, LANES), F32)] * 2,
        compiler_params=_params("parallel"),
        name="rope_tables",
    )(pos.reshape(t, 1), invf, sign, keep)


def _online_block(q, k, v, s_mask, carry):
    m, l, acc = carry
    s = s_mask(lax.dot_general(q, k, (((1,), (1,)), ((), ())), preferred_element_type=F32))
    m_new = jnp.maximum(m, jnp.max(s, axis=1, keepdims=True))
    alpha = jnp.exp2(m - m_new)
    p = jnp.exp2(s - m_new)
    l = alpha * l + jnp.sum(p, axis=1, keepdims=True)
    acc = alpha * acc + jnp.dot(p.astype(BF16), v, preferred_element_type=F32)
    return m_new, l, acc


def _causal_attend(qi, qs, load_kvs, tq, dv, diag_mask, past_mask):
    heads = range(len(qs))
    init = (jnp.full((tq, 1), NEG_INF, F32), jnp.zeros((tq, 1), F32), jnp.zeros((tq, dv), F32))
    kvs = load_kvs(qi)
    carries = tuple(_online_block(qs[h], *kvs[h], functools.partial(diag_mask, h), init) for h in heads)

    def past(j, carries):
        j = jnp.asarray(j, jnp.int32)
        kvs = load_kvs(j)
        return tuple(_online_block(qs[h], *kvs[h], functools.partial(past_mask, h, j), carries[h])
                     for h in heads)

    carries = lax.fori_loop(0, qi, past, carries)
    return [acc / l for _, l, acc in carries]


def _mla_kernel(q_ref, kv_ref, kpe_ref, o_ref, *, tq, heads):
    n_q = q_ref.shape[0] // tq
    qw, dv = 2 * LANES, C_V_DIM
    row = lax.broadcasted_iota(jnp.int32, (tq, tq), 0)
    col = lax.broadcasted_iota(jnp.int32, (tq, tq), 1)
    causal = col <= row

    def load_kvs(j):
        rows = pl.ds(pl.multiple_of(j * tq, tq), tq)
        kpe = kpe_ref[rows, :].astype(BF16)
        return [(jnp.concatenate([kv_ref[rows, h * qw:h * qw + C_NOPE_DIM], kpe], axis=1),
                 kv_ref[rows, h * qw + C_NOPE_DIM:(h + 1) * qw]) for h in range(heads)]

    def q_tile(qi, _):
        qi = jnp.asarray(qi, jnp.int32)
        rows = pl.ds(pl.multiple_of(qi * tq, tq), tq)
        qs = [q_ref[rows, h * qw:(h + 1) * qw] for h in range(heads)]
        outs = _causal_attend(qi, qs, load_kvs, tq, dv,
                              lambda h, s: jnp.where(causal, s, NEG_INF), lambda h, j, s: s)
        for h in range(heads):
            o_ref[rows, h * dv:(h + 1) * dv] = outs[h].astype(o_ref.dtype)
        return 0

    lax.fori_loop(0, n_q, q_tile, 0)


MLA_Q_FACTOR = (C_NOPE_DIM + C_ROPE_DIM) ** -0.5 * LOG2E


def _mla_attention(q, kv, z1, *, batch, seq, tq=1024, heads=2):
    t = q.shape[0]
    wide = pl.BlockSpec((seq, heads * 2 * LANES), lambda b, g: (b, g))
    return pl.pallas_call(
        functools.partial(_mla_kernel, tq=tq, heads=heads),
        grid=(batch, C_HEADS // heads),
        in_specs=[wide, wide, pl.BlockSpec((seq, LANES), lambda b, g: (b, OFF_KPE // LANES))],
        out_specs=pl.BlockSpec((seq, heads * C_V_DIM), lambda b, g: (b, g)),
        out_shape=jax.ShapeDtypeStruct((t, C_WIDTH), BF16),
        compiler_params=_params("parallel", "parallel"),
        name="mla_attention",
    )(q, kv, z1)


def _moba_kernel(q_ref, k_ref, v_ref, o_ref, kmean_ref, *, tq, heads, scale):
    bs, d = MOBA_BLOCK, A_HEAD_DIM
    n_blk = q_ref.shape[0] // bs
    per_tile = tq // bs
    for h in range(heads):
        for n in range(n_blk):
            kmean_ref[h, n:n + 1, :] = jnp.mean(k_ref[n * bs:(n + 1) * bs, h * d:(h + 1) * d], axis=0, keepdims=True)
    blk_i = lax.broadcasted_iota(jnp.int32, (n_blk, tq), 0)
    blk = blk_i.astype(F32)
    bit_of_blk = lax.shift_left(jnp.ones_like(blk_i), blk_i).astype(F32)
    lane_q = lax.broadcasted_iota(jnp.int32, (1, tq), 1)
    lane_sub = sum(((lane_q >= c * bs).astype(F32) for c in range(1, per_tile)), jnp.zeros((1, tq), F32))
    row1 = lax.broadcasted_iota(jnp.int32, (tq, 1), 0)
    row_sub = sum(((row1 >= c * bs).astype(F32) for c in range(1, per_tile)), jnp.zeros((tq, 1), F32))
    col_local = lax.broadcasted_iota(jnp.int32, (tq, bs), 1)

    def load_kvs(j):
        rows = pl.ds(pl.multiple_of(j * tq, tq), tq)
        return [(k_ref[rows, h * d:(h + 1) * d].astype(BF16), v_ref[rows, h * d:(h + 1) * d].astype(BF16))
                for h in range(heads)]

    def q_tile(qi, _):
        qi = jnp.asarray(qi, jnp.int32)
        rows = pl.ds(pl.multiple_of(qi * tq, tq), tq)
        first = qi * per_tile
        own = first.astype(F32) + lane_sub
        qs, sels = [], []
        for h in range(heads):
            qf = q_ref[rows, h * d:(h + 1) * d]
            gate = lax.dot_general(kmean_ref[h], qf, (((1,), (1,)), ((), ())),
                                   precision=lax.Precision.HIGHEST, preferred_element_type=F32)
            gate = jnp.where(blk < own, gate, NEG_INF)
            sel = jnp.zeros((n_blk, tq), F32)
            for _ in range(MOBA_TOPK):
                best = jnp.max(gate, axis=0, keepdims=True)
                idx = jnp.min(jnp.where(gate == best, blk, float(n_blk)), axis=0, keepdims=True)
                pick = blk == idx
                sel = jnp.where(pick & (blk < own), 1.0, sel)
                gate = jnp.where(pick, PICKED, gate)
            qs.append((qf * (scale * LOG2E)).astype(BF16))
            code = jnp.sum(sel * bit_of_blk, axis=0, keepdims=True)
            code_col = jnp.max(jnp.broadcast_to(code, (LANES, tq)).T, axis=1, keepdims=True)
            sels.append(code_col.astype(jnp.int32))

        def chosen(h, b):
            return (lax.shift_right_logical(sels[h], jnp.broadcast_to(b, sels[h].shape)) & 1) == 1

        def by_block(s, mask_block):
            parts = [mask_block(c, s[:, c * bs:(c + 1) * bs]) for c in range(per_tile)]
            return parts[0] if per_tile == 1 else jnp.concatenate(parts, axis=1)

        def diag_mask(h, s):
            def mask_block(c, sc):
                limit = jnp.where(row_sub == c, row1 - c * bs + 1,
                                  jnp.where((row_sub > c) & chosen(h, first + c), bs, 0))
                return jnp.where(col_local < limit, sc, NEG_INF)
            return by_block(s, mask_block)

        def past_mask(h, j, s):
            jb = j * per_tile
            return by_block(s, lambda c, sc: jnp.where(chosen(h, jb + c), sc, NEG_INF))

        outs = _causal_attend(qi, qs, load_kvs, tq, d, diag_mask, past_mask)
        for h in range(heads):
            o_ref[rows, h * d:(h + 1) * d] = outs[h].astype(o_ref.dtype)
        return 0

    lax.fori_loop(0, q_ref.shape[0] // tq, q_tile, 0)


def _moba_attention(z1, *, batch, seq, tq=1024, heads=2):
    t = z1.shape[0]
    assert seq % tq == 0 and tq % MOBA_BLOCK == 0
    n_groups = A_HEADS // heads
    wide = heads * A_HEAD_DIM
    assert OFF_QA == 0 and OFF_KA == A_WIDTH and OFF_VA == 2 * A_WIDTH
    return pl.pallas_call(
        functools.partial(_moba_kernel, tq=tq, heads=heads, scale=A_HEAD_DIM ** -0.5),
        grid=(batch, n_groups),
        in_specs=[pl.BlockSpec((seq, wide), lambda b, g: (b, g)),
                  pl.BlockSpec((seq, wide), lambda b, g: (b, n_groups + g)),
                  pl.BlockSpec((seq, wide), lambda b, g: (b, 2 * n_groups + g))],
        out_specs=pl.BlockSpec((seq, wide), lambda b, g: (b, g)),
        out_shape=jax.ShapeDtypeStruct((t, A_WIDTH), BF16),
        scratch_shapes=[pltpu.VMEM((heads, seq // MOBA_BLOCK, A_HEAD_DIM), F32)],
        compiler_params=_params("parallel", "parallel"),
        name="moba_attention",
    )(z1, z1, z1)


def _swa_kernel(sinks_ref, q_ref, k_ref, v_ref, o_ref, kdup_ref, vdup_ref, *, scale):
    w = SWA_WINDOW
    n_blk = q_ref.shape[0] // w
    pair = pl.program_id(1)
    kv_odd = ((pair * 2) // (B_HEADS // B_KV_HEADS)) % 2
    lane = lax.broadcasted_iota(jnp.int32, (w, LANES), 1)
    low = lane < B_HEAD_DIM
    keep_orig = jnp.where(low, 0, 1) == kv_odd
    row = lax.broadcasted_iota(jnp.int32, (2 * w, 2 * w), 0)
    col = lax.broadcasted_iota(jnp.int32, (2 * w, 2 * w), 1)
    rel = (row & (w - 1)) + w - col
    band = (rel >= 0) & (rel < w)
    sink2 = jnp.where(lax.broadcasted_iota(jnp.int32, (2 * w, 1), 0) < w,
                      sinks_ref[2 * pair], sinks_ref[2 * pair + 1]) * LOG2E

    def dup(x):
        return jnp.where(keep_orig, x, pltpu.roll(x, B_HEAD_DIM, axis=1)).astype(BF16)

    @pl.when(lax.rem(pair, (B_HEADS // B_KV_HEADS) // 2) == 0)
    def _():
        def stage(n, _):
            rows = pl.ds(pl.multiple_of(jnp.asarray(n, jnp.int32) * w, w), w)
            kdup_ref[rows, :] = dup(k_ref[rows, :])
            vdup_ref[rows, :] = dup(v_ref[rows, :])
            return 0

        lax.fori_loop(0, n_blk, stage, 0, unroll=4)

    def q_block(n, first):
        start = n * w
        cur = pl.ds(start if first else pl.multiple_of(start, w), w)
        q = q_ref[cur, :] * (scale * LOG2E)
        zero = jnp.zeros_like(q)
        q2 = jnp.concatenate([jnp.where(low, q, zero), jnp.where(low, zero, q)], axis=0).astype(BF16)
        if first:
            k2 = jnp.concatenate([kdup_ref[cur, :], kdup_ref[cur, :]], axis=0)
            v2 = jnp.concatenate([vdup_ref[cur, :], vdup_ref[cur, :]], axis=0)
        else:
            both = pl.ds(pl.multiple_of(start - w, w), 2 * w)
            k2, v2 = kdup_ref[both, :], vdup_ref[both, :]
        s = lax.dot_general(q2, k2, (((1,), (1,)), ((), ())), preferred_element_type=F32)
        s = jnp.where(band & (col >= w) if first else band, s, NEG_INF)
        m = jnp.maximum(jnp.max(s, axis=1, keepdims=True), sink2)
        p = jnp.exp2(s - m)
        denom = jnp.sum(p, axis=1, keepdims=True) + jnp.exp2(sink2 - m)
        o2 = jnp.dot(p.astype(BF16), v2, preferred_element_type=F32) / denom
        o_ref[cur, :] = jnp.where(low, o2[:w], o2[w:]).astype(o_ref.dtype)

    for u in range(SWA_UNROLL):
        q_block(u, u == 0)

    def group(g, _):
        g = jnp.asarray(g, jnp.int32)
        for u in range(SWA_UNROLL):
            q_block(g * SWA_UNROLL + u, False)
        return 0

    lax.fori_loop(1, n_blk // SWA_UNROLL, group, 0)


def _swa_attention(z1, sinks, *, batch, seq):
    t = z1.shape[0]
    pairs = B_HEADS // 2
    rep = B_HEADS // B_KV_HEADS
    assert (seq // SWA_WINDOW) % SWA_UNROLL == 0
    kv_group = lambda p: (2 * p) // rep // 2
    return pl.pallas_call(
        functools.partial(_swa_kernel, scale=B_HEAD_DIM ** -0.5),
        grid_spec=pltpu.PrefetchScalarGridSpec(
            num_scalar_prefetch=1,
            grid=(batch, pairs),
            in_specs=[pl.BlockSpec((seq, LANES), lambda b, p, sinks: (b, OFF_QB // LANES + p)),
                      pl.BlockSpec((seq, LANES), lambda b, p, sinks: (b, OFF_KB // LANES + kv_group(p))),
                      pl.BlockSpec((seq, LANES), lambda b, p, sinks: (b, OFF_VB // LANES + kv_group(p)))],
            out_specs=pl.BlockSpec((seq, LANES), lambda b, p, sinks: (b, p)),
            scratch_shapes=[pltpu.VMEM((seq, LANES), BF16), pltpu.VMEM((seq, LANES), BF16)],
        ),
        out_shape=jax.ShapeDtypeStruct((t, B_WIDTH), BF16),
        compiler_params=_params("arbitrary", "arbitrary"),
        name="swa_attention",
    )(sinks, z1, z1, z1)


def _gated_out_kernel(oa_ref, ob_ref, oc_ref, ga_ref, gb_ref, gc_ref, wa_ref, wb_ref, wc_ref, y_ref,
                      wa_bf, wb_bf, wc_bf):
    @pl.when(pl.program_id(1) == 0)
    def _():
        wa_bf[...] = wa_ref[...].astype(BF16)
        wb_bf[...] = wb_ref[...].astype(BF16)
        wc_bf[...] = wc_ref[...].astype(BF16)

    y = jax.nn.sigmoid(ga_ref[...]) * jnp.dot(oa_ref[...], wa_bf[...], preferred_element_type=F32)
    y += jax.nn.sigmoid(gb_ref[...]) * jnp.dot(ob_ref[...], wb_bf[...], preferred_element_type=F32)
    y += jax.nn.sigmoid(gc_ref[...]) * jnp.dot(oc_ref[...], wc_bf[...], preferred_element_type=F32)
    y_ref[...] = y.astype(y_ref.dtype)


def _gated_out(oa, ob, oc, gates, w_out_a, w_out_b, w_out_c, layer, *, tm=512, tn=512):
    t = oa.shape[0]
    d = w_out_a.shape[2]
    nb = d // tn
    o_spec = lambda width: pl.BlockSpec((tm, width), lambda j, i: (i, 0))
    g_spec = lambda which: pl.BlockSpec((tm, tn), lambda j, i: (i, which * nb + j))
    w_spec = lambda width: pl.BlockSpec((None, width, tn), lambda j, i: (layer, 0, j))
    return pl.pallas_call(
        _gated_out_kernel,
        grid=(nb, t // tm),
        in_specs=[o_spec(A_WIDTH), o_spec(B_WIDTH), o_spec(C_WIDTH), g_spec(0), g_spec(1), g_spec(2),
                  w_spec(A_WIDTH), w_spec(B_WIDTH), w_spec(C_WIDTH)],
        out_specs=pl.BlockSpec((tm, tn), lambda j, i: (i, j)),
        out_shape=jax.ShapeDtypeStruct((t, d), BF16),
        scratch_shapes=[pltpu.VMEM((A_WIDTH, tn), BF16), pltpu.VMEM((B_WIDTH, tn), BF16),
                        pltpu.VMEM((C_WIDTH, tn), BF16)],
        compiler_params=_params("arbitrary", "arbitrary"),
        name="gated_out",
    )(oa, ob, oc, gates, gates, gates, w_out_a, w_out_b, w_out_c)


HI16 = -65536


def _bf16_bits_hi(v):
    return lax.bitcast_convert_type(v.astype(BF16).astype(F32), jnp.int32)


def _router_kernel(h_ref, g_ref, w_ref, wlo_ref, b_ref, ids_ref, wts_ref, packed_ref):
    x = h_ref[...]
    hn = x * lax.rsqrt(jnp.mean(x * x, axis=-1, keepdims=True) + NORM_EPS) * g_ref[...]
    half = hn.shape[1] // 2
    packed_ref[...] = lax.shift_right_logical(_bf16_bits_hi(hn[:, :half]), 16) | _bf16_bits_hi(hn[:, half:])
    x_hi = hn.astype(BF16)
    x_lo = (hn - x_hi.astype(F32)).astype(BF16)
    logits = (jnp.dot(x_hi, w_ref[...], preferred_element_type=F32)
              + jnp.dot(x_hi, wlo_ref[...], preferred_element_type=F32)
              + jnp.dot(x_lo, w_ref[...], preferred_element_type=F32)) + b_ref[...]
    lane = lax.broadcasted_iota(jnp.int32, logits.shape, 1)
    far = float(LANES)
    is_g = lane < N_GROUPS
    g_id = lane.astype(F32)
    gmax = jnp.max(jnp.where(is_g, logits, NEG_INF), axis=1, keepdims=True)
    gsel = jnp.min(jnp.where(is_g & (logits == gmax), g_id, far), axis=1, keepdims=True)
    p_g = 1.0 / jnp.sum(jnp.where(is_g, jnp.exp(logits - gmax), 0.0), axis=1, keepdims=True)
    e_lane = lane - N_GROUPS
    e_id = e_lane.astype(F32)
    e_group = jnp.right_shift(e_lane, 2).astype(F32)
    in_grp = (e_lane >= 0) & (e_lane < N_EXPERTS) & (e_group == gsel)
    emax = jnp.max(jnp.where(in_grp, logits, NEG_INF), axis=1, keepdims=True)
    ee = jnp.where(in_grp, jnp.exp(jnp.where(in_grp, logits, emax) - emax), 0.0)
    ep = ee / jnp.sum(ee, axis=1, keepdims=True)
    p1 = jnp.max(jnp.where(in_grp, ep, -1.0), axis=1, keepdims=True)
    i1 = jnp.min(jnp.where(in_grp & (ep == p1), e_id, far), axis=1, keepdims=True)
    rest = in_grp & (e_id != i1)
    p2 = jnp.max(jnp.where(rest, ep, -1.0), axis=1, keepdims=True)
    i2 = jnp.min(jnp.where(rest & (ep == p2), e_id, far), axis=1, keepdims=True)
    tot = p1 + p2
    ids_ref[...] = jnp.where(lane == 0, i1, jnp.where(lane == 1, i2, 0.0)).astype(jnp.int32)
    wts_ref[...] = jnp.where(lane == 0, p_g * p1 / tot, jnp.where(lane == 1, p_g * p2 / tot, 0.0))


def _router(h, g, w_r, b_r, layer, *, tm=256):
    t, d = h.shape
    out = pl.BlockSpec((tm, LANES), lambda i: (i, 0))
    return pl.pallas_call(
        _router_kernel,
        grid=(t // tm,),
        in_specs=[pl.BlockSpec((tm, d), lambda i: (i, 0)),
                  pl.BlockSpec((None, 1, d), lambda i: (layer, 0, 0)),
                  pl.BlockSpec((None, d, LANES), lambda i: (layer, 0, 0)),
                  pl.BlockSpec((None, d, LANES), lambda i: (layer, 0, 0)),
                  pl.BlockSpec((None, 1, LANES), lambda i: (layer, 0, 0))],
        out_specs=[out, out, pl.BlockSpec((tm, d // 2), lambda i: (i, 0))],
        out_shape=[jax.ShapeDtypeStruct((t, LANES), jnp.int32), jax.ShapeDtypeStruct((t, LANES), F32),
                   jax.ShapeDtypeStruct((t, d // 2), jnp.int32)],
        compiler_params=_params("parallel"),
        name="moe_router",
    )(h, g, *w_r, b_r)


def _row_copy(src_hbm, tok, dst_ref, r, sem):
    return pltpu.make_async_copy(src_hbm.at[pl.ds(tok, 1), :], dst_ref.at[pl.ds(r, 1), :], sem)


GATHER_UNROLL = 8
NORM_CHUNK = 16


def _dispatch_kernel(tok_ref, nvalid_ref, h_hbm, o_ref, rows_ref, sem):
    i = pl.program_id(0)
    n_valid = nvalid_ref[0]
    n_rows = rows_ref.shape[1]
    slot = lax.rem(i, 2)

    def fetch(blk, slot):
        base = blk * n_rows

        def issue(g, _):
            for u in range(GATHER_UNROLL):
                r = g * GATHER_UNROLL + u
                _row_copy(h_hbm, tok_ref[base + r], rows_ref.at[slot], r, sem.at[slot]).start(priority=u % 2)
            return 0

        lax.fori_loop(0, n_rows // GATHER_UNROLL, issue, 0)

    @pl.when(i == 0)
    def _():
        fetch(0, 0)

    @pl.when(i + 1 < n_valid)
    def _():
        fetch(i + 1, 1 - slot)

    @pl.when(i < n_valid)
    def _():
        def drain(r, _):
            _row_copy(h_hbm, 0, rows_ref.at[slot], r, sem.at[slot]).wait()
            return 0

        lax.fori_loop(0, n_rows, drain, 0, unroll=GATHER_UNROLL)

        half = rows_ref.shape[2]

        def unpack(c, _):
            rows = pl.ds(pl.multiple_of(jnp.asarray(c, jnp.int32) * NORM_CHUNK, NORM_CHUNK), NORM_CHUNK)
            word = rows_ref[slot, rows, :]
            o_ref[rows, :half] = lax.bitcast_convert_type(lax.shift_left(word, 16), F32).astype(o_ref.dtype)
            o_ref[rows, half:] = lax.bitcast_convert_type(word & HI16, F32).astype(o_ref.dtype)
            return 0

        lax.fori_loop(0, n_rows // NORM_CHUNK, unpack, 0, unroll=4)


def _dispatch(packed, buf_tok, nvalid, *, n_blocks):
    t, half = packed.shape
    blk = lambda i, tok, nv: (jnp.minimum(i, nv[0] - 1), 0)
    return pl.pallas_call(
        _dispatch_kernel,
        grid_spec=pltpu.PrefetchScalarGridSpec(
            num_scalar_prefetch=2,
            grid=(n_blocks,),
            in_specs=[pl.BlockSpec(memory_space=pl.ANY)],
            out_specs=pl.BlockSpec((MOE_ROWS, 2 * half), blk),
            scratch_shapes=[pltpu.VMEM((2, MOE_ROWS, half), jnp.int32), pltpu.SemaphoreType.DMA((2,))],
        ),
        out_shape=jax.ShapeDtypeStruct((n_blocks * MOE_ROWS, 2 * half), BF16),
        compiler_params=_params("arbitrary"),
        name="moe_dispatch",
    )(buf_tok, nvalid, packed)


def _run_weights(be_ref, first_ref, next_ref, i, pass_id, n_passes, n_blocks, copies, stage):
    @pl.when((pass_id == 0) & (i == 0))
    def _():
        for c in copies(be_ref[0], 0):
            c.start()

    @pl.when(first_ref[i] == 1)
    def _():
        for c in copies(be_ref[i], pass_id):
            c.wait()
        stage()
        nxt = next_ref[i]
        same_pass = nxt < n_blocks
        e_next = jnp.where(same_pass, be_ref[jnp.minimum(nxt, n_blocks - 1)], be_ref[0])
        pass_next = jnp.where(same_pass, pass_id, pass_id + 1)

        @pl.when(same_pass | (pass_id + 1 < n_passes))
        def _():
            for c in copies(e_next, pass_next):
                c.start()


def _expert_up_kernel(be_ref, nvalid_ref, first_ref, next_ref, x_ref, wg_hbm, wu_hbm, o_ref,
                      wg_land, wu_land, wg_bf, wu_bf, sem, *, layer):
    f, i = pl.program_id(0), pl.program_id(1)
    tf = wg_land.shape[1]

    def copies(e, ff_tile):
        cols = pl.ds(pl.multiple_of(ff_tile * tf, tf), tf)
        return (pltpu.make_async_copy(wg_hbm.at[layer, e, :, cols], wg_land, sem.at[0]),
                pltpu.make_async_copy(wu_hbm.at[layer, e, :, cols], wu_land, sem.at[1]))

    def stage():
        wg_bf[...] = wg_land[...].astype(BF16)
        wu_bf[...] = wu_land[...].astype(BF16)

    _run_weights(be_ref, first_ref, next_ref, i, f, pl.num_programs(0), pl.num_programs(1), copies, stage)

    @pl.when(i < nvalid_ref[0])
    def _():
        x = x_ref[...]
        hg = jnp.dot(x, wg_bf[...], preferred_element_type=F32)
        hu = jnp.dot(x, wu_bf[...], preferred_element_type=F32)
        o_ref[...] = (jax.nn.silu(hg) * hu).astype(o_ref.dtype)


def _expert_up(xb, w_gate, w_up, layer, runs, *, n_blocks):
    d = xb.shape[1]
    ff = w_gate.shape[3]
    tf = MOE_FF_TILE
    row_blk = lambda f, i, be, nv, first, nxt: (jnp.minimum(i, nv[0] - 1), 0)
    return pl.pallas_call(
        functools.partial(_expert_up_kernel, layer=layer),
        grid_spec=pltpu.PrefetchScalarGridSpec(
            num_scalar_prefetch=4,
            grid=(ff // tf, n_blocks),
            in_specs=[pl.BlockSpec((MOE_ROWS, d), row_blk), pl.BlockSpec(memory_space=pl.ANY),
                      pl.BlockSpec(memory_space=pl.ANY)],
            out_specs=pl.BlockSpec((MOE_ROWS, tf),
                                   lambda f, i, be, nv, first, nxt: (jnp.minimum(i, nv[0] - 1), f)),
            scratch_shapes=[pltpu.VMEM((d, tf), F32), pltpu.VMEM((d, tf), F32),
                            pltpu.VMEM((d, tf), BF16), pltpu.VMEM((d, tf), BF16),
                            pltpu.SemaphoreType.DMA((2,))],
        ),
        out_shape=jax.ShapeDtypeStruct((n_blocks * MOE_ROWS, ff), BF16),
        compiler_params=_params("arbitrary", "arbitrary"),
        name="moe_expert_up",
    )(*runs, xb, w_gate, w_up)


def _expert_down_kernel(be_ref, nvalid_ref, first_ref, next_ref, a_ref, wd_hbm, o_ref, wd_land, wd_bf, sem,
                        *, layer):
    i = pl.program_id(0)

    def copies(e, _):
        return (pltpu.make_async_copy(wd_hbm.at[layer, e], wd_land, sem.at[0]),)

    def stage():
        wd_bf[...] = wd_land[...].astype(BF16)

    _run_weights(be_ref, first_ref, next_ref, i, 0, 1, pl.num_programs(0), copies, stage)

    @pl.when(i < nvalid_ref[0])
    def _():
        o_ref[...] = jnp.dot(a_ref[...], wd_bf[...], preferred_element_type=F32)


def _expert_down(act, w_down, layer, runs, *, n_blocks):
    ff, d = w_down.shape[2], w_down.shape[3]
    row_blk = lambda i, be, nv, first, nxt: (jnp.minimum(i, nv[0] - 1), 0)
    return pl.pallas_call(
        functools.partial(_expert_down_kernel, layer=layer),
        grid_spec=pltpu.PrefetchScalarGridSpec(
            num_scalar_prefetch=4,
            grid=(n_blocks,),
            in_specs=[pl.BlockSpec((MOE_ROWS, ff), row_blk), pl.BlockSpec(memory_space=pl.ANY)],
            out_specs=pl.BlockSpec((MOE_ROWS, d), row_blk),
            scratch_shapes=[pltpu.VMEM((ff, d), F32), pltpu.VMEM((ff, d), BF16), pltpu.SemaphoreType.DMA((1,))],
        ),
        out_shape=jax.ShapeDtypeStruct((n_blocks * MOE_ROWS, d), F32),
        compiler_params=_params("arbitrary"),
        name="moe_expert_down",
    )(*runs, act, w_down)


def _combine_kernel(pos_ref, h_ref, wts_ref, g_ref, y_hbm, *refs, emit_h):
    o_ref = refs[0] if emit_h else None
    n_ref, y0_ref, y1_ref, sem = refs[int(emit_h):]
    i = pl.program_id(0)
    tm = h_ref.shape[0]
    slot = lax.rem(i, 2)

    def fetch(blk, slot):
        base = blk * tm * EXPERT_TOPK

        def issue(g, _):
            for u in range(GATHER_UNROLL):
                r = g * GATHER_UNROLL + u
                _row_copy(y_hbm, pos_ref[base + EXPERT_TOPK * r], y0_ref.at[slot], r, sem.at[slot]).start(priority=0)
                _row_copy(y_hbm, pos_ref[base + EXPERT_TOPK * r + 1], y1_ref.at[slot], r,
                          sem.at[slot]).start(priority=1)
            return 0

        lax.fori_loop(0, tm // GATHER_UNROLL, issue, 0)

    @pl.when(i == 0)
    def _():
        fetch(0, 0)

    @pl.when(i + 1 < pl.num_programs(0))
    def _():
        fetch(i + 1, 1 - slot)

    def drain(r, _):
        _row_copy(y_hbm, 0, y0_ref.at[slot], r, sem.at[slot]).wait()
        _row_copy(y_hbm, 0, y1_ref.at[slot], r, sem.at[slot]).wait()
        return 0

    lax.fori_loop(0, tm, drain, 0, unroll=GATHER_UNROLL)

    def rows_chunk(c, _):
        rows = pl.ds(pl.multiple_of(jnp.asarray(c, jnp.int32) * NORM_CHUNK, NORM_CHUNK), NORM_CHUNK)
        w = wts_ref[rows, :]
        hv = h_ref[rows, :] + w[:, 0:1] * y0_ref[slot, rows, :] + w[:, 1:2] * y1_ref[slot, rows, :]
        if emit_h:
            o_ref[rows, :] = hv
        hn = hv * lax.rsqrt(jnp.mean(hv * hv, axis=-1, keepdims=True) + NORM_EPS) * g_ref[...]
        n_ref[rows, :] = hn.astype(n_ref.dtype)
        return 0

    lax.fori_loop(0, tm // NORM_CHUNK, rows_chunk, 0, unroll=4)


def _combine(h, wts, y, pos, g, *, norm_dtype, emit_h, tm=128):
    t, d = h.shape
    row = pl.BlockSpec((tm, d), lambda i, pos: (i, 0))
    outs = pl.pallas_call(
        functools.partial(_combine_kernel, emit_h=emit_h),
        grid_spec=pltpu.PrefetchScalarGridSpec(
            num_scalar_prefetch=1,
            grid=(t // tm,),
            in_specs=[row, pl.BlockSpec((tm, LANES), lambda i, pos: (i, 0)),
                      pl.BlockSpec((1, d), lambda i, pos: (0, 0)), pl.BlockSpec(memory_space=pl.ANY)],
            out_specs=[row] * (1 + emit_h),
            scratch_shapes=[pltpu.VMEM((2, tm, d), F32), pltpu.VMEM((2, tm, d), F32),
                            pltpu.SemaphoreType.DMA((2,))],
        ),
        out_shape=([jax.ShapeDtypeStruct((t, d), F32)] if emit_h else [])
        + [jax.ShapeDtypeStruct((t, d), norm_dtype)],
        compiler_params=_params("arbitrary"),
        name="moe_combine",
    )(pos, h, wts, g.reshape(1, d), y)
    return (outs[0], outs[1]) if emit_h else (None, outs[0])


def _hier_moe(h, ffn_norm_g, w_r, b_r, w_gate, w_up, w_down, layer, next_norm_g, *, norm_dtype, emit_h):
    t, d = h.shape
    tk = t * EXPERT_TOPK
    n_blocks = tk // MOE_ROWS + N_EXPERTS
    ids, wts, hn_packed = _router(h, ffn_norm_g, w_r, b_r, layer)
    flat_e = ids[:, :EXPERT_TOPK].reshape(tk)
    onehot = (flat_e[:, None] == jnp.arange(N_EXPERTS, dtype=jnp.int32)[None, :]).astype(jnp.int32)
    csum = jnp.cumsum(onehot, axis=0)
    rank = jnp.sum(onehot * csum, axis=1) - 1
    counts = csum[-1]
    padded = (counts + MOE_ROWS - 1) // MOE_ROWS * MOE_ROWS
    pad_end = jnp.cumsum(padded)
    pad_start = pad_end - padded
    dest = (pad_start[flat_e] + rank).astype(jnp.int32)
    flat_tok = jnp.arange(tk, dtype=jnp.int32) // EXPERT_TOPK
    buf_tok = jnp.zeros((n_blocks * MOE_ROWS,), jnp.int32).at[dest].set(flat_tok)
    nvalid = (pad_end[-1] // MOE_ROWS).astype(jnp.int32).reshape(1)
    blk_start = jnp.minimum(jnp.arange(n_blocks, dtype=jnp.int32), nvalid[0] - 1) * MOE_ROWS
    block_expert = jnp.minimum(jnp.searchsorted(pad_end, blk_start, side='right'), N_EXPERTS - 1).astype(jnp.int32)

    blk = jnp.arange(n_blocks, dtype=jnp.int32)
    prev_expert = jnp.concatenate([jnp.full((1,), -1, jnp.int32), block_expert[:-1]])
    run_first = (blk < nvalid[0]) & (block_expert != prev_expert)
    first_at_or_after = lax.cummin(jnp.where(run_first, blk, n_blocks), reverse=True)
    run_next = jnp.concatenate([first_at_or_after[1:], jnp.full((1,), n_blocks, jnp.int32)])
    runs = (block_expert, nvalid, run_first.astype(jnp.int32), run_next)

    xb = _dispatch(hn_packed, buf_tok, nvalid, n_blocks=n_blocks)
    act = _expert_up(xb, w_gate, w_up, layer, runs, n_blocks=n_blocks)
    y = _expert_down(act, w_down, layer, runs, n_blocks=n_blocks)
    return _combine(h, wts, y, dest, next_norm_g, norm_dtype=norm_dtype, emit_h=emit_h)


ROPE_A, ROPE_B, ROPE_PE = 0, 1, 2


def _in_proj_rope_table(col):
    if OFF_QA <= col < OFF_VA:
        return ROPE_A
    if OFF_QB <= col < OFF_VB:
        return ROPE_B
    if col == OFF_KPE:
        return ROPE_PE
    return None


def _mla_q_rope_table(col):
    return 0 if (col // LANES) % 2 == 1 else None


def kernel(x, positions, attn_norm_g, w_in, q_norm_g, kv_norm_g, wq_b, wkv_b, sinks, w_out_a, w_out_b,
           w_out_c, w_o, ffn_norm_g, w_group, b_group, w_expert, b_expert, w_gate, w_up, w_down,
           final_norm_g):
    batch, seq, d = x.shape
    depth = w_in.shape[0]
    t = batch * seq
    h = x.reshape(t, d)
    pos = positions.reshape(t).astype(F32)

    w_in_t = jnp.swapaxes(w_in, 1, 2)
    q_head = C_NOPE_DIM + C_ROPE_DIM
    wq_pad = jnp.pad(wq_b.reshape(depth, C_Q_RANK, C_HEADS, q_head),
                     ((0, 0), (0, 0), (0, 0), (0, 2 * LANES - q_head))
                     ).reshape(depth, C_Q_RANK, C_HEADS * 2 * LANES)
    wq_pad = (wq_pad * MLA_Q_FACTOR).astype(BF16)
    w_r = jnp.concatenate([w_group, w_expert,
                           jnp.zeros((depth, d, LANES - N_GROUPS - N_EXPERTS), F32)], axis=2)
    w_r_hi = w_r.astype(BF16)
    w_r = (w_r_hi, (w_r - w_r_hi.astype(F32)).astype(BF16))
    b_r = jnp.concatenate([b_group, b_expert,
                           jnp.zeros((depth, LANES - N_GROUPS - N_EXPERTS), F32)], axis=1).reshape(depth, 1, LANES)
    ffn_g = ffn_norm_g.reshape(depth, 1, d)

    c_a, s_a = _rope_tables(pos, A_HEAD_DIM)
    c_b, s_b = _rope_tables(pos, B_HEAD_DIM)
    c_pe, s_pe = _rope_tables(pos, C_ROPE_DIM, keep_lanes=C_ROPE_DIM)
    rope_tables = ((c_a, s_a, A_HEAD_DIM // 2), (c_b, s_b, B_HEAD_DIM // 2), (c_pe, s_pe, C_ROPE_DIM // 2))

    hn = _rmsnorm(h, attn_norm_g[0])
    for l in range(depth):
        z1 = _matmul(hn, w_in_t, l, n_out=Z1_WIDTH, w_is_nk=True, rope_tables=rope_tables,
                     rope_group_table=_in_proj_rope_table, name="in_proj")
        gates = _matmul(hn, w_in_t, l, n_out=3 * d, col_off=OFF_GATES, w_is_nk=True, name="gate_proj")
        o_a = _moba_attention(z1, batch=batch, seq=seq)
        o_b = _swa_attention(z1, sinks[l], batch=batch, seq=seq)
        cq_n = _rmsnorm(z1, q_norm_g[l], col_off=OFF_CQ, width=C_Q_RANK, piece=512)
        ckv_n = _rmsnorm(z1, kv_norm_g[l], col_off=OFF_CKV, width=C_KV_RANK, piece=512)
        q_c = _matmul(cq_n, wq_pad, l, n_out=C_HEADS * 2 * LANES, tn=1024, out_dtype=BF16,
                      rope_tables=rope_tables[ROPE_PE:], rope_group_table=_mla_q_rope_table, name="mla_q_proj")
        kv_c = _matmul(ckv_n, wkv_b, l, n_out=C_HEADS * (C_NOPE_DIM + C_V_DIM), tn=1024, out_dtype=BF16,
                       name="mla_kv_proj")
        o_c = _mla_attention(q_c, kv_c, z1, batch=batch, seq=seq)
        y = _gated_out(o_a, o_b, o_c, gates, w_out_a, w_out_b, w_out_c, l)
        h = _matmul(y, w_o, l, n_out=d, residual=h, name="out_proj")
        last = l == depth - 1
        h, hn = _hier_moe(h, ffn_g, w_r, b_r, w_gate, w_up, w_down, l,
                          final_norm_g if last else attn_norm_g[l + 1],
                          norm_dtype=F32 if last else BF16, emit_h=not last)
    return hn.reshape(batch, seq, d)
```

```python
import functools

import jax
import jax.numpy as jnp
from jax import lax
from jax.experimental import pallas as pl
from jax.experimental.pallas import tpu as pltpu

F32 = jnp.float32
BF16 = jnp.bfloat16

ROPE_THETA = 10000.0
NORM_EPS = 1e-6
NEG_INF = -1e30
PICKED = -3e38
LOG2E = 1.4426950408889634

A_HEADS, A_HEAD_DIM = 16, 128
MOBA_BLOCK, MOBA_TOPK = 256, 3
B_HEADS, B_KV_HEADS, B_HEAD_DIM, SWA_WINDOW = 32, 4, 64, 128
C_HEADS, C_Q_RANK, C_KV_RANK, C_NOPE_DIM, C_ROPE_DIM, C_V_DIM = 16, 1024, 512, 128, 64, 128
N_GROUPS, EXPERTS_PER_GROUP, EXPERT_TOPK, D_FF_EXPERT = 8, 4, 2, 768
N_EXPERTS = N_GROUPS * EXPERTS_PER_GROUP
assert EXPERTS_PER_GROUP == 4

LANES = 128
VMEM_LIMIT_BYTES = 58 * 1024 * 1024

A_WIDTH = A_HEADS * A_HEAD_DIM
B_WIDTH = B_HEADS * B_HEAD_DIM
B_KV_WIDTH = B_KV_HEADS * B_HEAD_DIM
C_WIDTH = C_HEADS * C_V_DIM
OFF_QA = 0
OFF_KA = OFF_QA + A_WIDTH
OFF_VA = OFF_KA + A_WIDTH
OFF_QB = OFF_VA + A_WIDTH
OFF_KB = OFF_QB + B_WIDTH
OFF_VB = OFF_KB + B_KV_WIDTH
OFF_CQ = OFF_VB + B_KV_WIDTH
OFF_CKV = OFF_CQ + C_Q_RANK
OFF_KPE = OFF_CKV + C_KV_RANK
OFF_GATES = OFF_KPE + C_ROPE_DIM
Z1_WIDTH = 10752

MOE_ROWS = 256
MOE_FF_TILE = 768
SWA_UNROLL = 8


def _params(*sem):
    return pltpu.CompilerParams(dimension_semantics=sem, vmem_limit_bytes=VMEM_LIMIT_BYTES)


def _rmsnorm_kernel(*refs, n_pieces, width):
    x_refs, g_ref, o_ref = refs[:n_pieces], refs[n_pieces], refs[n_pieces + 1]
    xs = [r[...].astype(F32) for r in x_refs]
    ss = sum(jnp.sum(x * x, axis=-1, keepdims=True) for x in xs)
    inv = lax.rsqrt(ss * (1.0 / width) + NORM_EPS)
    pw = xs[0].shape[1]
    for p, x in enumerate(xs):
        o_ref[:, p * pw:(p + 1) * pw] = (x * inv * g_ref[:, p * pw:(p + 1) * pw]).astype(o_ref.dtype)


def _rmsnorm(x, g, *, col_off=0, width=None, piece=None, tm=256, out_dtype=BF16):
    t = x.shape[0]
    width = width or x.shape[1]
    piece = piece or width
    n_pieces = width // piece
    off = col_off // piece
    assert col_off % piece == 0 and width % piece == 0
    in_specs = [pl.BlockSpec((tm, piece), functools.partial(lambda i, p: (i, off + p), p=p))
                for p in range(n_pieces)]
    in_specs.append(pl.BlockSpec((1, width), lambda i: (0, 0)))
    return pl.pallas_call(
        functools.partial(_rmsnorm_kernel, n_pieces=n_pieces, width=width),
        grid=(t // tm,),
        in_specs=in_specs,
        out_specs=pl.BlockSpec((tm, width), lambda i: (i, 0)),
        out_shape=jax.ShapeDtypeStruct((t, width), out_dtype),
        compiler_params=_params("parallel"),
        name="rmsnorm",
    )(*([x] * n_pieces), g.reshape(1, width))


def _stage_weight(w_ref, wbf_ref, w_is_nk):
    if not w_is_nk:
        wbf_ref[...] = w_ref[...].astype(BF16)
        return
    wbf_ref[...] = w_ref[0].astype(BF16)


def _swap_halves(x, half):
    if half == 64:
        return pltpu.roll(x, 64, axis=1)
    lane = lax.broadcasted_iota(jnp.int32, x.shape, 1)
    return jnp.where((lane & half) == 0, pltpu.roll(x, LANES - half, axis=1), pltpu.roll(x, half, axis=1))


def _matmul_kernel(*refs, stage_w, w_is_nk, has_res, rope_modes, rope_halves):
    x_ref, w_ref = refs[0], refs[1]
    n_in = 2 + has_res + 2 * len(rope_halves)
    r_ref = refs[2] if has_res else None
    table_refs = refs[2 + has_res:n_in]
    o_ref = refs[n_in]
    if stage_w:
        wbf_ref = refs[n_in + 1]

        @pl.when(pl.program_id(1) == 0)
        def _():
            _stage_weight(w_ref, wbf_ref, w_is_nk)

        w_src = wbf_ref
    else:
        w_src = w_ref

    def product():
        if w_is_nk:
            acc = lax.dot_general(x_ref[...], w_src[...], (((1,), (1,)), ((), ())), preferred_element_type=F32)
        else:
            acc = jnp.dot(x_ref[...], w_src[...], preferred_element_type=F32)
        return acc + r_ref[...] if has_res else acc

    if rope_modes is None:
        o_ref[...] = product().astype(o_ref.dtype)
        return
    j = pl.program_id(0)
    for pattern in sorted(set(rope_modes), key=str):
        blocks = [b for b, p in enumerate(rope_modes) if p == pattern]

        @pl.when(functools.reduce(jnp.logical_or, [j == b for b in blocks]))
        def _(pattern=pattern):
            acc = product()
            for g, table in enumerate(pattern):
                xg = acc[:, g * LANES:(g + 1) * LANES]
                if table is not None:
                    c_ref, s_ref = table_refs[2 * table], table_refs[2 * table + 1]
                    xg = xg * c_ref[...] + _swap_halves(xg, rope_halves[table]) * s_ref[...]
                o_ref[:, g * LANES:(g + 1) * LANES] = xg.astype(o_ref.dtype)


def _matmul(x, w, layer, *, n_out, col_off=0, w_is_nk=False, tm=1024, tn=512, out_dtype=F32, residual=None,
            rope_tables=(), rope_group_table=None, name="matmul"):
    m, k = x.shape
    assert w.shape[2 if w_is_nk else 1] == k and m % tm == 0 and n_out % tn == 0
    stage_w = w_is_nk or w.dtype != BF16
    rope_modes = None
    if rope_group_table is not None:
        rope_modes = tuple(tuple(rope_group_table(b * tn + g * LANES) for g in range(tn // LANES))
                           for b in range(n_out // tn))
    if w_is_nk:
        assert k % tn == 0 and col_off % 8 == 0
        w_spec = pl.BlockSpec((pl.Element(1), pl.Element(tn), pl.Element(k)),
                              lambda j, i: (layer, pl.multiple_of(col_off + j * tn, 8), 0))
    else:
        assert col_off % tn == 0
        w_spec = pl.BlockSpec((None, k, tn), lambda j, i: (layer, 0, j + col_off // tn))
    in_specs = [pl.BlockSpec((tm, k), lambda j, i: (i, 0)), w_spec]
    args = [x, w]
    if residual is not None:
        in_specs.append(pl.BlockSpec((tm, tn), lambda j, i: (i, j)))
        args.append(residual)
    for c, s, _ in rope_tables:
        in_specs += [pl.BlockSpec((tm, LANES), lambda j, i: (i, 0))] * 2
        args += [c, s]
    return pl.pallas_call(
        functools.partial(_matmul_kernel, stage_w=stage_w, w_is_nk=w_is_nk, has_res=residual is not None,
                          rope_modes=rope_modes, rope_halves=tuple(half for _, _, half in rope_tables)),
        grid=(n_out // tn, m // tm),
        in_specs=in_specs,
        out_specs=pl.BlockSpec((tm, tn), lambda j, i: (i, j)),
        out_shape=jax.ShapeDtypeStruct((m, n_out), out_dtype),
        scratch_shapes=[pltpu.VMEM((tn, k) if w_is_nk else (k, tn), BF16)] if stage_w else [],
        compiler_params=_params("arbitrary", "arbitrary"),
        name=name,
    )(*args)


def _rope_tables_kernel(pos_ref, invf_ref, sign_ref, keep_ref, c_ref, s_ref):
    ang = pos_ref[...] * invf_ref[...]
    c_ref[...] = jnp.cos(ang) * keep_ref[...]
    s_ref[...] = jnp.sin(ang) * sign_ref[...]


def _rope_tables(pos, dim, *, keep_lanes=LANES):
    t = pos.shape[0]
    half = dim // 2
    lane = jnp.arange(LANES)
    invf = (ROPE_THETA ** (-(2.0 * (lane % half)).astype(F32) / dim)).reshape(1, LANES)
    keep = (lane < keep_lanes).astype(F32).reshape(1, LANES)
    sign = jnp.where((lane % dim) < half, -1.0, 1.0).astype(F32).reshape(1, LANES) * keep
    tm = 1024
    row = pl.BlockSpec((1, LANES), lambda i: (0, 0))
    out = pl.BlockSpec((tm, LANES), lambda i: (i, 0))
    return pl.pallas_call(
        _rope_tables_kernel,
        grid=(t // tm,),
        in_specs=[pl.BlockSpec((tm, 1), lambda i: (i, 0)), row, row, row],
        out_specs=[out, out],
        out_shape=[jax.ShapeDtypeStruct((t, LANES), F32)] * 2,
        compiler_params=_params("parallel"),
        name="rope_tables",
    )(pos.reshape(t, 1), invf, sign, keep)


def _online_block(q, k, v, s_mask, carry):
    m, l, acc = carry
    s = s_mask(lax.dot_general(q, k, (((1,), (1,)), ((), ())), preferred_element_type=F32))
    m_new = jnp.maximum(m, jnp.max(s, axis=1, keepdims=True))
    alpha = jnp.exp2(m - m_new)
    p = jnp.exp2(s - m_new)
    l = alpha * l + jnp.sum(p, axis=1, keepdims=True)
    acc = alpha * acc + jnp.dot(p.astype(BF16), v, preferred_element_type=F32)
    return m_new, l, acc


def _causal_attend(qi, qs, load_kvs, tq, dv, diag_mask, past_mask):
    heads = range(len(qs))
    init = (jnp.full((tq, 1), NEG_INF, F32), jnp.zeros((tq, 1), F32), jnp.zeros((tq, dv), F32))
    kvs = load_kvs(qi)
    carries = tuple(_online_block(qs[h], *kvs[h], functools.partial(diag_mask, h), init) for h in heads)

    def past(j, carries):
        j = jnp.asarray(j, jnp.int32)
        kvs = load_kvs(j)
        return tuple(_online_block(qs[h], *kvs[h], functools.partial(past_mask, h, j), carries[h])
                     for h in heads)

    carries = lax.fori_loop(0, qi, past, carries)
    return [acc / l for _, l, acc in carries]


def _mla_kernel(q_ref, kv_ref, kpe_ref, o_ref, *, tq, heads):
    n_q = q_ref.shape[0] // tq
    qw, dv = 2 * LANES, C_V_DIM
    row = lax.broadcasted_iota(jnp.int32, (tq, tq), 0)
    col = lax.broadcasted_iota(jnp.int32, (tq, tq), 1)
    causal = col <= row

    def load_kvs(j):
        rows = pl.ds(pl.multiple_of(j * tq, tq), tq)
        kpe = kpe_ref[rows, :].astype(BF16)
        return [(jnp.concatenate([kv_ref[rows, h * qw:h * qw + C_NOPE_DIM], kpe], axis=1),
                 kv_ref[rows, h * qw + C_NOPE_DIM:(h + 1) * qw]) for h in range(heads)]

    def q_tile(qi, _):
        qi = jnp.asarray(qi, jnp.int32)
        rows = pl.ds(pl.multiple_of(qi * tq, tq), tq)
        qs = [q_ref[rows, h * qw:(h + 1) * qw] for h in range(heads)]
        outs = _causal_attend(qi, qs, load_kvs, tq, dv,
                              lambda h, s: jnp.where(causal, s, NEG_INF), lambda h, j, s: s)
        for h in range(heads):
            o_ref[rows, h * dv:(h + 1) * dv] = outs[h].astype(o_ref.dtype)
        return 0

    lax.fori_loop(0, n_q, q_tile, 0)


MLA_Q_FACTOR = (C_NOPE_DIM + C_ROPE_DIM) ** -0.5 * LOG2E


def _mla_attention(q, kv, z1, *, batch, seq, tq=1024, heads=2):
    t = q.shape[0]
    wide = pl.BlockSpec((seq, heads * 2 * LANES), lambda b, g: (b, g))
    return pl.pallas_call(
        functools.partial(_mla_kernel, tq=tq, heads=heads),
        grid=(batch, C_HEADS // heads),
        in_specs=[wide, wide, pl.BlockSpec((seq, LANES), lambda b, g: (b, OFF_KPE // LANES))],
        out_specs=pl.BlockSpec((seq, heads * C_V_DIM), lambda b, g: (b, g)),
        out_shape=jax.ShapeDtypeStruct((t, C_WIDTH), BF16),
        compiler_params=_params("parallel", "parallel"),
        name="mla_attention",
    )(q, kv, z1)


def _moba_kernel(q_ref, k_ref, v_ref, o_ref, kmean_ref, *, tq, heads, scale):
    bs, d = MOBA_BLOCK, A_HEAD_DIM
    n_blk = q_ref.shape[0] // bs
    per_tile = tq // bs
    for h in range(heads):
        for n in range(n_blk):
            kmean_ref[h, n:n + 1, :] = jnp.mean(k_ref[n * bs:(n + 1) * bs, h * d:(h + 1) * d], axis=0, keepdims=True)
    blk_i = lax.broadcasted_iota(jnp.int32, (n_blk, tq), 0)
    blk = blk_i.astype(F32)
    bit_of_blk = lax.shift_left(jnp.ones_like(blk_i), blk_i).astype(F32)
    lane_q = lax.broadcasted_iota(jnp.int32, (1, tq), 1)
    lane_sub = sum(((lane_q >= c * bs).astype(F32) for c in range(1, per_tile)), jnp.zeros((1, tq), F32))
    row1 = lax.broadcasted_iota(jnp.int32, (tq, 1), 0)
    row_sub = sum(((row1 >= c * bs).astype(F32) for c in range(1, per_tile)), jnp.zeros((tq, 1), F32))
    col_local = lax.broadcasted_iota(jnp.int32, (tq, bs), 1)

    def load_kvs(j):
        rows = pl.ds(pl.multiple_of(j * tq, tq), tq)
        return [(k_ref[rows, h * d:(h + 1) * d].astype(BF16), v_ref[rows, h * d:(h + 1) * d].astype(BF16))
                for h in range(heads)]

    def q_tile(qi, _):
        qi = jnp.asarray(qi, jnp.int32)
        rows = pl.ds(pl.multiple_of(qi * tq, tq), tq)
        first = qi * per_tile
        own = first.astype(F32) + lane_sub
        qs, sels = [], []
        for h in range(heads):
            qf = q_ref[rows, h * d:(h + 1) * d]
            gate = lax.dot_general(kmean_ref[h], qf, (((1,), (1,)), ((), ())),
                                   precision=lax.Precision.HIGHEST, preferred_element_type=F32)
            gate = jnp.where(blk < own, gate, NEG_INF)
            sel = jnp.zeros((n_blk, tq), F32)
            for _ in range(MOBA_TOPK):
                best = jnp.max(gate, axis=0, keepdims=True)
                idx = jnp.min(jnp.where(gate == best, blk, float(n_blk)), axis=0, keepdims=True)
                pick = blk == idx
                sel = jnp.where(pick & (blk < own), 1.0, sel)
                gate = jnp.where(pick, PICKED, gate)
            qs.append((qf * (scale * LOG2E)).astype(BF16))
            code = jnp.sum(sel * bit_of_blk, axis=0, keepdims=True)
            code_col = jnp.max(jnp.broadcast_to(code, (LANES, tq)).T, axis=1, keepdims=True)
            sels.append(code_col.astype(jnp.int32))

        def chosen(h, b):
            return (lax.shift_right_logical(sels[h], jnp.broadcast_to(b, sels[h].shape)) & 1) == 1

        def by_block(s, mask_block):
            parts = [mask_block(c, s[:, c * bs:(c + 1) * bs]) for c in range(per_tile)]
            return parts[0] if per_tile == 1 else jnp.concatenate(parts, axis=1)

        def diag_mask(h, s):
            def mask_block(c, sc):
                limit = jnp.where(row_sub == c, row1 - c * bs + 1,
                                  jnp.where((row_sub > c) & chosen(h, first + c), bs, 0))
                return jnp.where(col_local < limit, sc, NEG_INF)
            return by_block(s, mask_block)

        def past_mask(h, j, s):
            jb = j * per_tile
            return by_block(s, lambda c, sc: jnp.where(chosen(h, jb + c), sc, NEG_INF))

        outs = _causal_attend(qi, qs, load_kvs, tq, d, diag_mask, past_mask)
        for h in range(heads):
            o_ref[rows, h * d:(h + 1) * d] = outs[h].astype(o_ref.dtype)
        return 0

    lax.fori_loop(0, q_ref.shape[0] // tq, q_tile, 0)


def _moba_attention(z1, *, batch, seq, tq=1024, heads=2):
    t = z1.shape[0]
    assert seq % tq == 0 and tq % MOBA_BLOCK == 0
    n_groups = A_HEADS // heads
    wide = heads * A_HEAD_DIM
    assert OFF_QA == 0 and OFF_KA == A_WIDTH and OFF_VA == 2 * A_WIDTH
    return pl.pallas_call(
        functools.partial(_moba_kernel, tq=tq, heads=heads, scale=A_HEAD_DIM ** -0.5),
        grid=(batch, n_groups),
        in_specs=[pl.BlockSpec((seq, wide), lambda b, g: (b, g)),
                  pl.BlockSpec((seq, wide), lambda b, g: (b, n_groups + g)),
                  pl.BlockSpec((seq, wide), lambda b, g: (b, 2 * n_groups + g))],
        out_specs=pl.BlockSpec((seq, wide), lambda b, g: (b, g)),
        out_shape=jax.ShapeDtypeStruct((t, A_WIDTH), BF16),
        scratch_shapes=[pltpu.VMEM((heads, seq // MOBA_BLOCK, A_HEAD_DIM), F32)],
        compiler_params=_params("parallel", "parallel"),
        name="moba_attention",
    )(z1, z1, z1)


def _swa_kernel(sinks_ref, q_ref, k_ref, v_ref, o_ref, kdup_ref, vdup_ref, *, scale):
    w = SWA_WINDOW
    n_blk = q_ref.shape[0] // w
    pair = pl.program_id(1)
    kv_odd = ((pair * 2) // (B_HEADS // B_KV_HEADS)) % 2
    lane = lax.broadcasted_iota(jnp.int32, (w, LANES), 1)
    low = lane < B_HEAD_DIM
    keep_orig = jnp.where(low, 0, 1) == kv_odd
    row = lax.broadcasted_iota(jnp.int32, (2 * w, 2 * w), 0)
    col = lax.broadcasted_iota(jnp.int32, (2 * w, 2 * w), 1)
    rel = (row & (w - 1)) + w - col
    band = (rel >= 0) & (rel < w)
    sink2 = jnp.where(lax.broadcasted_iota(jnp.int32, (2 * w, 1), 0) < w,
                      sinks_ref[2 * pair], sinks_ref[2 * pair + 1]) * LOG2E

    def dup(x):
        return jnp.where(keep_orig, x, pltpu.roll(x, B_HEAD_DIM, axis=1)).astype(BF16)

    @pl.when(lax.rem(pair, (B_HEADS // B_KV_HEADS) // 2) == 0)
    def _():
        def stage(n, _):
            rows = pl.ds(pl.multiple_of(jnp.asarray(n, jnp.int32) * w, w), w)
            kdup_ref[rows, :] = dup(k_ref[rows, :])
            vdup_ref[rows, :] = dup(v_ref[rows, :])
            return 0

        lax.fori_loop(0, n_blk, stage, 0, unroll=4)

    def q_block(n, first):
        start = n * w
        cur = pl.ds(start if first else pl.multiple_of(start, w), w)
        q = q_ref[cur, :] * (scale * LOG2E)
        zero = jnp.zeros_like(q)
        q2 = jnp.concatenate([jnp.where(low, q, zero), jnp.where(low, zero, q)], axis=0).astype(BF16)
        if first:
            k2 = jnp.concatenate([kdup_ref[cur, :], kdup_ref[cur, :]], axis=0)
            v2 = jnp.concatenate([vdup_ref[cur, :], vdup_ref[cur, :]], axis=0)
        else:
            both = pl.ds(pl.multiple_of(start - w, w), 2 * w)
            k2, v2 = kdup_ref[both, :], vdup_ref[both, :]
        s = lax.dot_general(q2, k2, (((1,), (1,)), ((), ())), preferred_element_type=F32)
        s = jnp.where(band & (col >= w) if first else band, s, NEG_INF)
        m = jnp.maximum(jnp.max(s, axis=1, keepdims=True), sink2)
        p = jnp.exp2(s - m)
        denom = jnp.sum(p, axis=1, keepdims=True) + jnp.exp2(sink2 - m)
        o2 = jnp.dot(p.astype(BF16), v2, preferred_element_type=F32) / denom
        o_ref[cur, :] = jnp.where(low, o2[:w], o2[w:]).astype(o_ref.dtype)

    for u in range(SWA_UNROLL):
        q_block(u, u == 0)

    def group(g, _):
        g = jnp.asarray(g, jnp.int32)
        for u in range(SWA_UNROLL):
            q_block(g * SWA_UNROLL + u, False)
        return 0

    lax.fori_loop(1, n_blk // SWA_UNROLL, group, 0)


def _swa_attention(z1, sinks, *, batch, seq):
    t = z1.shape[0]
    pairs = B_HEADS // 2
    rep = B_HEADS // B_KV_HEADS
    assert (seq // SWA_WINDOW) % SWA_UNROLL == 0
    kv_group = lambda p: (2 * p) // rep // 2
    return pl.pallas_call(
        functools.partial(_swa_kernel, scale=B_HEAD_DIM ** -0.5),
        grid_spec=pltpu.PrefetchScalarGridSpec(
            num_scalar_prefetch=1,
            grid=(batch, pairs),
            in_specs=[pl.BlockSpec((seq, LANES), lambda b, p, sinks: (b, OFF_QB // LANES + p)),
                      pl.BlockSpec((seq, LANES), lambda b, p, sinks: (b, OFF_KB // LANES + kv_group(p))),
                      pl.BlockSpec((seq, LANES), lambda b, p, sinks: (b, OFF_VB // LANES + kv_group(p)))],
            out_specs=pl.BlockSpec((seq, LANES), lambda b, p, sinks: (b, p)),
            scratch_shapes=[pltpu.VMEM((seq, LANES), BF16), pltpu.VMEM((seq, LANES), BF16)],
        ),
        out_shape=jax.ShapeDtypeStruct((t, B_WIDTH), BF16),
        compiler_params=_params("arbitrary", "arbitrary"),
        name="swa_attention",
    )(sinks, z1, z1, z1)


def _gated_out_kernel(oa_ref, ob_ref, oc_ref, ga_ref, gb_ref, gc_ref, wa_ref, wb_ref, wc_ref, y_ref,
                      wa_bf, wb_bf, wc_bf):
    @pl.when(pl.program_id(1) == 0)
    def _():
        wa_bf[...] = wa_ref[...].astype(BF16)
        wb_bf[...] = wb_ref[...].astype(BF16)
        wc_bf[...] = wc_ref[...].astype(BF16)

    y = jax.nn.sigmoid(ga_ref[...]) * jnp.dot(oa_ref[...], wa_bf[...], preferred_element_type=F32)
    y += jax.nn.sigmoid(gb_ref[...]) * jnp.dot(ob_ref[...], wb_bf[...], preferred_element_type=F32)
    y += jax.nn.sigmoid(gc_ref[...]) * jnp.dot(oc_ref[...], wc_bf[...], preferred_element_type=F32)
    y_ref[...] = y.astype(y_ref.dtype)


def _gated_out(oa, ob, oc, gates, w_out_a, w_out_b, w_out_c, layer, *, tm=512, tn=512):
    t = oa.shape[0]
    d = w_out_a.shape[2]
    nb = d // tn
    o_spec = lambda width: pl.BlockSpec((tm, width), lambda j, i: (i, 0))
    g_spec = lambda which: pl.BlockSpec((tm, tn), lambda j, i: (i, which * nb + j))
    w_spec = lambda width: pl.BlockSpec((None, width, tn), lambda j, i: (layer, 0, j))
    return pl.pallas_call(
        _gated_out_kernel,
        grid=(nb, t // tm),
        in_specs=[o_spec(A_WIDTH), o_spec(B_WIDTH), o_spec(C_WIDTH), g_spec(0), g_spec(1), g_spec(2),
                  w_spec(A_WIDTH), w_spec(B_WIDTH), w_spec(C_WIDTH)],
        out_specs=pl.BlockSpec((tm, tn), lambda j, i: (i, j)),
        out_shape=jax.ShapeDtypeStruct((t, d), BF16),
        scratch_shapes=[pltpu.VMEM((A_WIDTH, tn), BF16), pltpu.VMEM((B_WIDTH, tn), BF16),
                        pltpu.VMEM((C_WIDTH, tn), BF16)],
        compiler_params=_params("arbitrary", "arbitrary"),
        name="gated_out",
    )(oa, ob, oc, gates, gates, gates, w_out_a, w_out_b, w_out_c)


HI16 = -65536


def _bf16_bits_hi(v):
    return lax.bitcast_convert_type(v.astype(BF16).astype(F32), jnp.int32)


def _router_kernel(h_ref, g_ref, w_ref, wlo_ref, b_ref, ids_ref, wts_ref, packed_ref):
    x = h_ref[...]
    hn = x * lax.rsqrt(jnp.mean(x * x, axis=-1, keepdims=True) + NORM_EPS) * g_ref[...]
    half = hn.shape[1] // 2
    packed_ref[...] = lax.shift_right_logical(_bf16_bits_hi(hn[:, :half]), 16) | _bf16_bits_hi(hn[:, half:])
    x_hi = hn.astype(BF16)
    x_lo = (hn - x_hi.astype(F32)).astype(BF16)
    logits = (jnp.dot(x_hi, w_ref[...], preferred_element_type=F32)
              + jnp.dot(x_hi, wlo_ref[...], preferred_element_type=F32)
              + jnp.dot(x_lo, w_ref[...], preferred_element_type=F32)) + b_ref[...]
    lane = lax.broadcasted_iota(jnp.int32, logits.shape, 1)
    far = float(LANES)
    is_g = lane < N_GROUPS
    g_id = lane.astype(F32)
    gmax = jnp.max(jnp.where(is_g, logits, NEG_INF), axis=1, keepdims=True)
    gsel = jnp.min(jnp.where(is_g & (logits == gmax), g_id, far), axis=1, keepdims=True)
    p_g = 1.0 / jnp.sum(jnp.where(is_g, jnp.exp(logits - gmax), 0.0), axis=1, keepdims=True)
    e_lane = lane - N_GROUPS
    e_id = e_lane.astype(F32)
    e_group = jnp.right_shift(e_lane, 2).astype(F32)
    in_grp = (e_lane >= 0) & (e_lane < N_EXPERTS) & (e_group == gsel)
    emax = jnp.max(jnp.where(in_grp, logits, NEG_INF), axis=1, keepdims=True)
    ee = jnp.where(in_grp, jnp.exp(jnp.where(in_grp, logits, emax) - emax), 0.0)
    ep = ee / jnp.sum(ee, axis=1, keepdims=True)
    p1 = jnp.max(jnp.where(in_grp, ep, -1.0), axis=1, keepdims=True)
    i1 = jnp.min(jnp.where(in_grp & (ep == p1), e_id, far), axis=1, keepdims=True)
    rest = in_grp & (e_id != i1)
    p2 = jnp.max(jnp.where(rest, ep, -1.0), axis=1, keepdims=True)
    i2 = jnp.min(jnp.where(rest & (ep == p2), e_id, far), axis=1, keepdims=True)
    tot = p1 + p2
    ids_ref[...] = jnp.where(lane == 0, i1, jnp.where(lane == 1, i2, 0.0)).astype(jnp.int32)
    wts_ref[...] = jnp.where(lane == 0, p_g * p1 / tot, jnp.where(lane == 1, p_g * p2 / tot, 0.0))


def _router(h, g, w_r, b_r, layer, *, tm=256):
    t, d = h.shape
    out = pl.BlockSpec((tm, LANES), lambda i: (i, 0))
    return pl.pallas_call(
        _router_kernel,
        grid=(t // tm,),
        in_specs=[pl.BlockSpec((tm, d), lambda i: (i, 0)),
                  pl.BlockSpec((None, 1, d), lambda i: (layer, 0, 0)),
                  pl.BlockSpec((None, d, LANES), lambda i: (layer, 0, 0)),
                  pl.BlockSpec((None, d, LANES), lambda i: (layer, 0, 0)),
                  pl.BlockSpec((None, 1, LANES), lambda i: (layer, 0, 0))],
        out_specs=[out, out, pl.BlockSpec((tm, d // 2), lambda i: (i, 0))],
        out_shape=[jax.ShapeDtypeStruct((t, LANES), jnp.int32), jax.ShapeDtypeStruct((t, LANES), F32),
                   jax.ShapeDtypeStruct((t, d // 2), jnp.int32)],
        compiler_params=_params("parallel"),
        name="moe_router",
    )(h, g, *w_r, b_r)


def _row_copy(src_hbm, tok, dst_ref, r, sem):
    return pltpu.make_async_copy(src_hbm.at[pl.ds(tok, 1), :], dst_ref.at[pl.ds(r, 1), :], sem)


GATHER_UNROLL = 8
NORM_CHUNK = 16


def _dispatch_kernel(tok_ref, nvalid_ref, h_hbm, o_ref, rows_ref, sem):
    i = pl.program_id(0)
    n_valid = nvalid_ref[0]
    n_rows = rows_ref.shape[1]
    slot = lax.rem(i, 2)

    def fetch(blk, slot):
        base = blk * n_rows

        def issue(g, _):
            for u in range(GATHER_UNROLL):
                r = g * GATHER_UNROLL + u
                _row_copy(h_hbm, tok_ref[base + r], rows_ref.at[slot], r, sem.at[slot]).start(priority=u % 2)
            return 0

        lax.fori_loop(0, n_rows // GATHER_UNROLL, issue, 0)

    @pl.when(i == 0)
    def _():
        fetch(0, 0)

    @pl.when(i + 1 < n_valid)
    def _():
        fetch(i + 1, 1 - slot)

    @pl.when(i < n_valid)
    def _():
        def drain(r, _):
            _row_copy(h_hbm, 0, rows_ref.at[slot], r, sem.at[slot]).wait()
            return 0

        lax.fori_loop(0, n_rows, drain, 0, unroll=GATHER_UNROLL)

        half = rows_ref.shape[2]

        def unpack(c, _):
            rows = pl.ds(pl.multiple_of(jnp.asarray(c, jnp.int32) * NORM_CHUNK, NORM_CHUNK), NORM_CHUNK)
            word = rows_ref[slot, rows, :]
            o_ref[rows, :half] = lax.bitcast_convert_type(lax.shift_left(word, 16), F32).astype(o_ref.dtype)
            o_ref[rows, half:] = lax.bitcast_convert_type(word & HI16, F32).astype(o_ref.dtype)
            return 0

        lax.fori_loop(0, n_rows // NORM_CHUNK, unpack, 0, unroll=4)


def _dispatch(packed, buf_tok, nvalid, *, n_blocks):
    t, half = packed.shape
    blk = lambda i, tok, nv: (jnp.minimum(i, nv[0] - 1), 0)
    return pl.pallas_call(
        _dispatch_kernel,
        grid_spec=pltpu.PrefetchScalarGridSpec(
            num_scalar_prefetch=2,
            grid=(n_blocks,),
            in_specs=[pl.BlockSpec(memory_space=pl.ANY)],
            out_specs=pl.BlockSpec((MOE_ROWS, 2 * half), blk),
            scratch_shapes=[pltpu.VMEM((2, MOE_ROWS, half), jnp.int32), pltpu.SemaphoreType.DMA((2,))],
        ),
        out_shape=jax.ShapeDtypeStruct((n_blocks * MOE_ROWS, 2 * half), BF16),
        compiler_params=_params("arbitrary"),
        name="moe_dispatch",
    )(buf_tok, nvalid, packed)


def _run_weights(be_ref, first_ref, next_ref, i, pass_id, n_passes, n_blocks, copies, stage):
    @pl.when((pass_id == 0) & (i == 0))
    def _():
        for c in copies(be_ref[0], 0):
            c.start()

    @pl.when(first_ref[i] == 1)
    def _():
        for c in copies(be_ref[i], pass_id):
            c.wait()
        stage()
        nxt = next_ref[i]
        same_pass = nxt < n_blocks
        e_next = jnp.where(same_pass, be_ref[jnp.minimum(nxt, n_blocks - 1)], be_ref[0])
        pass_next = jnp.where(same_pass, pass_id, pass_id + 1)

        @pl.when(same_pass | (pass_id + 1 < n_passes))
        def _():
            for c in copies(e_next, pass_next):
                c.start()


def _expert_up_kernel(be_ref, nvalid_ref, first_ref, next_ref, x_ref, wg_hbm, wu_hbm, o_ref,
                      wg_land, wu_land, wg_bf, wu_bf, sem, *, layer):
    f, i = pl.program_id(0), pl.program_id(1)
    tf = wg_land.shape[1]

    def copies(e, ff_tile):
        cols = pl.ds(pl.multiple_of(ff_tile * tf, tf), tf)
        return (pltpu.make_async_copy(wg_hbm.at[layer, e, :, cols], wg_land, sem.at[0]),
                pltpu.make_async_copy(wu_hbm.at[layer, e, :, cols], wu_land, sem.at[1]))

    def stage():
        wg_bf[...] = wg_land[...].astype(BF16)
        wu_bf[...] = wu_land[...].astype(BF16)

    _run_weights(be_ref, first_ref, next_ref, i, f, pl.num_programs(0), pl.num_programs(1), copies, stage)

    @pl.when(i < nvalid_ref[0])
    def _():
        x = x_ref[...]
        hg = jnp.dot(x, wg_bf[...], preferred_element_type=F32)
        hu = jnp.dot(x, wu_bf[...], preferred_element_type=F32)
        o_ref[...] = (jax.nn.silu(hg) * hu).astype(o_ref.dtype)


def _expert_up(xb, w_gate, w_up, layer, runs, *, n_blocks):
    d = xb.shape[1]
    ff = w_gate.shape[3]
    tf = MOE_FF_TILE
    row_blk = lambda f, i, be, nv, first, nxt: (jnp.minimum(i, nv[0] - 1), 0)
    return pl.pallas_call(
        functools.partial(_expert_up_kernel, layer=layer),
        grid_spec=pltpu.PrefetchScalarGridSpec(
            num_scalar_prefetch=4,
            grid=(ff // tf, n_blocks),
            in_specs=[pl.BlockSpec((MOE_ROWS, d), row_blk), pl.BlockSpec(memory_space=pl.ANY),
                      pl.BlockSpec(memory_space=pl.ANY)],
            out_specs=pl.BlockSpec((MOE_ROWS, tf),
                                   lambda f, i, be, nv, first, nxt: (jnp.minimum(i, nv[0] - 1), f)),
            scratch_shapes=[pltpu.VMEM((d, tf), F32), pltpu.VMEM((d, tf), F32),
                            pltpu.VMEM((d, tf), BF16), pltpu.VMEM((d, tf), BF16),
                            pltpu.SemaphoreType.DMA((2,))],
        ),
        out_shape=jax.ShapeDtypeStruct((n_blocks * MOE_ROWS, ff), BF16),
        compiler_params=_params("arbitrary", "arbitrary"),
        name="moe_expert_up",
    )(*runs, xb, w_gate, w_up)


def _expert_down_kernel(be_ref, nvalid_ref, first_ref, next_ref, a_ref, wd_hbm, o_ref, wd_land, wd_bf, sem,
                        *, layer):
    i = pl.program_id(0)

    def copies(e, _):
        return (pltpu.make_async_copy(wd_hbm.at[layer, e], wd_land, sem.at[0]),)

    def stage():
        wd_bf[...] = wd_land[...].astype(BF16)

    _run_weights(be_ref, first_ref, next_ref, i, 0, 1, pl.num_programs(0), copies, stage)

    @pl.when(i < nvalid_ref[0])
    def _():
        o_ref[...] = jnp.dot(a_ref[...], wd_bf[...], preferred_element_type=F32)


def _expert_down(act, w_down, layer, runs, *, n_blocks):
    ff, d = w_down.shape[2], w_down.shape[3]
    row_blk = lambda i, be, nv, first, nxt: (jnp.minimum(i, nv[0] - 1), 0)
    return pl.pallas_call(
        functools.partial(_expert_down_kernel, layer=layer),
        grid_spec=pltpu.PrefetchScalarGridSpec(
            num_scalar_prefetch=4,
            grid=(n_blocks,),
            in_specs=[pl.BlockSpec((MOE_ROWS, ff), row_blk), pl.BlockSpec(memory_space=pl.ANY)],
            out_specs=pl.BlockSpec((MOE_ROWS, d), row_blk),
            scratch_shapes=[pltpu.VMEM((ff, d), F32), pltpu.VMEM((ff, d), BF16), pltpu.SemaphoreType.DMA((1,))],
        ),
        out_shape=jax.ShapeDtypeStruct((n_blocks * MOE_ROWS, d), F32),
        compiler_params=_params("arbitrary"),
        name="moe_expert_down",
    )(*runs, act, w_down)


def _combine_kernel(pos_ref, h_ref, wts_ref, g_ref, y_hbm, *refs, emit_h):
    o_ref = refs[0] if emit_h else None
    n_ref, y0_ref, y1_ref, sem = refs[int(emit_h):]
    i = pl.program_id(0)
    tm = h_ref.shape[0]
    slot = lax.rem(i, 2)

    def fetch(blk, slot):
        base = blk * tm * EXPERT_TOPK

        def issue(g, _):
            for u in range(GATHER_UNROLL):
                r = g * GATHER_UNROLL + u
                _row_copy(y_hbm, pos_ref[base + EXPERT_TOPK * r], y0_ref.at[slot], r, sem.at[slot]).start(priority=0)
                _row_copy(y_hbm, pos_ref[base + EXPERT_TOPK * r + 1], y1_ref.at[slot], r,
                          sem.at[slot]).start(priority=1)
            return 0

        lax.fori_loop(0, tm // GATHER_UNROLL, issue, 0)

    @pl.when(i == 0)
    def _():
        fetch(0, 0)

    @pl.when(i + 1 < pl.num_programs(0))
    def _():
        fetch(i + 1, 1 - slot)

    def drain(r, _):
        _row_copy(y_hbm, 0, y0_ref.at[slot], r, sem.at[slot]).wait()
        _row_copy(y_hbm, 0, y1_ref.at[slot], r, sem.at[slot]).wait()
        return 0

    lax.fori_loop(0, tm, drain, 0, unroll=GATHER_UNROLL)

    def rows_chunk(c, _):
        rows = pl.ds(pl.multiple_of(jnp.asarray(c, jnp.int32) * NORM_CHUNK, NORM_CHUNK), NORM_CHUNK)
        w = wts_ref[rows, :]
        hv = h_ref[rows, :] + w[:, 0:1] * y0_ref[slot, rows, :] + w[:, 1:2] * y1_ref[slot, rows, :]
        if emit_h:
            o_ref[rows, :] = hv
        hn = hv * lax.rsqrt(jnp.mean(hv * hv, axis=-1, keepdims=True) + NORM_EPS) * g_ref[...]
        n_ref[rows, :] = hn.astype(n_ref.dtype)
        return 0

    lax.fori_loop(0, tm // NORM_CHUNK, rows_chunk, 0, unroll=4)


def _combine(h, wts, y, pos, g, *, norm_dtype, emit_h, tm=128):
    t, d = h.shape
    row = pl.BlockSpec((tm, d), lambda i, pos: (i, 0))
    outs = pl.pallas_call(
        functools.partial(_combine_kernel, emit_h=emit_h),
        grid_spec=pltpu.PrefetchScalarGridSpec(
            num_scalar_prefetch=1,
            grid=(t // tm,),
            in_specs=[row, pl.BlockSpec((tm, LANES), lambda i, pos: (i, 0)),
                      pl.BlockSpec((1, d), lambda i, pos: (0, 0)), pl.BlockSpec(memory_space=pl.ANY)],
            out_specs=[row] * (1 + emit_h),
            scratch_shapes=[pltpu.VMEM((2, tm, d), F32), pltpu.VMEM((2, tm, d), F32),
                            pltpu.SemaphoreType.DMA((2,))],
        ),
        out_shape=([jax.ShapeDtypeStruct((t, d), F32)] if emit_h else [])
        + [jax.ShapeDtypeStruct((t, d), norm_dtype)],
        compiler_params=_params("arbitrary"),
        name="moe_combine",
    )(pos, h, wts, g.reshape(1, d), y)
    return (outs[0], outs[1]) if emit_h else (None, outs[0])


def _hier_moe(h, ffn_norm_g, w_r, b_r, w_gate, w_up, w_down, layer, next_norm_g, *, norm_dtype, emit_h):
    t, d = h.shape
    tk = t * EXPERT_TOPK
    n_blocks = tk // MOE_ROWS + N_EXPERTS
    ids, wts, hn_packed = _router(h, ffn_norm_g, w_r, b_r, layer)
    flat_e = ids[:, :EXPERT_TOPK].reshape(tk)
    onehot = (flat_e[:, None] == jnp.arange(N_EXPERTS, dtype=jnp.int32)[None, :]).astype(jnp.int32)
    csum = jnp.cumsum(onehot, axis=0)
    rank = jnp.sum(onehot * csum, axis=1) - 1
    counts = csum[-1]
    padded = (counts + MOE_ROWS - 1) // MOE_ROWS * MOE_ROWS
    pad_end = jnp.cumsum(padded)
    pad_start = pad_end - padded
    dest = (pad_start[flat_e] + rank).astype(jnp.int32)
    flat_tok = jnp.arange(tk, dtype=jnp.int32) // EXPERT_TOPK
    buf_tok = jnp.zeros((n_blocks * MOE_ROWS,), jnp.int32).at[dest].set(flat_tok)
    nvalid = (pad_end[-1] // MOE_ROWS).astype(jnp.int32).reshape(1)
    blk_start = jnp.minimum(jnp.arange(n_blocks, dtype=jnp.int32), nvalid[0] - 1) * MOE_ROWS
    block_expert = jnp.minimum(jnp.searchsorted(pad_end, blk_start, side='right'), N_EXPERTS - 1).astype(jnp.int32)

    blk = jnp.arange(n_blocks, dtype=jnp.int32)
    prev_expert = jnp.concatenate([jnp.full((1,), -1, jnp.int32), block_expert[:-1]])
    run_first = (blk < nvalid[0]) & (block_expert != prev_expert)
    first_at_or_after = lax.cummin(jnp.where(run_first, blk, n_blocks), reverse=True)
    run_next = jnp.concatenate([first_at_or_after[1:], jnp.full((1,), n_blocks, jnp.int32)])
    runs = (block_expert, nvalid, run_first.astype(jnp.int32), run_next)

    xb = _dispatch(hn_packed, buf_tok, nvalid, n_blocks=n_blocks)
    act = _expert_up(xb, w_gate, w_up, layer, runs, n_blocks=n_blocks)
    y = _expert_down(act, w_down, layer, runs, n_blocks=n_blocks)
    return _combine(h, wts, y, dest, next_norm_g, norm_dtype=norm_dtype, emit_h=emit_h)


ROPE_A, ROPE_B, ROPE_PE = 0, 1, 2


def _in_proj_rope_table(col):
    if OFF_QA <= col < OFF_VA:
        return ROPE_A
    if OFF_QB <= col < OFF_VB:
        return ROPE_B
    if col == OFF_KPE:
        return ROPE_PE
    return None


def _mla_q_rope_table(col):
    return 0 if (col // LANES) % 2 == 1 else None


def kernel(x, positions, attn_norm_g, w_in, q_norm_g, kv_norm_g, wq_b, wkv_b, sinks, w_out_a, w_out_b,
           w_out_c, w_o, ffn_norm_g, w_group, b_group, w_expert, b_expert, w_gate, w_up, w_down,
           final_norm_g):
    batch, seq, d = x.shape
    depth = w_in.shape[0]
    t = batch * seq
    h = x.reshape(t, d)
    pos = positions.reshape(t).astype(F32)

    w_in_t = jnp.swapaxes(w_in, 1, 2)
    q_head = C_NOPE_DIM + C_ROPE_DIM
    wq_pad = jnp.pad(wq_b.reshape(depth, C_Q_RANK, C_HEADS, q_head),
                     ((0, 0), (0, 0), (0, 0), (0, 2 * LANES - q_head))
                     ).reshape(depth, C_Q_RANK, C_HEADS * 2 * LANES)
    wq_pad = (wq_pad * MLA_Q_FACTOR).astype(BF16)
    w_r = jnp.concatenate([w_group, w_expert,
                           jnp.zeros((depth, d, LANES - N_GROUPS - N_EXPERTS), F32)], axis=2)
    w_r_hi = w_r.astype(BF16)
    w_r = (w_r_hi, (w_r - w_r_hi.astype(F32)).astype(BF16))
    b_r = jnp.concatenate([b_group, b_expert,
                           jnp.zeros((depth, LANES - N_GROUPS - N_EXPERTS), F32)], axis=1).reshape(depth, 1, LANES)
    ffn_g = ffn_norm_g.reshape(depth, 1, d)

    c_a, s_a = _rope_tables(pos, A_HEAD_DIM)
    c_b, s_b = _rope_tables(pos, B_HEAD_DIM)
    c_pe, s_pe = _rope_tables(pos, C_ROPE_DIM, keep_lanes=C_ROPE_DIM)
    rope_tables = ((c_a, s_a, A_HEAD_DIM // 2), (c_b, s_b, B_HEAD_DIM // 2), (c_pe, s_pe, C_ROPE_DIM // 2))

    hn = _rmsnorm(h, attn_norm_g[0])
    for l in range(depth):
        z1 = _matmul(hn, w_in_t, l, n_out=Z1_WIDTH, w_is_nk=True, rope_tables=rope_tables,
                     rope_group_table=_in_proj_rope_table, name="in_proj")
        gates = _matmul(hn, w_in_t, l, n_out=3 * d, col_off=OFF_GATES, w_is_nk=True, name="gate_proj")
        o_a = _moba_attention(z1, batch=batch, seq=seq)
        o_b = _swa_attention(z1, sinks[l], batch=batch, seq=seq)
        cq_n = _rmsnorm(z1, q_norm_g[l], col_off=OFF_CQ, width=C_Q_RANK, piece=512)
        ckv_n = _rmsnorm(z1, kv_norm_g[l], col_off=OFF_CKV, width=C_KV_RANK, piece=512)
        q_c = _matmul(cq_n, wq_pad, l, n_out=C_HEADS * 2 * LANES, tn=1024, out_dtype=BF16,
                      rope_tables=rope_tables[ROPE_PE:], rope_group_table=_mla_q_rope_table, name="mla_q_proj")
        kv_c = _matmul(ckv_n, wkv_b, l, n_out=C_HEADS * (C_NOPE_DIM + C_V_DIM), tn=1024, out_dtype=BF16,
                       name="mla_kv_proj")
        o_c = _mla_attention(q_c, kv_c, z1, batch=batch, seq=seq)
        y = _gated_out(o_a, o_b, o_c, gates, w_out_a, w_out_b, w_out_c, l)
        h = _matmul(y, w_o, l, n_out=d, residual=h, name="out_proj")
        last = l == depth - 1
        h, hn = _hier_moe(h, ffn_g, w_r, b_r, w_gate, w_up, w_down, l,
                          final_norm_g if last else attn_norm_g[l + 1],
                          norm_dtype=F32 if last else BF16, emit_h=not last)
    return hn.reshape(batch, seq, d)
```
